```python
import math
import jax, jax.numpy as jnp
from jax import lax
import numpy as np

D_MODEL = 1024
BATCH = 8
SEQ = 2048
DEPTH = 4
DEC_BATCH = 32
DEC_SEQ = 32
PAST_LEN = 1024

CHUNK = 64
N_MIXERS = 4
D_FF = 4 * D_MODEL
PLE_DIM = 256
EPS = 1e-6
POOL_WINDOWS = (2, 4, 8, 16)
POOL_GROUPS = 4
POOL_GC = D_MODEL // POOL_GROUPS
POOL_CTX = max(POOL_WINDOWS) - 1
N_HEADS = 8
N_KV_HEADS = 2
HEAD_DIM = D_MODEL // N_HEADS
GROUP = N_HEADS // N_KV_HEADS
IDX_HEADS = 8
IDX_DIM = 64
TOPK_MAX = 256
Q_BLOCK = 128
REL_BUCKETS = 32
REL_MAX_DIST = 128
DSA_SPLITS = (N_HEADS * HEAD_DIM, N_KV_HEADS * HEAD_DIM, N_KV_HEADS * HEAD_DIM, IDX_HEADS * IDX_DIM, IDX_DIM, IDX_HEADS)
GLA_HEADS = 4
GLA_DK = D_MODEL // 2 // GLA_HEADS
GLA_DV = D_MODEL // GLA_HEADS
GLA_RANK = 16
GLA_GATE_NORM = 16.0
GLA_SPLITS = (GLA_HEADS * GLA_DK, GLA_HEADS * GLA_DK, GLA_HEADS * GLA_DV, GLA_HEADS * GLA_DV, GLA_RANK)
CONV_W = 3

kernel_name = 'hybrid_streaming_encoder_step'

f32 = jnp.float32


def split_last(x, sizes):
    return jnp.split(x, [int(s) for s in np.cumsum(sizes)[:-1]], axis=-1)


def rms_norm(x, g):
    xf = x.astype(f32)
    y = xf * lax.rsqrt(jnp.mean(xf * xf, axis=-1, keepdims=True) + EPS)
    return (y * g.astype(f32)).astype(x.dtype)


def t5_bucket(rel):
    half = REL_BUCKETS // 2
    exact = half // 2
    n = jnp.abs(rel)
    scaled = jnp.log(jnp.maximum(n, 1).astype(f32) / exact) / math.log(REL_MAX_DIST / exact) * (half - exact)
    large = jnp.minimum(exact + scaled.astype(jnp.int32), half - 1)
    return (rel > 0).astype(jnp.int32) * half + jnp.where(n < exact, n, large)


def pool_mixer(xn, prev, pos0, w_pool, b_pool, scale):
    b, t, _ = xn.shape
    xe = jnp.concatenate([prev.astype(xn.dtype), xn], axis=1)
    xf = xe.astype(f32)
    cs = jnp.concatenate([jnp.zeros((b, 1, D_MODEL), f32), jnp.cumsum(xf, axis=1)], axis=1)
    pos = (pos0 + jnp.arange(t)).astype(f32)
    end = cs[:, POOL_CTX + 1:]
    means = []
    for g, w in enumerate(POOL_WINDOWS):
        sl = slice(g * POOL_GC, (g + 1) * POOL_GC)
        win = end[:, :, sl] - cs[:, POOL_CTX + 1 - w:POOL_CTX + 1 - w + t, sl]
        means.append(win / jnp.minimum(float(w), pos + 1.0)[None, :, None])
    d = (jnp.concatenate(means, axis=-1) - xf[:, POOL_CTX:]).astype(xn.dtype)
    d = d.reshape(b, t, POOL_GROUPS, POOL_GC)
    y = jnp.einsum('btgc,gcd->btgd', d, w_pool) + b_pool
    return y.reshape(b, t, D_MODEL) * scale, xe[:, -POOL_CTX:]


def dsa_block(q, iq, iw, q_pos, k_all, v_all, ik_all, k_pos, rel_bias, topk):
    b, nq = q.shape[0], q.shape[1]
    s = jnp.einsum('bqhd,bsd->bqhs', iq.astype(f32), ik_all.astype(f32)) * IDX_DIM ** -0.5
    score = jnp.einsum('bqh,bqhs->bqs', iw.astype(f32), jax.nn.relu(s))
    adm = (k_pos[None, :] // CHUNK) <= (q_pos[:, None] // CHUNK)
    score = jnp.where(adm[None], score, -jnp.inf)
    vals, idx = lax.top_k(score, topk)
    valid = jnp.isfinite(vals)
    take = jax.vmap(lambda rows, ids: rows[ids])
    k_sel = take(k_all, idx)
    v_sel = take(v_all, idx)
    qg = q.reshape(b, nq, N_KV_HEADS, GROUP, HEAD_DIM)
    logits = jnp.einsum('bqhgd,bqnhd->bqhgn', qg, k_sel).astype(f32) * HEAD_DIM ** -0.5
    bias = rel_bias[t5_bucket(k_pos[idx] - q_pos[None, :, None])].astype(f32)
    bias = bias.reshape(b, nq, topk, N_KV_HEADS, GROUP).transpose(0, 1, 3, 4, 2)
    logits = jnp.where(valid[:, :, None, None, :], logits + bias, -jnp.inf)
    probs = jax.nn.softmax(logits, axis=-1).astype(v_sel.dtype)
    o = jnp.einsum('bqhgn,bqnhd->bqhgd', probs, v_sel)
    return o.reshape(b, nq, N_HEADS, HEAD_DIM)


def dsa_mixer(xn, k_past, v_past, ik_past, w_in, w_out, q_gain, k_gain, rel_bias):
    b, t, _ = xn.shape
    past = k_past.shape[1]
    q, k, v, iq, ik, iw = split_last(xn @ w_in, DSA_SPLITS)
    q = rms_norm(q.reshape(b, t, N_HEADS, HEAD_DIM), q_gain)
    k = rms_norm(k.reshape(b, t, N_KV_HEADS, HEAD_DIM), k_gain)
    v = v.reshape(b, t, N_KV_HEADS, HEAD_DIM)
    iq = iq.reshape(b, t, IDX_HEADS, IDX_DIM)
    k_all = jnp.concatenate([k_past.astype(k.dtype), k], axis=1)
    v_all = jnp.concatenate([v_past.astype(v.dtype), v], axis=1)
    ik_all = jnp.concatenate([ik_past.astype(ik.dtype), ik], axis=1)
    n_keys = past + t
    topk = min(TOPK_MAX, n_keys // 4)
    k_pos = jnp.arange(n_keys, dtype=jnp.int32)
    q_pos = past + jnp.arange(t, dtype=jnp.int32)
    qb = min(t, Q_BLOCK)
    nb = t // qb
    blocks = lambda a: a.reshape(b, nb, qb, *a.shape[2:]).swapaxes(0, 1)
    o = lax.map(lambda args: dsa_block(*args, k_all, v_all, ik_all, k_pos, rel_bias, topk),
                (blocks(q), blocks(iq), blocks(iw), q_pos.reshape(nb, qb)))
    o = o.swapaxes(0, 1).reshape(b, t, N_HEADS * HEAD_DIM)
    return o @ w_out, k, v, ik


def gla_chunk(state, inp):
    q, k, v, g = inp
    c = q.shape[2]
    cum = jnp.cumsum(g, axis=2)
    causal = jnp.tril(jnp.ones((c, c), dtype=bool))
    rel = cum[:, :, :, None, :] - cum[:, :, None, :, :]
    decay = jnp.exp(jnp.where(causal[None, None, :, :, None], rel, -jnp.inf))
    scores = jnp.einsum('bhtd,bhsd,bhtsd->bhts', q, k, decay)
    out = jnp.einsum('bhts,bhsv->bhtv', scores, v) + jnp.einsum('bhtd,bhdv->bhtv', q * jnp.exp(cum), state)
    last = cum[:, :, -1:, :]
    state = jnp.exp(last[:, :, 0, :, None]) * state + jnp.einsum('bhsd,bhsv->bhdv', k * jnp.exp(last - cum), v)
    return state, out


def gla_mixer(xn, s0, w_in, w_a2, b_a, o_gain, w_out):
    b, t, _ = xn.shape
    q, k, v, r, a = split_last(xn @ w_in, GLA_SPLITS)
    g = jax.nn.log_sigmoid((a @ w_a2 + b_a).astype(f32)) / GLA_GATE_NORM
    c = min(t, CHUNK)
    n = t // c

    def heads(z, dim):
        return z.astype(f32).reshape(b, n, c, GLA_HEADS, dim).transpose(1, 0, 3, 2, 4)

    s_new, o = lax.scan(gla_chunk, s0.astype(f32),
                        (heads(q, GLA_DK) * GLA_DK ** -0.5, heads(k, GLA_DK), heads(v, GLA_DV), heads(g, GLA_DK)))
    o = o.transpose(1, 0, 3, 2, 4).reshape(b, t, GLA_HEADS, GLA_DV)
    o = rms_norm(o, o_gain) * jax.nn.silu(r.astype(f32)).reshape(b, t, GLA_HEADS, GLA_DV)
    y = o.reshape(b, t, GLA_HEADS * GLA_DV).astype(xn.dtype) @ w_out
    return y, s_new.astype(s0.dtype)


def conv_mixer(xn, prev, w_in, conv_w, conv_b, w_out):
    t = xn.shape[1]
    gate_b, gate_c, h = split_last(xn @ w_in, (D_MODEL, D_MODEL, D_MODEL))
    u = gate_c * h
    ue = jnp.concatenate([prev.astype(u.dtype), u], axis=1)
    conv = conv_b + conv_w[0] * ue[:, 0:t]
    for j in range(1, CONV_W):
        conv = conv + conv_w[j] * ue[:, j:j + t]
    return (gate_b * conv) @ w_out, ue[:, -(CONV_W - 1):]


def run_trunk(x, p, pool_prev, k_past, v_past, ik_past, gla_prev, conv_prev, prm):
    h = x
    pos0 = k_past.shape[1]
    for i in range(DEPTH):
        xn = rms_norm(h, prm['norm_mix'][i])
        kind = i % N_MIXERS
        if kind == 0:
            y, pool_new = pool_mixer(xn, pool_prev, pos0, prm['w_pool'], prm['b_pool'], prm['pool_scale'])
        elif kind == 1:
            y, k_new, v_new, ik_new = dsa_mixer(xn, k_past, v_past, ik_past, prm['w_dsa_in'], prm['w_dsa_out'],
                                                prm['q_norm'], prm['k_norm'], prm['rel_bias'])
        elif kind == 2:
            y, gla_new = gla_mixer(xn, gla_prev, prm['w_gla_in'], prm['w_gla_a2'], prm['b_gla_a'],
                                   prm['gla_norm'], prm['w_gla_out'])
        else:
            y, conv_new = conv_mixer(xn, conv_prev, prm['w_conv_in'], prm['conv_w'], prm['conv_b'], prm['w_conv_out'])
        h = h + y
        hn = rms_norm(h, prm['norm_mlp'][i])
        h = h + jnp.square(jax.nn.relu(hn @ prm['w_mlp1'][i])) @ prm['w_mlp2'][i]
        gate = jax.nn.sigmoid(rms_norm(h, prm['norm_ple'][i]) @ prm['w_ple_gate'][i])
        h = h + (p[i] @ prm['w_ple_proj'][i]) * gate
    return (h, pool_new, k_new, v_new, ik_new, gla_new, conv_new)


def setup_inputs(seed: int = 0) -> dict:
    key = jax.random.key(seed)
    ks = iter(jax.random.split(key, 40))
    nrm = lambda shape, scale: jax.random.normal(next(ks), shape, f32) * scale
    gain = lambda shape: 1.0 + nrm(shape, 0.05)
    D = D_MODEL
    return {
        'x_prompt': nrm((BATCH, SEQ, D), 1.0),
        'x_sample': nrm((DEC_BATCH, DEC_SEQ, D), 1.0),
        'p_prompt': nrm((DEPTH, BATCH, SEQ, PLE_DIM), 1.0),
        'p_sample': nrm((DEPTH, DEC_BATCH, DEC_SEQ, PLE_DIM), 1.0),
        'state_pool': nrm((DEC_BATCH, POOL_CTX, D), 1.0),
        'cache_k': nrm((DEC_BATCH, PAST_LEN, N_KV_HEADS, HEAD_DIM), 1.0),
        'cache_v': nrm((DEC_BATCH, PAST_LEN, N_KV_HEADS, HEAD_DIM), 1.0),
        'cache_idx_k': nrm((DEC_BATCH, PAST_LEN, IDX_DIM), 1.0),
        'state_gla': nrm((DEC_BATCH, GLA_HEADS, GLA_DK, GLA_DV), 0.3),
        'state_conv': nrm((DEC_BATCH, CONV_W - 1, D), 1.0),
        'norm_mix': gain((DEPTH, D)),
        'norm_mlp': gain((DEPTH, D)),
        'norm_ple': gain((DEPTH, D)),
        'w_mlp1': nrm((DEPTH, D, D_FF), D ** -0.5),
        'w_mlp2': nrm((DEPTH, D_FF, D), D_FF ** -0.5),
        'w_ple_proj': nrm((DEPTH, PLE_DIM, D), PLE_DIM ** -0.5),
        'w_ple_gate': nrm((DEPTH, D, D), D ** -0.5),
        'w_pool': nrm((POOL_GROUPS, POOL_GC, POOL_GC), POOL_GC ** -0.5),
        'b_pool': nrm((POOL_GROUPS, POOL_GC), 0.02),
        'pool_scale': gain((D,)),
        'w_dsa_in': nrm((D, sum(DSA_SPLITS)), D ** -0.5),
        'w_dsa_out': nrm((N_HEADS * HEAD_DIM, D), (N_HEADS * HEAD_DIM) ** -0.5),
        'q_norm': gain((HEAD_DIM,)),
        'k_norm': gain((HEAD_DIM,)),
        'rel_bias': nrm((REL_BUCKETS, N_HEADS), 0.5),
        'w_gla_in': nrm((D, sum(GLA_SPLITS)), D ** -0.5),
        'w_gla_a2': nrm((GLA_RANK, GLA_HEADS * GLA_DK), GLA_RANK ** -0.5),
        'b_gla_a': nrm((GLA_HEADS * GLA_DK,), 0.1),
        'gla_norm': gain((GLA_DV,)),
        'w_gla_out': nrm((GLA_HEADS * GLA_DV, D), (GLA_HEADS * GLA_DV) ** -0.5),
        'w_conv_in': nrm((D, 3 * D), D ** -0.5),
        'conv_w': nrm((CONV_W, D), CONV_W ** -0.5),
        'conv_b': nrm((D,), 0.02),
        'w_conv_out': nrm((D, D), D ** -0.5),
    }


def reference(x_prompt, x_sample, p_prompt, p_sample, state_pool, cache_k, cache_v, cache_idx_k, state_gla,
              state_conv, norm_mix, norm_mlp, norm_ple, w_mlp1, w_mlp2, w_ple_proj, w_ple_gate, w_pool, b_pool,
              pool_scale, w_dsa_in, w_dsa_out, q_norm, k_norm, rel_bias, w_gla_in, w_gla_a2, b_gla_a, gla_norm,
              w_gla_out, w_conv_in, conv_w, conv_b, w_conv_out):
    prm = dict(norm_mix=norm_mix, norm_mlp=norm_mlp, norm_ple=norm_ple, w_mlp1=w_mlp1, w_mlp2=w_mlp2,
               w_ple_proj=w_ple_proj, w_ple_gate=w_ple_gate, w_pool=w_pool, b_pool=b_pool, pool_scale=pool_scale,
               w_dsa_in=w_dsa_in, w_dsa_out=w_dsa_out, q_norm=q_norm, k_norm=k_norm, rel_bias=rel_bias,
               w_gla_in=w_gla_in, w_gla_a2=w_gla_a2, b_gla_a=b_gla_a, gla_norm=gla_norm, w_gla_out=w_gla_out,
               w_conv_in=w_conv_in, conv_w=conv_w, conv_b=conv_b, w_conv_out=w_conv_out)
    bp = x_prompt.shape[0]
    dt = x_prompt.dtype
    y_p, pool_p, k_p, v_p, ik_p, gla_p, conv_p = run_trunk(
        x_prompt, p_prompt,
        jnp.zeros((bp, POOL_CTX, D_MODEL), dt),
        jnp.zeros((bp, 0, N_KV_HEADS, HEAD_DIM), dt),
        jnp.zeros((bp, 0, N_KV_HEADS, HEAD_DIM), dt),
        jnp.zeros((bp, 0, IDX_DIM), dt),
        jnp.zeros((bp, GLA_HEADS, GLA_DK, GLA_DV), dt),
        jnp.zeros((bp, CONV_W - 1, D_MODEL), dt),
        prm)
    y_s, pool_s, k_s, v_s, ik_s, gla_s, conv_s = run_trunk(
        x_sample, p_sample, state_pool, cache_k, cache_v, cache_idx_k, state_gla, state_conv, prm)
    return (y_p, y_s, pool_p, pool_s, k_p, v_p, ik_p, k_s, v_s, ik_s, gla_p, gla_s, conv_p, conv_s)
```

```python
import functools

import jax
import jax.numpy as jnp
from jax import lax
from jax.experimental import pallas as pl
from jax.experimental.pallas import tpu as pltpu

F32 = jnp.float32
BF16 = jnp.bfloat16
I32 = jnp.int32

D_MODEL = 1024
D_FF = 4 * D_MODEL
PLE_DIM = 256
EPS = 1e-6
CHUNK = 64
CHUNK_SHIFT = 6
POOL_WINDOWS = (2, 4, 8, 16)
POOL_GC = D_MODEL // len(POOL_WINDOWS)
POOL_CTX = max(POOL_WINDOWS) - 1
N_HEADS = 8
N_KV_HEADS = 2
HEAD_DIM = D_MODEL // N_HEADS
GROUP = N_HEADS // N_KV_HEADS
IDX_HEADS = 8
IDX_DIM = 64
TOPK_MAX = 256
REL_BUCKETS = 32
REL_LOG_EDGES = (12, 16, 23, 32, 46, 64, 91)
REL_FAR_BUCKET = REL_BUCKETS // 2 - 1
GLA_HEADS = 4
GLA_DK = D_MODEL // 2 // GLA_HEADS
GLA_DV = D_MODEL // GLA_HEADS
GLA_RANK = 16
GLA_GATE_NORM = 16.0
GLA_SUB = 16
CONV_W = 3

V7X_VMEM_LIMIT_BYTES = 56 * 1024 * 1024
LANES = 128
INT_MIN = -(2 ** 31)
KEY_TILE = 128

HIGHEST = lax.Precision.HIGHEST


def _params(*sem):
    return pltpu.CompilerParams(dimension_semantics=sem, vmem_limit_bytes=V7X_VMEM_LIMIT_BYTES)


def _const_spec(shape):
    nd = len(shape)
    return pl.BlockSpec(shape, lambda *_: (0,) * nd, pipeline_mode=pl.Buffered(1))


def _rms(x, g):
    return x * lax.rsqrt(jnp.mean(x * x, axis=-1, keepdims=True) + EPS) * g


def _dot(a, b):
    return jnp.dot(a, b, preferred_element_type=F32)


def _dot_nt(a, b):
    return lax.dot_general(a, b, (((1,), (1,)), ((), ())), preferred_element_type=F32)


def _dot_tn(a, b):
    return lax.dot_general(a, b, (((0,), (0,)), ((), ())), preferred_element_type=F32)


def _row_tile(n, want):
    t = min(n, want)
    assert n % t == 0
    return t


def _post_kernel(h_ref, a_ref, p_ref, wout_ref, bout_ref, sout_ref, gmlp_ref, w1_ref, w2_ref, gple_ref, wg_ref,
                 wp_ref, *rest, ff_chunk, with_next):
    if with_next:
        gnext_ref, h_out_ref, xn_out_ref = rest
    else:
        (h_out_ref,) = rest
    y = (_dot(a_ref[...], wout_ref[...]) + bout_ref[...]) * sout_ref[...]
    h1 = h_ref[...] + y
    hn = _rms(h1, gmlp_ref[...]).astype(BF16)
    acc = h1
    for c in range(D_FF // ff_chunk):
        cols = slice(c * ff_chunk, (c + 1) * ff_chunk)
        hid = jnp.square(jnp.maximum(_dot(hn, w1_ref[:, cols]), 0.0)).astype(BF16)
        acc = acc + _dot(hid, w2_ref[cols, :])
    gate = jax.nn.sigmoid(_dot(_rms(acc, gple_ref[...]).astype(BF16), wg_ref[...]))
    h3 = acc + _dot(p_ref[...].astype(BF16), wp_ref[...]) * gate
    h_out_ref[...] = h3
    if with_next:
        xn_out_ref[...] = _rms(h3, gnext_ref[...]).astype(BF16)


def _post_block(h, a, p, lw, gnext):
    n = h.shape[0]
    tm = _row_tile(n, 512)
    with_next = gnext is not None
    row = lambda w: pl.BlockSpec((tm, w), lambda i: (i, 0))
    vec = _const_spec((1, D_MODEL))
    out_shape = [jax.ShapeDtypeStruct((n, D_MODEL), F32)]
    out_specs = [row(D_MODEL)]
    in_specs = [row(D_MODEL), row(D_MODEL), row(PLE_DIM),
                _const_spec((D_MODEL, D_MODEL)), vec, vec, vec,
                _const_spec((D_MODEL, D_FF)), _const_spec((D_FF, D_MODEL)), vec,
                _const_spec((D_MODEL, D_MODEL)), _const_spec((PLE_DIM, D_MODEL))]
    args = [h, a, p, lw["wout"], lw["bout"], lw["sout"], lw["gmlp"], lw["w1"], lw["w2"], lw["gple"], lw["wg"],
            lw["wp"]]
    if with_next:
        out_shape.append(jax.ShapeDtypeStruct((n, D_MODEL), BF16))
        out_specs.append(row(D_MODEL))
        in_specs.append(vec)
        args.append(gnext)
    outs = pl.pallas_call(
        functools.partial(_post_kernel, ff_chunk=1024, with_next=with_next),
        grid=(n // tm,),
        in_specs=in_specs,
        out_specs=out_specs,
        out_shape=out_shape,
        compiler_params=_params("parallel"),
        name="post_block",
    )(*args)
    return (outs[0], outs[1]) if with_next else (outs[0], None)


def _pool_kernel(x_ref, prev_ref, g_ref, a_ref, pool_ref, xe_ref, *, tt, pos0):
    t = pl.program_id(1)
    ctx = POOL_CTX + 1

    @pl.when(t == 0)
    def _():
        xe_ref[0:1, :] = jnp.zeros((1, D_MODEL), F32)
        xe_ref[1:ctx, :] = prev_ref[0]

    xn = _rms(x_ref[0], g_ref[...])
    xe_ref[ctx:ctx + tt, :] = xn
    pos1 = (pos0 + 1 + t * tt + lax.broadcasted_iota(I32, (tt, 1), 0)).astype(F32)
    for g, w in enumerate(POOL_WINDOWS):
        cols = slice(g * POOL_GC, (g + 1) * POOL_GC)
        win = xn[:, cols]
        for j in range(1, w):
            win = win + xe_ref[ctx - j:ctx - j + tt, cols]
        mean = win / jnp.minimum(float(w), pos1)
        a_ref[0, :, cols] = (mean - xn[:, cols]).astype(BF16)
    tail = xe_ref[tt:tt + ctx, :]
    xe_ref[0:ctx, :] = tail

    @pl.when(t == pl.num_programs(1) - 1)
    def _():
        pool_ref[0] = xe_ref[tt + 1:tt + ctx, :]


def _pool_front(x, prev, g, pos0):
    b, t, _ = x.shape
    tt = _row_tile(t, 512)
    return pl.pallas_call(
        functools.partial(_pool_kernel, tt=tt, pos0=pos0),
        grid=(b, t // tt),
        in_specs=[pl.BlockSpec((1, tt, D_MODEL), lambda i, j: (i, j, 0)),
                  pl.BlockSpec((1, POOL_CTX, D_MODEL), lambda i, j: (i, 0, 0)),
                  _const_spec((1, D_MODEL))],
        out_specs=[pl.BlockSpec((1, tt, D_MODEL), lambda i, j: (i, j, 0)),
                   pl.BlockSpec((1, POOL_CTX, D_MODEL), lambda i, j: (i, 0, 0))],
        out_shape=[jax.ShapeDtypeStruct((b, t, D_MODEL), BF16),
                   jax.ShapeDtypeStruct((b, POOL_CTX, D_MODEL), F32)],
        scratch_shapes=[pltpu.VMEM((POOL_CTX + 1 + tt, D_MODEL), F32)],
        compiler_params=_params("arbitrary", "arbitrary"),
        name="pool_front",
    )(x, prev, g)


def _dsa_in_kernel(xn_ref, wq_ref, wk_ref, wv_ref, wiq_ref, wik_ref, wiw_ref, qg_ref, kg_ref,
                   q_ref, k_ref, v_ref, iq_ref, ik_ref, iw_ref):
    x = xn_ref[...]
    q = _dot(x, wq_ref[...])
    for h in range(N_HEADS):
        cols = slice(h * HEAD_DIM, (h + 1) * HEAD_DIM)
        q_ref[:, cols] = _rms(q[:, cols], qg_ref[...]).astype(BF16)
    k = _dot(x, wk_ref[...])
    for h in range(N_KV_HEADS):
        cols = slice(h * HEAD_DIM, (h + 1) * HEAD_DIM)
        k_ref[:, cols] = _rms(k[:, cols], kg_ref[...])
    v_ref[...] = _dot(x, wv_ref[...])
    iq_ref[...] = _dot(x, wiq_ref[...]).astype(BF16)
    ik_ref[...] = _dot(x, wik_ref[...])
    iw_ref[...] = _dot(x, wiw_ref[...])


def _dsa_in(xn, w):
    n = xn.shape[0]
    tm = _row_tile(n, 512)
    kv = N_KV_HEADS * HEAD_DIM
    iqd = IDX_HEADS * IDX_DIM
    row = lambda wd: pl.BlockSpec((tm, wd), lambda i: (i, 0))
    return pl.pallas_call(
        _dsa_in_kernel,
        grid=(n // tm,),
        in_specs=[row(D_MODEL), _const_spec((D_MODEL, D_MODEL)), _const_spec((D_MODEL, kv)),
                  _const_spec((D_MODEL, kv)), _const_spec((D_MODEL, iqd)), _const_spec((D_MODEL, IDX_DIM)),
                  _const_spec((D_MODEL, IDX_HEADS)), _const_spec((1, HEAD_DIM)), _const_spec((1, HEAD_DIM))],
        out_specs=[row(D_MODEL), row(kv), row(kv), row(iqd), row(IDX_DIM), row(IDX_HEADS)],
        out_shape=[jax.ShapeDtypeStruct((n, D_MODEL), BF16), jax.ShapeDtypeStruct((n, kv), F32),
                   jax.ShapeDtypeStruct((n, kv), F32), jax.ShapeDtypeStruct((n, iqd), BF16),
                   jax.ShapeDtypeStruct((n, IDX_DIM), F32), jax.ShapeDtypeStruct((n, IDX_HEADS), F32)],
        compiler_params=_params("parallel"),
        name="dsa_in",
    )(xn, w["wq"], w["wk"], w["wv"], w["wiq"], w["wik"], w["wiw"], w["qg"], w["kg"])


def _dsa_attn_kernel(rb_ref, q_ref, iq_ref, iwt_ref, kb_ref, vt_ref, ik_ref, o_ref,
                     key_ref, msk_ref, lg_ref, bias_ref, j_ref, *, past, n_keys, topk, idx_bits):
    kt_sz = KEY_TILE
    qb = KEY_TILE
    b = pl.program_id(0)
    i = pl.program_id(1)
    q0 = past + i * qb
    kend = jnp.minimum(n_keys, (((q0 + qb - 1) >> CHUNK_SHIFT) + 1) * CHUNK)
    nkt = (kend + kt_sz - 1) // kt_sz
    row_i = lax.broadcasted_iota(I32, (kt_sz, qb), 0)
    col_i = lax.broadcasted_iota(I32, (kt_sz, qb), 1)

    @pl.when((b == 0) & (i == 0))
    def _():
        for w in range(2):
            rel = row_i - col_i + (w - 1) * kt_sz
            n = jnp.abs(rel)
            log_bucket = jnp.full_like(n, REL_BUCKETS // 4)
            for edge in REL_LOG_EDGES:
                log_bucket = log_bucket + (n >= edge).astype(I32)
            bucket = jnp.where(n < REL_BUCKETS // 4, n, log_bucket) + jnp.where(rel > 0, REL_BUCKETS // 2, 0)
            for h in range(N_HEADS):
                val = jnp.zeros((kt_sz, qb), F32)
                for bk in range(REL_BUCKETS):
                    val = jnp.where(bucket == bk, rb_ref[bk, h], val)
                bias_ref[h, w] = val

    iw = iwt_ref[0]
    iq = iq_ref[0]
    iq_heads = [iq[:, h * IDX_DIM:(h + 1) * IDX_DIM] for h in range(IDX_HEADS)]
    q_chunk = (q0 + col_i) >> CHUNK_SHIFT

    def tile_rows(jt):
        return pl.ds(pl.multiple_of(jt * kt_sz, kt_sz), kt_sz)

    def score_tile(jt, carry):
        ikt = ik_ref[0, tile_rows(jt), :]
        s = jnp.zeros((kt_sz, qb), F32)
        for h in range(IDX_HEADS):
            s = s + iw[h:h + 1, :] * jnp.maximum(_dot_nt(ikt, iq_heads[h]), 0.0)
        s = s * (IDX_DIM ** -0.5)
        bits = pltpu.bitcast(s, I32)
        key = bits ^ ((bits >> 31) & 0x7FFFFFFF)
        kpos = jt * kt_sz + row_i
        adm = ((kpos >> CHUNK_SHIFT) <= q_chunk) & (kpos < n_keys)
        key_ref[tile_rows(jt), :] = jnp.where(adm, key, INT_MIN)
        return carry

    lax.fori_loop(0, nkt, score_tile, 0)

    def count(pred):
        def body(jt, acc):
            hit = jnp.where(pred(key_ref[tile_rows(jt), :], jt), 1.0, 0.0)
            return acc + jnp.sum(hit.reshape(kt_sz // 8, 8, qb), axis=0)

        acc = lax.fori_loop(0, nkt, body, jnp.zeros((8, qb), F32))
        return jnp.sum(acc, axis=0, keepdims=True)

    kf = float(topk)
    thr0 = jnp.where(count(lambda kk, jt: kk >= 0) >= kf, 0, INT_MIN).astype(I32)

    def bisect(p, thr):
        cand = thr | jnp.left_shift(jnp.int32(1), 30 - p)
        return jnp.where(count(lambda kk, jt: kk >= cand) >= kf, cand, thr)

    thr = lax.fori_loop(0, 31, bisect, thr0)
    n_gt = count(lambda kk, jt: kk > thr)
    n_ge = count(lambda kk, jt: kk >= thr)
    need = kf - n_gt
    has_thr = thr > INT_MIN
    j_ref[...] = jnp.where(has_thr, n_keys, -1).astype(I32)
    tied_cut = jnp.max(jnp.where(has_thr & (n_ge > kf), 1, 0)) > 0

    @pl.when(tied_cut)
    def _():
        lo = jnp.zeros((1, qb), I32)
        for bit in range(idx_bits - 1, -1, -1):
            cand = lo + (1 << bit)
            below = count(lambda kk, jt: (kk == thr) & ((jt * kt_sz + row_i) < cand))
            lo = jnp.where(below < need, cand, lo)
        j_ref[...] = jnp.where(has_thr, lo, -1)

    j_last = j_ref[...]

    def mask_tile(jt, carry):
        kk = key_ref[tile_rows(jt), :]
        sel = (kk > thr) | ((kk == thr) & ((jt * kt_sz + row_i) <= j_last))
        msk_ref[tile_rows(jt), :] = jnp.where(sel, 0.0, -jnp.inf)
        return carry

    lax.fori_loop(0, nkt, mask_tile, 0)

    ones_rows = jnp.ones((HEAD_DIM, kt_sz), BF16)
    for h in range(N_HEADS):
        kvh = h // GROUP
        qh = q_ref[0, :, h * HEAD_DIM:(h + 1) * HEAD_DIM]
        far_bias = rb_ref[REL_FAR_BUCKET, h]

        def logits_tile(jt, m):
            kt = kb_ref[0, tile_rows(jt), kvh * HEAD_DIM:(kvh + 1) * HEAD_DIM]
            near = jt - (nkt - 2)
            bias = jnp.where(near >= 0, bias_ref[h, jnp.maximum(near, 0)], far_bias)
            lg = _dot_nt(kt, qh) * (HEAD_DIM ** -0.5) + bias + msk_ref[tile_rows(jt), :]
            lg_ref[tile_rows(jt), :] = lg
            return jnp.maximum(m, jnp.max(lg.reshape(kt_sz // 8, 8, qb), axis=0))

        m8 = lax.fori_loop(0, nkt, logits_tile, jnp.full((8, qb), -jnp.inf, F32))
        m = jnp.max(m8, axis=0, keepdims=True)

        def pv_tile(jt, acc):
            p = jnp.exp(lg_ref[tile_rows(jt), :] - m).astype(BF16)
            vt = vt_ref[0, kvh, jt]
            return acc + _dot(jnp.concatenate([vt, ones_rows], axis=0), p)

        acc = lax.fori_loop(0, nkt, pv_tile, jnp.zeros((2 * HEAD_DIM, qb), F32))
        o_t = acc[:HEAD_DIM, :] / acc[HEAD_DIM:HEAD_DIM + 1, :]
        o_ref[0, :, h * HEAD_DIM:(h + 1) * HEAD_DIM] = o_t.T.astype(BF16)


def _dsa_attn(rel_bias, q, iq, iwt, kb, vt, ikb, *, past, n_keys):
    b, tq, _ = q.shape
    lp = kb.shape[1]
    qb = KEY_TILE
    assert tq % qb == 0 and lp % KEY_TILE == 0 and past % KEY_TILE == 0
    topk = min(TOPK_MAX, n_keys // 4)
    kv = N_KV_HEADS * HEAD_DIM
    return pl.pallas_call(
        functools.partial(_dsa_attn_kernel, past=past, n_keys=n_keys, topk=topk,
                          idx_bits=max(1, (lp - 1).bit_length())),
        grid=(b, tq // qb),
        in_specs=[pl.BlockSpec(memory_space=pltpu.SMEM),
                  pl.BlockSpec((1, qb, D_MODEL), lambda i, j: (i, j, 0)),
                  pl.BlockSpec((1, qb, IDX_HEADS * IDX_DIM), lambda i, j: (i, j, 0)),
                  pl.BlockSpec((1, IDX_HEADS, qb), lambda i, j: (i, 0, j)),
                  pl.BlockSpec((1, lp, kv), lambda i, j: (i, 0, 0)),
                  pl.BlockSpec((1, N_KV_HEADS, lp // KEY_TILE, HEAD_DIM, KEY_TILE), lambda i, j: (i, 0, 0, 0, 0)),
                  pl.BlockSpec((1, lp, IDX_DIM), lambda i, j: (i, 0, 0))],
        out_specs=pl.BlockSpec((1, qb, D_MODEL), lambda i, j: (i, j, 0)),
        out_shape=jax.ShapeDtypeStruct((b, tq, D_MODEL), BF16),
        scratch_shapes=[pltpu.VMEM((lp, qb), I32), pltpu.VMEM((lp, qb), F32), pltpu.VMEM((lp, qb), F32),
                        pltpu.VMEM((N_HEADS, 2, KEY_TILE, qb), F32), pltpu.VMEM((1, qb), I32)],
        compiler_params=_params("arbitrary", "arbitrary"),
        name="dsa_attn",
    )(rel_bias, q, iq, iwt, kb, vt, ikb)


def _dsa_mixer(xn, k_past, v_past, ik_past, w, b, t):
    past = k_past.shape[1]
    kv = N_KV_HEADS * HEAD_DIM
    q, k, v, iq, ik, iw = _dsa_in(xn, w)
    n_keys = past + t
    tq = -(-t // KEY_TILE) * KEY_TILE
    lp = -(-n_keys // KEY_TILE) * KEY_TILE
    pad_q = lambda z: jnp.pad(z.reshape(b, t, z.shape[-1]), ((0, 0), (0, tq - t), (0, 0)))
    keys = lambda old, new: jnp.pad(
        jnp.concatenate([old.reshape(b, past, new.shape[-1]).astype(BF16),
                         new.reshape(b, t, new.shape[-1]).astype(BF16)], axis=1),
        ((0, 0), (0, lp - n_keys), (0, 0)))
    kb = keys(k_past, k)
    vt = keys(v_past, v).reshape(b, lp // KEY_TILE, KEY_TILE, N_KV_HEADS, HEAD_DIM).transpose(0, 3, 1, 4, 2)
    ikb = keys(ik_past, ik)
    o = _dsa_attn(w["rel_bias"], pad_q(q), pad_q(iq), pad_q(iw).transpose(0, 2, 1), kb, vt, ikb,
                  past=past, n_keys=n_keys)
    o = o[:, :t].reshape(b * t, D_MODEL)
    return (o, k.reshape(b, t, N_KV_HEADS, HEAD_DIM), v.reshape(b, t, N_KV_HEADS, HEAD_DIM),
            ik.reshape(b, t, IDX_DIM))


def _gla_in_kernel(xn_ref, wq_ref, wk_ref, wv_ref, wr_ref, wa_ref, wa2_ref, ba_ref,
                   q_ref, k_ref, v_ref, r_ref, g_ref):
    x = xn_ref[...]
    q_ref[...] = _dot(x, wq_ref[...]) * (GLA_DK ** -0.5)
    k_ref[...] = _dot(x, wk_ref[...])
    v_ref[...] = _dot(x, wv_ref[...]).astype(BF16)
    r_ref[...] = _dot(x, wr_ref[...])
    z = _dot(_dot(x, wa_ref[...]).astype(BF16), wa2_ref[...]) + ba_ref[...]
    g_ref[...] = (jnp.minimum(z, 0.0) - jnp.log1p(jnp.exp(-jnp.abs(z)))) * (1.0 / GLA_GATE_NORM)


def _gla_in(xn, w):
    n = xn.shape[0]
    tm = _row_tile(n, 512)
    qk = GLA_HEADS * GLA_DK
    row = lambda wd: pl.BlockSpec((tm, wd), lambda i: (i, 0))
    return pl.pallas_call(
        _gla_in_kernel,
        grid=(n // tm,),
        in_specs=[row(D_MODEL), _const_spec((D_MODEL, qk)), _const_spec((D_MODEL, qk)),
                  _const_spec((D_MODEL, D_MODEL)), _const_spec((D_MODEL, D_MODEL)),
                  _const_spec((D_MODEL, GLA_RANK)), _const_spec((GLA_RANK, qk)), _const_spec((1, qk))],
        out_specs=[row(qk), row(qk), row(D_MODEL), row(D_MODEL), row(qk)],
        out_shape=[jax.ShapeDtypeStruct((n, qk), F32), jax.ShapeDtypeStruct((n, qk), F32),
                   jax.ShapeDtypeStruct((n, D_MODEL), BF16), jax.ShapeDtypeStruct((n, D_MODEL), F32),
                   jax.ShapeDtypeStruct((n, qk), F32)],
        compiler_params=_params("parallel"),
        name="gla_in",
    )(xn, w["wq"], w["wk"], w["wv"], w["wr"], w["wa"], w["wa2"], w["ba"])


def _gla_core_kernel(q_ref, k_ref, g_ref, v_ref, r_ref, s0_ref, gain_ref, a_ref, sout_ref, st_ref, *, tt):
    sb = GLA_SUB
    t = pl.program_id(1)

    @pl.when(t == 0)
    def _():
        for h in range(GLA_HEADS):
            st_ref[h] = s0_ref[0, h].T

    tri = lax.broadcasted_iota(I32, (sb, sb), 0) >= lax.broadcasted_iota(I32, (sb, sb), 1)
    tri_f = tri.astype(F32)
    gain = gain_ref[...]

    def sub_block(si, carry):
        rows = pl.ds(pl.multiple_of(si * sb, sb), sb)
        for h in range(GLA_HEADS):
            dk = slice(h * GLA_DK, (h + 1) * GLA_DK)
            dv = slice(h * GLA_DV, (h + 1) * GLA_DV)
            q = q_ref[0, rows, dk]
            k = k_ref[0, rows, dk]
            v = v_ref[0, rows, dv]
            cum = jnp.dot(tri_f, g_ref[0, rows, dk], precision=HIGHEST, preferred_element_type=F32)
            decay = jnp.exp(jnp.minimum(cum[:, None, :] - cum[None, :, :], 0.0))
            sc = jnp.sum(decay * q[:, None, :] * k[None, :, :], axis=-1)
            sc = jnp.where(tri, sc, 0.0)
            st = st_ref[h]
            o = _dot(sc.astype(BF16), v) + _dot_nt((q * jnp.exp(cum)).astype(BF16), st.astype(BF16))
            last = cum[sb - 1:sb, :]
            k_dec = (k * jnp.exp(last - cum)).astype(BF16)
            st_ref[h] = st * jnp.exp(last) + _dot_tn(v, k_dec)
            r = r_ref[0, rows, dv]
            a_ref[0, rows, dv] = (_rms(o, gain) * (r * jax.nn.sigmoid(r))).astype(BF16)
        return carry

    lax.fori_loop(0, tt // sb, sub_block, 0)

    @pl.when(t == pl.num_programs(1) - 1)
    def _():
        for h in range(GLA_HEADS):
            sout_ref[0, h] = st_ref[h].T


def _gla_core(q, k, g, v, r, s0, gain):
    b, t, _ = q.shape
    tt = _row_tile(t, 256)
    assert tt % GLA_SUB == 0
    qk = GLA_HEADS * GLA_DK
    tok = lambda wd: pl.BlockSpec((1, tt, wd), lambda i, j: (i, j, 0))
    st_spec = pl.BlockSpec((1, GLA_HEADS, GLA_DK, GLA_DV), lambda i, j: (i, 0, 0, 0))
    return pl.pallas_call(
        functools.partial(_gla_core_kernel, tt=tt),
        grid=(b, t // tt),
        in_specs=[tok(qk), tok(qk), tok(qk), tok(D_MODEL), tok(D_MODEL), st_spec, _const_spec((1, GLA_DV))],
        out_specs=[tok(D_MODEL), st_spec],
        out_shape=[jax.ShapeDtypeStruct((b, t, D_MODEL), BF16),
                   jax.ShapeDtypeStruct((b, GLA_HEADS, GLA_DK, GLA_DV), F32)],
        scratch_shapes=[pltpu.VMEM((GLA_HEADS, GLA_DV, GLA_DK), F32)],
        compiler_params=_params("arbitrary", "arbitrary"),
        name="gla_core",
    )(q, k, g, v, r, s0, gain)


def _conv_kernel(xn_ref, wb_ref, wc_ref, wh_ref, prev_ref, cw_ref, cb_ref, a_ref, new_ref, ue_ref, *, tt):
    t = pl.program_id(1)
    pad = 8
    lo = pad - (CONV_W - 1)

    @pl.when(t == 0)
    def _():
        ue_ref[lo:pad, :] = prev_ref[0]

    x = xn_ref[0]
    u = _dot(x, wc_ref[...]) * _dot(x, wh_ref[...])
    ue_ref[pad:pad + tt, :] = u
    conv = cb_ref[...] + cw_ref[CONV_W - 1:CONV_W, :] * u
    for j in range(CONV_W - 1):
        conv = conv + cw_ref[j:j + 1, :] * ue_ref[lo + j:lo + j + tt, :]
    a_ref[0] = (_dot(x, wb_ref[...]) * conv).astype(BF16)
    tail = ue_ref[lo + tt:pad + tt, :]
    ue_ref[lo:pad, :] = tail

    @pl.when(t == pl.num_programs(1) - 1)
    def _():
        new_ref[0] = tail


def _conv_front(xn, prev, w):
    b, t, _ = xn.shape
    tt = _row_tile(t, 512)
    tok = pl.BlockSpec((1, tt, D_MODEL), lambda i, j: (i, j, 0))
    st = pl.BlockSpec((1, CONV_W - 1, D_MODEL), lambda i, j: (i, 0, 0))
    sq = _const_spec((D_MODEL, D_MODEL))
    return pl.pallas_call(
        functools.partial(_conv_kernel, tt=tt),
        grid=(b, t // tt),
        in_specs=[tok, sq, sq, sq, st, _const_spec((CONV_W, D_MODEL)), _const_spec((1, D_MODEL))],
        out_specs=[tok, st],
        out_shape=[jax.ShapeDtypeStruct((b, t, D_MODEL), BF16),
                   jax.ShapeDtypeStruct((b, CONV_W - 1, D_MODEL), F32)],
        scratch_shapes=[pltpu.VMEM((8 + tt, D_MODEL), F32)],
        compiler_params=_params("arbitrary", "arbitrary"),
        name="conv_front",
    )(xn, w["wb"], w["wc"], w["wh"], prev, w["cw"], w["cb"])


def _trunk(x, p, pool_prev, k_past, v_past, ik_past, gla_prev, conv_prev, w):
    b, t, _ = x.shape
    n = b * t
    past = k_past.shape[1]
    flat = lambda z: z.reshape(n, z.shape[-1])
    post = w["post"]

    a, pool_new = _pool_front(x, pool_prev, w["norm_mix"][0], past)
    h, xn = _post_block(flat(x), flat(a), flat(p[0]), post[0], w["norm_mix"][1])

    a, k_new, v_new, ik_new = _dsa_mixer(xn, k_past, v_past, ik_past, w["dsa"], b, t)
    h, xn = _post_block(h, a, flat(p[1]), post[1], w["norm_mix"][2])

    q, k, v, r, g = _gla_in(xn, w["gla"])
    seq = lambda z: z.reshape(b, t, z.shape[-1])
    a, gla_new = _gla_core(seq(q), seq(k), seq(g), seq(v), seq(r), gla_prev, w["gla"]["gain"])
    h, xn = _post_block(h, flat(a), flat(p[2]), post[2], w["norm_mix"][3])

    a, conv_new = _conv_front(seq(xn), conv_prev, w["conv"])
    h, _ = _post_block(h, flat(a), flat(p[3]), post[3], None)
    return h.reshape(b, t, D_MODEL), pool_new, k_new, v_new, ik_new, gla_new, conv_new


def _prepare_weights(norm_mix, norm_mlp, norm_ple, w_mlp1, w_mlp2, w_ple_proj, w_ple_gate, w_pool, b_pool,
                     pool_scale, w_dsa_in, w_dsa_out, q_norm, k_norm, rel_bias, w_gla_in, w_gla_a2, b_gla_a,
                     gla_norm, w_gla_out, w_conv_in, conv_w, conv_b, w_conv_out):
    bf = lambda z: z.astype(BF16)
    vec = lambda z: z.reshape(1, -1).astype(F32)
    depth = norm_mix.shape[0]
    w_pool_bd = jax.scipy.linalg.block_diag(*[w_pool[g] for g in range(w_pool.shape[0])])
    wouts = [w_pool_bd, w_dsa_out, w_gla_out, w_conv_out]
    zeros = jnp.zeros((1, D_MODEL), F32)
    ones = jnp.ones((1, D_MODEL), F32)
    bouts = [vec(b_pool), zeros, zeros, zeros]
    souts = [vec(pool_scale), ones, ones, ones]
    post = [dict(wout=bf(wouts[i]), bout=bouts[i], sout=souts[i], gmlp=vec(norm_mlp[i]), w1=bf(w_mlp1[i]),
                 w2=bf(w_mlp2[i]), gple=vec(norm_ple[i]), wg=bf(w_ple_gate[i]), wp=bf(w_ple_proj[i]))
            for i in range(depth)]

    def split(wm, sizes):
        out, o = [], 0
        for s in sizes:
            out.append(bf(wm[:, o:o + s]))
            o += s
        return out

    kv = N_KV_HEADS * HEAD_DIM
    wq, wk, wv, wiq, wik, wiw = split(w_dsa_in, (D_MODEL, kv, kv, IDX_HEADS * IDX_DIM, IDX_DIM, IDX_HEADS))
    dsa = dict(wq=wq, wk=wk, wv=wv, wiq=wiq, wik=wik, wiw=wiw, qg=vec(q_norm), kg=vec(k_norm),
               rel_bias=rel_bias.astype(F32))
    qk = GLA_HEADS * GLA_DK
    gq, gk, gv, gr, ga = split(w_gla_in, (qk, qk, D_MODEL, D_MODEL, GLA_RANK))
    gla = dict(wq=gq, wk=gk, wv=gv, wr=gr, wa=ga, wa2=bf(w_gla_a2), ba=vec(b_gla_a), gain=vec(gla_norm))
    cb_, cc_, ch_ = split(w_conv_in, (D_MODEL, D_MODEL, D_MODEL))
    conv = dict(wb=cb_, wc=cc_, wh=ch_, cw=conv_w.astype(F32), cb=vec(conv_b))
    return dict(post=post, norm_mix=[vec(norm_mix[i]) for i in range(depth)], dsa=dsa, gla=gla, conv=conv)


def kernel(x_prompt, x_sample, p_prompt, p_sample, state_pool, cache_k, cache_v, cache_idx_k, state_gla, state_conv, norm_mix, norm_mlp, norm_ple, w_mlp1, w_mlp2, w_ple_proj, w_ple_gate, w_pool, b_pool, pool_scale, w_dsa_in, w_dsa_out, q_norm, k_norm, rel_bias, w_gla_in, w_gla_a2, b_gla_a, gla_norm, w_gla_out, w_conv_in, conv_w, conv_b, w_conv_out):
    w = _prepare_weights(norm_mix, norm_mlp, norm_ple, w_mlp1, w_mlp2, w_ple_proj, w_ple_gate, w_pool, b_pool,
                         pool_scale, w_dsa_in, w_dsa_out, q_norm, k_norm, rel_bias, w_gla_in, w_gla_a2, b_gla_a,
                         gla_norm, w_gla_out, w_conv_in, conv_w, conv_b, w_conv_out)
    bp = x_prompt.shape[0]
    dt = x_prompt.dtype
    y_p, pool_p, k_p, v_p, ik_p, gla_p, conv_p = _trunk(
        x_prompt, p_prompt,
        jnp.zeros((bp, POOL_CTX, D_MODEL), dt),
        jnp.zeros((bp, 0, N_KV_HEADS, HEAD_DIM), dt),
        jnp.zeros((bp, 0, N_KV_HEADS, HEAD_DIM), dt),
        jnp.zeros((bp, 0, IDX_DIM), dt),
        jnp.zeros((bp, GLA_HEADS, GLA_DK, GLA_DV), dt),
        jnp.zeros((bp, CONV_W - 1, D_MODEL), dt),
        w)
    y_s, pool_s, k_s, v_s, ik_s, gla_s, conv_s = _trunk(
        x_sample, p_sample, state_pool, cache_k, cache_v, cache_idx_k, state_gla, state_conv, w)
    return (y_p, y_s, pool_p, pool_s, k_p, v_p, ik_p, k_s, v_s, ik_s, gla_p, gla_s, conv_p, conv_s)
```

```python
import functools

import jax
import jax.numpy as jnp
from jax import lax
from jax.experimental import pallas as pl
from jax.experimental.pallas import tpu as pltpu

F32 = jnp.float32
BF16 = jnp.bfloat16
I32 = jnp.int32

D_MODEL = 1024
D_FF = 4 * D_MODEL
PLE_DIM = 256
EPS = 1e-6
CHUNK = 64
CHUNK_SHIFT = 6
POOL_WINDOWS = (2, 4, 8, 16)
POOL_GC = D_MODEL // len(POOL_WINDOWS)
POOL_CTX = max(POOL_WINDOWS) - 1
N_HEADS = 8
N_KV_HEADS = 2
HEAD_DIM = D_MODEL // N_HEADS
GROUP = N_HEADS // N_KV_HEADS
IDX_HEADS = 8
IDX_DIM = 64
TOPK_MAX = 256
REL_BUCKETS = 32
REL_LOG_EDGES = (12, 16, 23, 32, 46, 64, 91)
REL_FAR_BUCKET = REL_BUCKETS // 2 - 1
GLA_HEADS = 4
GLA_DK = D_MODEL // 2 // GLA_HEADS
GLA_DV = D_MODEL // GLA_HEADS
GLA_RANK = 16
GLA_GATE_NORM = 16.0
GLA_SUB = 16
CONV_W = 3

V7X_VMEM_LIMIT_BYTES = 56 * 1024 * 1024
LANES = 128
INT_MIN = -(2 ** 31)
QUERY_BLOCK = 128
KEY_TILE = 256
NEAR_SLOTS = KEY_TILE // QUERY_BLOCK + 2

HIGHEST = lax.Precision.HIGHEST


def _params(*sem):
    return pltpu.CompilerParams(dimension_semantics=sem, vmem_limit_bytes=V7X_VMEM_LIMIT_BYTES)


def _const_spec(shape):
    nd = len(shape)
    return pl.BlockSpec(shape, lambda *_: (0,) * nd, pipeline_mode=pl.Buffered(1))


def _rms(x, g):
    return x * lax.rsqrt(jnp.mean(x * x, axis=-1, keepdims=True) + EPS) * g


def _dot(a, b):
    return jnp.dot(a, b, preferred_element_type=F32)


def _dot_nt(a, b):
    return lax.dot_general(a, b, (((1,), (1,)), ((), ())), preferred_element_type=F32)


def _dot_tn(a, b):
    return lax.dot_general(a, b, (((0,), (0,)), ((), ())), preferred_element_type=F32)


def _fold_rows(x, op):
    n = x.shape[0] // 8
    assert n & (n - 1) == 0
    parts = x.reshape(n, 8, x.shape[1])
    while n > 1:
        n //= 2
        parts = op(parts[:n], parts[n:])
    return parts[0]


def _row_tile(n, want):
    t = min(n, want)
    assert n % t == 0
    return t


def _post_kernel(h_ref, a_ref, p_ref, wout_ref, bout_ref, sout_ref, gmlp_ref, w1_ref, w2_ref, gple_ref, wg_ref,
                 wp_ref, *rest, ff_chunk, with_next):
    if with_next:
        gnext_ref, h_out_ref, xn_out_ref = rest
    else:
        (h_out_ref,) = rest
    y = (_dot(a_ref[...], wout_ref[...]) + bout_ref[...]) * sout_ref[...]
    h1 = h_ref[...] + y
    hn = _rms(h1, gmlp_ref[...]).astype(BF16)
    acc = h1
    for c in range(D_FF // ff_chunk):
        cols = slice(c * ff_chunk, (c + 1) * ff_chunk)
        hid = jnp.square(jnp.maximum(_dot(hn, w1_ref[:, cols]), 0.0)).astype(BF16)
        acc = acc + _dot(hid, w2_ref[cols, :])
    gate = jax.nn.sigmoid(_dot(_rms(acc, gple_ref[...]).astype(BF16), wg_ref[...]))
    h3 = acc + _dot(p_ref[...].astype(BF16), wp_ref[...]) * gate
    h_out_ref[...] = h3
    if with_next:
        xn_out_ref[...] = _rms(h3, gnext_ref[...]).astype(BF16)


def _post_block(h, a, p, lw, gnext):
    n = h.shape[0]
    tm = _row_tile(n, 512)
    with_next = gnext is not None
    row = lambda w: pl.BlockSpec((tm, w), lambda i: (i, 0))
    vec = _const_spec((1, D_MODEL))
    out_shape = [jax.ShapeDtypeStruct((n, D_MODEL), F32)]
    out_specs = [row(D_MODEL)]
    in_specs = [row(D_MODEL), row(D_MODEL), row(PLE_DIM),
                _const_spec((D_MODEL, D_MODEL)), vec, vec, vec,
                _const_spec((D_MODEL, D_FF)), _const_spec((D_FF, D_MODEL)), vec,
                _const_spec((D_MODEL, D_MODEL)), _const_spec((PLE_DIM, D_MODEL))]
    args = [h, a, p, lw["wout"], lw["bout"], lw["sout"], lw["gmlp"], lw["w1"], lw["w2"], lw["gple"], lw["wg"],
            lw["wp"]]
    if with_next:
        out_shape.append(jax.ShapeDtypeStruct((n, D_MODEL), BF16))
        out_specs.append(row(D_MODEL))
        in_specs.append(vec)
        args.append(gnext)
    outs = pl.pallas_call(
        functools.partial(_post_kernel, ff_chunk=1024, with_next=with_next),
        grid=(n // tm,),
        in_specs=in_specs,
        out_specs=out_specs,
        out_shape=out_shape,
        compiler_params=_params("parallel"),
        name="post_block",
    )(*args)
    return (outs[0], outs[1]) if with_next else (outs[0], None)


def _pool_kernel(x_ref, prev_ref, g_ref, a_ref, pool_ref, xe_ref, *, tt, pos0):
    t = pl.program_id(1)
    ctx = POOL_CTX + 1

    @pl.when(t == 0)
    def _():
        xe_ref[0:1, :] = jnp.zeros((1, D_MODEL), F32)
        xe_ref[1:ctx, :] = prev_ref[0]

    xn = _rms(x_ref[0], g_ref[...])
    xe_ref[ctx:ctx + tt, :] = xn
    pos1 = (pos0 + 1 + t * tt + lax.broadcasted_iota(I32, (tt, 1), 0)).astype(F32)
    for g, w in enumerate(POOL_WINDOWS):
        cols = slice(g * POOL_GC, (g + 1) * POOL_GC)
        win = xn[:, cols]
        for j in range(1, w):
            win = win + xe_ref[ctx - j:ctx - j + tt, cols]
        mean = win / jnp.minimum(float(w), pos1)
        a_ref[0, :, cols] = (mean - xn[:, cols]).astype(BF16)
    tail = xe_ref[tt:tt + ctx, :]
    xe_ref[0:ctx, :] = tail

    @pl.when(t == pl.num_programs(1) - 1)
    def _():
        pool_ref[0] = xe_ref[tt + 1:tt + ctx, :]


def _pool_front(x, prev, g, pos0):
    b, t, _ = x.shape
    tt = _row_tile(t, 512)
    return pl.pallas_call(
        functools.partial(_pool_kernel, tt=tt, pos0=pos0),
        grid=(b, t // tt),
        in_specs=[pl.BlockSpec((1, tt, D_MODEL), lambda i, j: (i, j, 0)),
                  pl.BlockSpec((1, POOL_CTX, D_MODEL), lambda i, j: (i, 0, 0)),
                  _const_spec((1, D_MODEL))],
        out_specs=[pl.BlockSpec((1, tt, D_MODEL), lambda i, j: (i, j, 0)),
                   pl.BlockSpec((1, POOL_CTX, D_MODEL), lambda i, j: (i, 0, 0))],
        out_shape=[jax.ShapeDtypeStruct((b, t, D_MODEL), BF16),
                   jax.ShapeDtypeStruct((b, POOL_CTX, D_MODEL), F32)],
        scratch_shapes=[pltpu.VMEM((POOL_CTX + 1 + tt, D_MODEL), F32)],
        compiler_params=_params("arbitrary", "arbitrary"),
        name="pool_front",
    )(x, prev, g)


def _dsa_in_kernel(xn_ref, wq_ref, wk_ref, wv_ref, wiq_ref, wik_ref, wiw_ref, qg_ref, kg_ref,
                   q_ref, k_ref, v_ref, iq_ref, ik_ref, iw_ref):
    x = xn_ref[...]
    q = _dot(x, wq_ref[...])
    for h in range(N_HEADS):
        cols = slice(h * HEAD_DIM, (h + 1) * HEAD_DIM)
        q_ref[:, cols] = _rms(q[:, cols], qg_ref[...]).astype(BF16)
    k = _dot(x, wk_ref[...])
    for h in range(N_KV_HEADS):
        cols = slice(h * HEAD_DIM, (h + 1) * HEAD_DIM)
        k_ref[:, cols] = _rms(k[:, cols], kg_ref[...])
    v_ref[...] = _dot(x, wv_ref[...])
    iq_ref[...] = _dot(x, wiq_ref[...]).astype(BF16)
    ik_ref[...] = _dot(x, wik_ref[...])
    iw_ref[...] = _dot(x, wiw_ref[...])


def _dsa_in(xn, w):
    n = xn.shape[0]
    tm = _row_tile(n, 512)
    kv = N_KV_HEADS * HEAD_DIM
    iqd = IDX_HEADS * IDX_DIM
    row = lambda wd: pl.BlockSpec((tm, wd), lambda i: (i, 0))
    return pl.pallas_call(
        _dsa_in_kernel,
        grid=(n // tm,),
        in_specs=[row(D_MODEL), _const_spec((D_MODEL, D_MODEL)), _const_spec((D_MODEL, kv)),
                  _const_spec((D_MODEL, kv)), _const_spec((D_MODEL, iqd)), _const_spec((D_MODEL, IDX_DIM)),
                  _const_spec((D_MODEL, IDX_HEADS)), _const_spec((1, HEAD_DIM)), _const_spec((1, HEAD_DIM))],
        out_specs=[row(D_MODEL), row(kv), row(kv), row(iqd), row(IDX_DIM), row(IDX_HEADS)],
        out_shape=[jax.ShapeDtypeStruct((n, D_MODEL), BF16), jax.ShapeDtypeStruct((n, kv), F32),
                   jax.ShapeDtypeStruct((n, kv), F32), jax.ShapeDtypeStruct((n, iqd), BF16),
                   jax.ShapeDtypeStruct((n, IDX_DIM), F32), jax.ShapeDtypeStruct((n, IDX_HEADS), F32)],
        compiler_params=_params("parallel"),
        name="dsa_in",
    )(xn, w["wq"], w["wk"], w["wv"], w["wiq"], w["wik"], w["wiw"], w["qg"], w["kg"])


def _dsa_attn_kernel(rb_ref, q_ref, iq_ref, iwt_ref, kb_ref, vt_ref, ik_ref, o_ref,
                     key_ref, msk_ref, lg_ref, bias_ref, j_ref, *, past, n_keys, topk, idx_bits):
    kt_sz = KEY_TILE
    qb = QUERY_BLOCK
    b = pl.program_id(0)
    i = pl.program_id(1)
    q0 = past + i * qb
    kend = jnp.minimum(n_keys, (((q0 + qb - 1) >> CHUNK_SHIFT) + 1) * CHUNK)
    nkt = (kend + kt_sz - 1) // kt_sz
    row_i = lax.broadcasted_iota(I32, (kt_sz, qb), 0)
    col_i = lax.broadcasted_iota(I32, (kt_sz, qb), 1)

    @pl.when((b == 0) & (i == 0))
    def _():
        for w in range(1, NEAR_SLOTS):
            rel = row_i - col_i + (w - (NEAR_SLOTS - 1)) * qb
            n = jnp.abs(rel)
            log_bucket = jnp.full_like(n, REL_BUCKETS // 4)
            for edge in REL_LOG_EDGES:
                log_bucket = log_bucket + (n >= edge).astype(I32)
            bucket = jnp.where(n < REL_BUCKETS // 4, n, log_bucket) + jnp.where(rel > 0, REL_BUCKETS // 2, 0)
            for h in range(N_HEADS):
                val = jnp.zeros((kt_sz, qb), F32)
                for bk in range(REL_BUCKETS):
                    val = jnp.where(bucket == bk, rb_ref[bk, h], val)
                bias_ref[h, w] = val
        for h in range(N_HEADS):
            bias_ref[h, 0] = jnp.full((kt_sz, qb), rb_ref[REL_FAR_BUCKET, h], F32)

    iw = iwt_ref[0]
    iq_all = iq_ref[0, 0].reshape(IDX_HEADS * qb, IDX_DIM)
    q_chunk = (q0 + col_i) >> CHUNK_SHIFT

    def tile_rows(jt):
        return pl.ds(pl.multiple_of(jt * kt_sz, kt_sz), kt_sz)

    def score_tile(jt, carry):
        ikt = ik_ref[0, tile_rows(jt), :]
        s = jnp.zeros((kt_sz, qb), F32)
        for hp in range(IDX_HEADS // 2):
            s2 = _dot_nt(ikt, iq_all[2 * hp * qb:(2 * hp + 2) * qb, :])
            for h in (2 * hp, 2 * hp + 1):
                s = s + iw[h:h + 1, :] * jnp.maximum(s2[:, (h - 2 * hp) * qb:(h - 2 * hp + 1) * qb], 0.0)
        s = s * (IDX_DIM ** -0.5)
        bits = pltpu.bitcast(s, I32)
        key = bits ^ ((bits >> 31) & 0x7FFFFFFF)
        kpos = jt * kt_sz + row_i
        adm = ((kpos >> CHUNK_SHIFT) <= q_chunk) & (kpos < n_keys)
        key_ref[tile_rows(jt), :] = jnp.where(adm, key, INT_MIN)
        return carry

    lax.fori_loop(0, nkt, score_tile, 0)

    def count(pred):
        def body(jt, acc):
            hit = jnp.where(pred(key_ref[tile_rows(jt), :], jt), 1.0, 0.0)
            return acc + _fold_rows(hit, jnp.add)

        acc = lax.fori_loop(0, nkt, body, jnp.zeros((8, qb), F32))
        return jnp.sum(acc, axis=0, keepdims=True)

    kf = float(topk)
    thr0 = jnp.where(count(lambda kk, jt: kk >= 0) >= kf, 0, INT_MIN).astype(I32)

    def bisect(p, thr):
        cand = thr | jnp.left_shift(jnp.int32(1), 30 - p)
        return jnp.where(count(lambda kk, jt: kk >= cand) >= kf, cand, thr)

    thr = lax.fori_loop(0, 31, bisect, thr0)
    n_gt = count(lambda kk, jt: kk > thr)
    n_ge = count(lambda kk, jt: kk >= thr)
    need = kf - n_gt
    has_thr = thr > INT_MIN
    j_ref[...] = jnp.where(has_thr, n_keys, -1).astype(I32)
    tied_cut = jnp.max(jnp.where(has_thr & (n_ge > kf), 1, 0)) > 0

    @pl.when(tied_cut)
    def _():
        lo = jnp.zeros((1, qb), I32)
        for bit in range(idx_bits - 1, -1, -1):
            cand = lo + (1 << bit)
            below = count(lambda kk, jt: (kk == thr) & ((jt * kt_sz + row_i) < cand))
            lo = jnp.where(below < need, cand, lo)
        j_ref[...] = jnp.where(has_thr, lo, -1)

    j_last = j_ref[...]

    def mask_tile(jt, carry):
        kk = key_ref[tile_rows(jt), :]
        sel = (kk > thr) | ((kk == thr) & ((jt * kt_sz + row_i) <= j_last))
        msk_ref[tile_rows(jt), :] = jnp.where(sel, 0.0, -jnp.inf)
        return carry

    lax.fori_loop(0, nkt, mask_tile, 0)

    gq = GROUP * qb
    slot0 = (past // qb + i) - (NEAR_SLOTS - 1)

    def bias_slot(jt):
        w = jt * (kt_sz // qb) - slot0
        return jnp.where(w >= 1, w, 0)

    for kvh in range(N_KV_HEADS):
        q_g = q_ref[0, 0, kvh * GROUP:(kvh + 1) * GROUP].reshape(gq, HEAD_DIM)

        def logits_tile(jt, m):
            kt = kb_ref[0, tile_rows(jt), kvh * HEAD_DIM:(kvh + 1) * HEAD_DIM]
            lg = _dot_nt(kt, q_g) * (HEAD_DIM ** -0.5)
            msk = msk_ref[tile_rows(jt), :]
            w = bias_slot(jt)
            parts = []
            for g in range(GROUP):
                lgh = lg[:, g * qb:(g + 1) * qb] + bias_ref[kvh * GROUP + g, w] + msk
                lg_ref[tile_rows(jt), g * qb:(g + 1) * qb] = lgh
                parts.append(_fold_rows(lgh, jnp.maximum))
            return jnp.maximum(m, jnp.concatenate(parts, axis=1))

        m8 = lax.fori_loop(0, nkt, logits_tile, jnp.full((8, gq), -jnp.inf, F32))
        m = jnp.max(m8, axis=0, keepdims=True)

        def pv_tile(jt, carry):
            acc, l8 = carry
            p = jnp.exp(lg_ref[tile_rows(jt), :] - m)
            l8 = l8 + _fold_rows(p, jnp.add)
            return acc + _dot(vt_ref[0, kvh, jt], p.astype(BF16)), l8

        acc, l8 = lax.fori_loop(0, nkt, pv_tile, (jnp.zeros((HEAD_DIM, gq), F32), jnp.zeros((8, gq), F32)))
        o_t = acc / jnp.sum(l8, axis=0, keepdims=True)
        for g in range(GROUP):
            h = kvh * GROUP + g
            o_ref[0, :, h * HEAD_DIM:(h + 1) * HEAD_DIM] = o_t[:, g * qb:(g + 1) * qb].T.astype(BF16)


def _dsa_attn(rel_bias, q, iq, iwt, kb, vt, ikb, *, past, n_keys):
    b, nblk = q.shape[:2]
    lp = kb.shape[1]
    qb = QUERY_BLOCK
    assert lp % KEY_TILE == 0 and past % KEY_TILE == 0 and KEY_TILE == (NEAR_SLOTS - 2) * qb
    topk = min(TOPK_MAX, n_keys // 4)
    kv = N_KV_HEADS * HEAD_DIM
    return pl.pallas_call(
        functools.partial(_dsa_attn_kernel, past=past, n_keys=n_keys, topk=topk,
                          idx_bits=max(1, (lp - 1).bit_length())),
        grid=(b, nblk),
        in_specs=[pl.BlockSpec(memory_space=pltpu.SMEM),
                  pl.BlockSpec((1, 1, N_HEADS, qb, HEAD_DIM), lambda i, j: (i, j, 0, 0, 0)),
                  pl.BlockSpec((1, 1, IDX_HEADS, qb, IDX_DIM), lambda i, j: (i, j, 0, 0, 0)),
                  pl.BlockSpec((1, IDX_HEADS, qb), lambda i, j: (i, 0, j)),
                  pl.BlockSpec((1, lp, kv), lambda i, j: (i, 0, 0)),
                  pl.BlockSpec((1, N_KV_HEADS, lp // KEY_TILE, HEAD_DIM, KEY_TILE), lambda i, j: (i, 0, 0, 0, 0)),
                  pl.BlockSpec((1, lp, IDX_DIM), lambda i, j: (i, 0, 0))],
        out_specs=pl.BlockSpec((1, qb, D_MODEL), lambda i, j: (i, j, 0)),
        out_shape=jax.ShapeDtypeStruct((b, nblk * qb, D_MODEL), BF16),
        scratch_shapes=[pltpu.VMEM((lp, qb), I32), pltpu.VMEM((lp, qb), F32), pltpu.VMEM((lp, GROUP * qb), F32),
                        pltpu.VMEM((N_HEADS, NEAR_SLOTS, KEY_TILE, qb), F32), pltpu.VMEM((1, qb), I32)],
        compiler_params=_params("arbitrary", "arbitrary"),
        name="dsa_attn",
    )(rel_bias, q, iq, iwt, kb, vt, ikb)


def _dsa_mixer(xn, k_past, v_past, ik_past, w, b, t):
    past = k_past.shape[1]
    kv = N_KV_HEADS * HEAD_DIM
    q, k, v, iq, ik, iw = _dsa_in(xn, w)
    n_keys = past + t
    qb = QUERY_BLOCK
    tq = -(-t // qb) * qb
    lp = -(-n_keys // KEY_TILE) * KEY_TILE
    pad_q = lambda z: jnp.pad(z.reshape(b, t, z.shape[-1]), ((0, 0), (0, tq - t), (0, 0)))
    by_head = lambda z, nh: pad_q(z).reshape(b, tq // qb, qb, nh, z.shape[-1] // nh).transpose(0, 1, 3, 2, 4)
    keys = lambda old, new: jnp.pad(
        jnp.concatenate([old.reshape(b, past, new.shape[-1]).astype(BF16),
                         new.reshape(b, t, new.shape[-1]).astype(BF16)], axis=1),
        ((0, 0), (0, lp - n_keys), (0, 0)))
    kb = keys(k_past, k)
    vt = keys(v_past, v).reshape(b, lp // KEY_TILE, KEY_TILE, N_KV_HEADS, HEAD_DIM).transpose(0, 3, 1, 4, 2)
    ikb = keys(ik_past, ik)
    o = _dsa_attn(w["rel_bias"], by_head(q, N_HEADS), by_head(iq, IDX_HEADS), pad_q(iw).transpose(0, 2, 1),
                  kb, vt, ikb, past=past, n_keys=n_keys)
    o = o[:, :t].reshape(b * t, D_MODEL)
    return (o, k.reshape(b, t, N_KV_HEADS, HEAD_DIM), v.reshape(b, t, N_KV_HEADS, HEAD_DIM),
            ik.reshape(b, t, IDX_DIM))


def _gla_in_kernel(xn_ref, wq_ref, wk_ref, wv_ref, wr_ref, wa_ref, wa2_ref, ba_ref,
                   q_ref, k_ref, v_ref, r_ref, g_ref):
    x = xn_ref[...]
    q_ref[...] = _dot(x, wq_ref[...]) * (GLA_DK ** -0.5)
    k_ref[...] = _dot(x, wk_ref[...])
    v_ref[...] = _dot(x, wv_ref[...]).astype(BF16)
    r_ref[...] = _dot(x, wr_ref[...])
    z = _dot(_dot(x, wa_ref[...]).astype(BF16), wa2_ref[...]) + ba_ref[...]
    g_ref[...] = (jnp.minimum(z, 0.0) - jnp.log1p(jnp.exp(-jnp.abs(z)))) * (1.0 / GLA_GATE_NORM)


def _gla_in(xn, w):
    n = xn.shape[0]
    tm = _row_tile(n, 512)
    qk = GLA_HEADS * GLA_DK
    row = lambda wd: pl.BlockSpec((tm, wd), lambda i: (i, 0))
    return pl.pallas_call(
        _gla_in_kernel,
        grid=(n // tm,),
        in_specs=[row(D_MODEL), _const_spec((D_MODEL, qk)), _const_spec((D_MODEL, qk)),
                  _const_spec((D_MODEL, D_MODEL)), _const_spec((D_MODEL, D_MODEL)),
                  _const_spec((D_MODEL, GLA_RANK)), _const_spec((GLA_RANK, qk)), _const_spec((1, qk))],
        out_specs=[row(qk), row(qk), row(D_MODEL), row(D_MODEL), row(qk)],
        out_shape=[jax.ShapeDtypeStruct((n, qk), F32), jax.ShapeDtypeStruct((n, qk), F32),
                   jax.ShapeDtypeStruct((n, D_MODEL), BF16), jax.ShapeDtypeStruct((n, D_MODEL), F32),
                   jax.ShapeDtypeStruct((n, qk), F32)],
        compiler_params=_params("parallel"),
        name="gla_in",
    )(xn, w["wq"], w["wk"], w["wv"], w["wr"], w["wa"], w["wa2"], w["ba"])


def _gla_core_kernel(q_ref, k_ref, g_ref, v_ref, r_ref, s0_ref, gain_ref, a_ref, sout_ref, st_ref, *, tt):
    sb = GLA_SUB
    t = pl.program_id(1)

    @pl.when(t == 0)
    def _():
        for h in range(GLA_HEADS):
            st_ref[h] = s0_ref[0, h].T

    tri = lax.broadcasted_iota(I32, (sb, sb), 0) >= lax.broadcasted_iota(I32, (sb, sb), 1)
    tri_f = tri.astype(F32)
    gain = gain_ref[...]

    def sub_block(si, carry):
        rows = pl.ds(pl.multiple_of(si * sb, sb), sb)
        for h in range(GLA_HEADS):
            dk = slice(h * GLA_DK, (h + 1) * GLA_DK)
            dv = slice(h * GLA_DV, (h + 1) * GLA_DV)
            q = q_ref[0, rows, dk]
            k = k_ref[0, rows, dk]
            v = v_ref[0, rows, dv]
            cum = jnp.dot(tri_f, g_ref[0, rows, dk], precision=HIGHEST, preferred_element_type=F32)
            decay = jnp.exp(jnp.minimum(cum[:, None, :] - cum[None, :, :], 0.0))
            sc = jnp.sum(decay * q[:, None, :] * k[None, :, :], axis=-1)
            sc = jnp.where(tri, sc, 0.0)
            st = st_ref[h]
            o = _dot(sc.astype(BF16), v) + _dot_nt((q * jnp.exp(cum)).astype(BF16), st.astype(BF16))
            last = cum[sb - 1:sb, :]
            k_dec = (k * jnp.exp(last - cum)).astype(BF16)
            st_ref[h] = st * jnp.exp(last) + _dot_tn(v, k_dec)
            r = r_ref[0, rows, dv]
            a_ref[0, rows, dv] = (_rms(o, gain) * (r * jax.nn.sigmoid(r))).astype(BF16)
        return carry

    lax.fori_loop(0, tt // sb, sub_block, 0)

    @pl.when(t == pl.num_programs(1) - 1)
    def _():
        for h in range(GLA_HEADS):
            sout_ref[0, h] = st_ref[h].T


def _gla_core(q, k, g, v, r, s0, gain):
    b, t, _ = q.shape
    tt = _row_tile(t, 256)
    assert tt % GLA_SUB == 0
    qk = GLA_HEADS * GLA_DK
    tok = lambda wd: pl.BlockSpec((1, tt, wd), lambda i, j: (i, j, 0))
    st_spec = pl.BlockSpec((1, GLA_HEADS, GLA_DK, GLA_DV), lambda i, j: (i, 0, 0, 0))
    return pl.pallas_call(
        functools.partial(_gla_core_kernel, tt=tt),
        grid=(b, t // tt),
        in_specs=[tok(qk), tok(qk), tok(qk), tok(D_MODEL), tok(D_MODEL), st_spec, _const_spec((1, GLA_DV))],
        out_specs=[tok(D_MODEL), st_spec],
        out_shape=[jax.ShapeDtypeStruct((b, t, D_MODEL), BF16),
                   jax.ShapeDtypeStruct((b, GLA_HEADS, GLA_DK, GLA_DV), F32)],
        scratch_shapes=[pltpu.VMEM((GLA_HEADS, GLA_DV, GLA_DK), F32)],
        compiler_params=_params("arbitrary", "arbitrary"),
        name="gla_core",
    )(q, k, g, v, r, s0, gain)


def _conv_kernel(xn_ref, wb_ref, wc_ref, wh_ref, prev_ref, cw_ref, cb_ref, a_ref, new_ref, ue_ref, *, tt):
    t = pl.program_id(1)
    pad = 8
    lo = pad - (CONV_W - 1)

    @pl.when(t == 0)
    def _():
        ue_ref[lo:pad, :] = prev_ref[0]

    x = xn_ref[0]
    u = _dot(x, wc_ref[...]) * _dot(x, wh_ref[...])
    ue_ref[pad:pad + tt, :] = u
    conv = cb_ref[...] + cw_ref[CONV_W - 1:CONV_W, :] * u
    for j in range(CONV_W - 1):
        conv = conv + cw_ref[j:j + 1, :] * ue_ref[lo + j:lo + j + tt, :]
    a_ref[0] = (_dot(x, wb_ref[...]) * conv).astype(BF16)
    tail = ue_ref[lo + tt:pad + tt, :]
    ue_ref[lo:pad, :] = tail

    @pl.when(t == pl.num_programs(1) - 1)
    def _():
        new_ref[0] = tail


def _conv_front(xn, prev, w):
    b, t, _ = xn.shape
    tt = _row_tile(t, 512)
    tok = pl.BlockSpec((1, tt, D_MODEL), lambda i, j: (i, j, 0))
    st = pl.BlockSpec((1, CONV_W - 1, D_MODEL), lambda i, j: (i, 0, 0))
    sq = _const_spec((D_MODEL, D_MODEL))
    return pl.pallas_call(
        functools.partial(_conv_kernel, tt=tt),
        grid=(b, t // tt),
        in_specs=[tok, sq, sq, sq, st, _const_spec((CONV_W, D_MODEL)), _const_spec((1, D_MODEL))],
        out_specs=[tok, st],
        out_shape=[jax.ShapeDtypeStruct((b, t, D_MODEL), BF16),
                   jax.ShapeDtypeStruct((b, CONV_W - 1, D_MODEL), F32)],
        scratch_shapes=[pltpu.VMEM((8 + tt, D_MODEL), F32)],
        compiler_params=_params("arbitrary", "arbitrary"),
        name="conv_front",
    )(xn, w["wb"], w["wc"], w["wh"], prev, w["cw"], w["cb"])


def _trunk(x, p, pool_prev, k_past, v_past, ik_past, gla_prev, conv_prev, w):
    b, t, _ = x.shape
    n = b * t
    past = k_past.shape[1]
    flat = lambda z: z.reshape(n, z.shape[-1])
    post = w["post"]

    a, pool_new = _pool_front(x, pool_prev, w["norm_mix"][0], past)
    h, xn = _post_block(flat(x), flat(a), flat(p[0]), post[0], w["norm_mix"][1])

    a, k_new, v_new, ik_new = _dsa_mixer(xn, k_past, v_past, ik_past, w["dsa"], b, t)
    h, xn = _post_block(h, a, flat(p[1]), post[1], w["norm_mix"][2])

    q, k, v, r, g = _gla_in(xn, w["gla"])
    seq = lambda z: z.reshape(b, t, z.shape[-1])
    a, gla_new = _gla_core(seq(q), seq(k), seq(g), seq(v), seq(r), gla_prev, w["gla"]["gain"])
    h, xn = _post_block(h, flat(a), flat(p[2]), post[2], w["norm_mix"][3])

    a, conv_new = _conv_front(seq(xn), conv_prev, w["conv"])
    h, _ = _post_block(h, flat(a), flat(p[3]), post[3], None)
    return h.reshape(b, t, D_MODEL), pool_new, k_new, v_new, ik_new, gla_new, conv_new


def _prepare_weights(norm_mix, norm_mlp, norm_ple, w_mlp1, w_mlp2, w_ple_proj, w_ple_gate, w_pool, b_pool,
                     pool_scale, w_dsa_in, w_dsa_out, q_norm, k_norm, rel_bias, w_gla_in, w_gla_a2, b_gla_a,
                     gla_norm, w_gla_out, w_conv_in, conv_w, conv_b, w_conv_out):
    bf = lambda z: z.astype(BF16)
    vec = lambda z: z.reshape(1, -1).astype(F32)
    depth = norm_mix.shape[0]
    w_pool_bd = jax.scipy.linalg.block_diag(*[w_pool[g] for g in range(w_pool.shape[0])])
    wouts = [w_pool_bd, w_dsa_out, w_gla_out, w_conv_out]
    zeros = jnp.zeros((1, D_MODEL), F32)
    ones = jnp.ones((1, D_MODEL), F32)
    bouts = [vec(b_pool), zeros, zeros, zeros]
    souts = [vec(pool_scale), ones, ones, ones]
    post = [dict(wout=bf(wouts[i]), bout=bouts[i], sout=souts[i], gmlp=vec(norm_mlp[i]), w1=bf(w_mlp1[i]),
                 w2=bf(w_mlp2[i]), gple=vec(norm_ple[i]), wg=bf(w_ple_gate[i]), wp=bf(w_ple_proj[i]))
            for i in range(depth)]

    def split(wm, sizes):
        out, o = [], 0
        for s in sizes:
            out.append(bf(wm[:, o:o + s]))
            o += s
        return out

    kv = N_KV_HEADS * HEAD_DIM
    wq, wk, wv, wiq, wik, wiw = split(w_dsa_in, (D_MODEL, kv, kv, IDX_HEADS * IDX_DIM, IDX_DIM, IDX_HEADS))
    dsa = dict(wq=wq, wk=wk, wv=wv, wiq=wiq, wik=wik, wiw=wiw, qg=vec(q_norm), kg=vec(k_norm),
               rel_bias=rel_bias.astype(F32))
    qk = GLA_HEADS * GLA_DK
    gq, gk, gv, gr, ga = split(w_gla_in, (qk, qk, D_MODEL, D_MODEL, GLA_RANK))
    gla = dict(wq=gq, wk=gk, wv=gv, wr=gr, wa=ga, wa2=bf(w_gla_a2), ba=vec(b_gla_a), gain=vec(gla_norm))
    cb_, cc_, ch_ = split(w_conv_in, (D_MODEL, D_MODEL, D_MODEL))
    conv = dict(wb=cb_, wc=cc_, wh=ch_, cw=conv_w.astype(F32), cb=vec(conv_b))
    return dict(post=post, norm_mix=[vec(norm_mix[i]) for i in range(depth)], dsa=dsa, gla=gla, conv=conv)


def kernel(x_prompt, x_sample, p_prompt, p_sample, state_pool, cache_k, cache_v, cache_idx_k, state_gla, state_conv, norm_mix, norm_mlp, norm_ple, w_mlp1, w_mlp2, w_ple_proj, w_ple_gate, w_pool, b_pool, pool_scale, w_dsa_in, w_dsa_out, q_norm, k_norm, rel_bias, w_gla_in, w_gla_a2, b_gla_a, gla_norm, w_gla_out, w_conv_in, conv_w, conv_b, w_conv_out):
    w = _prepare_weights(norm_mix, norm_mlp, norm_ple, w_mlp1, w_mlp2, w_ple_proj, w_ple_gate, w_pool, b_pool,
                         pool_scale, w_dsa_in, w_dsa_out, q_norm, k_norm, rel_bias, w_gla_in, w_gla_a2, b_gla_a,
                         gla_norm, w_gla_out, w_conv_in, conv_w, conv_b, w_conv_out)
    bp = x_prompt.shape[0]
    dt = x_prompt.dtype
    y_p, pool_p, k_p, v_p, ik_p, gla_p, conv_p = _trunk(
        x_prompt, p_prompt,
        jnp.zeros((bp, POOL_CTX, D_MODEL), dt),
        jnp.zeros((bp, 0, N_KV_HEADS, HEAD_DIM), dt),
        jnp.zeros((bp, 0, N_KV_HEADS, HEAD_DIM), dt),
        jnp.zeros((bp, 0, IDX_DIM), dt),
        jnp.zeros((bp, GLA_HEADS, GLA_DK, GLA_DV), dt),
        jnp.zeros((bp, CONV_W - 1, D_MODEL), dt),
        w)
    y_s, pool_s, k_s, v_s, ik_s, gla_s, conv_s = _trunk(
        x_sample, p_sample, state_pool, cache_k, cache_v, cache_idx_k, state_gla, state_conv, w)
    return (y_p, y_s, pool_p, pool_s, k_p, v_p, ik_p, k_s, v_s, ik_s, gla_p, gla_s, conv_p, conv_s)
```

```python
import functools

import jax
import jax.numpy as jnp
from jax import lax
from jax.experimental import pallas as pl
from jax.experimental.pallas import tpu as pltpu

F32 = jnp.float32
BF16 = jnp.bfloat16
I32 = jnp.int32

D_MODEL = 1024
D_FF = 4 * D_MODEL
PLE_DIM = 256
EPS = 1e-6
CHUNK = 64
CHUNK_SHIFT = 6
POOL_WINDOWS = (2, 4, 8, 16)
POOL_GC = D_MODEL // len(POOL_WINDOWS)
POOL_CTX = max(POOL_WINDOWS) - 1
N_HEADS = 8
N_KV_HEADS = 2
HEAD_DIM = D_MODEL // N_HEADS
GROUP = N_HEADS // N_KV_HEADS
IDX_HEADS = 8
IDX_DIM = 64
TOPK_MAX = 256
REL_BUCKETS = 32
REL_LOG_EDGES = (12, 16, 23, 32, 46, 64, 91)
REL_FAR_BUCKET = REL_BUCKETS // 2 - 1
GLA_HEADS = 4
GLA_DK = D_MODEL // 2 // GLA_HEADS
GLA_DV = D_MODEL // GLA_HEADS
GLA_RANK = 16
GLA_GATE_NORM = 16.0
GLA_SUB = 16
CONV_W = 3

V7X_VMEM_LIMIT_BYTES = 56 * 1024 * 1024
LANES = 128
INT_MIN = -(2 ** 31)
QUERY_BLOCK = 128
KEY_TILE = 256
NEAR_SLOTS = KEY_TILE // QUERY_BLOCK + 2

HIGHEST = lax.Precision.HIGHEST
LOG2_E = 1.4426950408889634


def _params(*sem):
    return pltpu.CompilerParams(dimension_semantics=sem, vmem_limit_bytes=V7X_VMEM_LIMIT_BYTES)


def _const_spec(shape):
    nd = len(shape)
    return pl.BlockSpec(shape, lambda *_: (0,) * nd, pipeline_mode=pl.Buffered(1))


def _rms(x, g):
    return x * lax.rsqrt(jnp.mean(x * x, axis=-1, keepdims=True) + EPS) * g


def _dot(a, b):
    return jnp.dot(a, b, preferred_element_type=F32)


def _dot_nt(a, b):
    return lax.dot_general(a, b, (((1,), (1,)), ((), ())), preferred_element_type=F32)


def _dot_tn(a, b):
    return lax.dot_general(a, b, (((0,), (0,)), ((), ())), preferred_element_type=F32)


def _fold_rows(x, op):
    n = x.shape[0] // 8
    assert n & (n - 1) == 0
    parts = x.reshape(n, 8, x.shape[1])
    while n > 1:
        n //= 2
        parts = op(parts[:n], parts[n:])
    return parts[0]


def _row_tile(n, want):
    t = min(n, want)
    assert n % t == 0
    return t


def _post_kernel(h_ref, a_ref, p_ref, wout_ref, bout_ref, sout_ref, gmlp_ref, w1_ref, w2_ref, gple_ref, wg_ref,
                 wp_ref, *rest, ff_chunk, with_next):
    if with_next:
        gnext_ref, h_out_ref, xn_out_ref = rest
    else:
        (h_out_ref,) = rest
    y = (_dot(a_ref[...], wout_ref[...]) + bout_ref[...]) * sout_ref[...]
    h1 = h_ref[...] + y
    hn = _rms(h1, gmlp_ref[...]).astype(BF16)
    acc = h1
    for c in range(D_FF // ff_chunk):
        cols = slice(c * ff_chunk, (c + 1) * ff_chunk)
        hid = jnp.square(jnp.maximum(_dot(hn, w1_ref[:, cols]), 0.0)).astype(BF16)
        acc = acc + _dot(hid, w2_ref[cols, :])
    gate = jax.nn.sigmoid(_dot(_rms(acc, gple_ref[...]).astype(BF16), wg_ref[...]))
    h3 = acc + _dot(p_ref[...].astype(BF16), wp_ref[...]) * gate
    h_out_ref[...] = h3
    if with_next:
        xn_out_ref[...] = _rms(h3, gnext_ref[...]).astype(BF16)


def _post_block(h, a, p, lw, gnext):
    n = h.shape[0]
    tm = _row_tile(n, 512)
    with_next = gnext is not None
    row = lambda w: pl.BlockSpec((tm, w), lambda i: (i, 0))
    vec = _const_spec((1, D_MODEL))
    out_shape = [jax.ShapeDtypeStruct((n, D_MODEL), F32)]
    out_specs = [row(D_MODEL)]
    in_specs = [row(D_MODEL), row(D_MODEL), row(PLE_DIM),
                _const_spec((D_MODEL, D_MODEL)), vec, vec, vec,
                _const_spec((D_MODEL, D_FF)), _const_spec((D_FF, D_MODEL)), vec,
                _const_spec((D_MODEL, D_MODEL)), _const_spec((PLE_DIM, D_MODEL))]
    args = [h, a, p, lw["wout"], lw["bout"], lw["sout"], lw["gmlp"], lw["w1"], lw["w2"], lw["gple"], lw["wg"],
            lw["wp"]]
    if with_next:
        out_shape.append(jax.ShapeDtypeStruct((n, D_MODEL), BF16))
        out_specs.append(row(D_MODEL))
        in_specs.append(vec)
        args.append(gnext)
    outs = pl.pallas_call(
        functools.partial(_post_kernel, ff_chunk=1024, with_next=with_next),
        grid=(n // tm,),
        in_specs=in_specs,
        out_specs=out_specs,
        out_shape=out_shape,
        compiler_params=_params("parallel"),
        name="post_block",
    )(*args)
    return (outs[0], outs[1]) if with_next else (outs[0], None)


def _pool_kernel(x_ref, prev_ref, g_ref, a_ref, pool_ref, xe_ref, *, tt, pos0):
    t = pl.program_id(1)
    ctx = POOL_CTX + 1

    @pl.when(t == 0)
    def _():
        xe_ref[0:1, :] = jnp.zeros((1, D_MODEL), F32)
        xe_ref[1:ctx, :] = prev_ref[0]

    xn = _rms(x_ref[0], g_ref[...])
    xe_ref[ctx:ctx + tt, :] = xn
    pos1 = (pos0 + 1 + t * tt + lax.broadcasted_iota(I32, (tt, 1), 0)).astype(F32)
    for g, w in enumerate(POOL_WINDOWS):
        cols = slice(g * POOL_GC, (g + 1) * POOL_GC)
        win = xn[:, cols]
        for j in range(1, w):
            win = win + xe_ref[ctx - j:ctx - j + tt, cols]
        mean = win / jnp.minimum(float(w), pos1)
        a_ref[0, :, cols] = (mean - xn[:, cols]).astype(BF16)
    tail = xe_ref[tt:tt + ctx, :]
    xe_ref[0:ctx, :] = tail

    @pl.when(t == pl.num_programs(1) - 1)
    def _():
        pool_ref[0] = xe_ref[tt + 1:tt + ctx, :]


def _pool_front(x, prev, g, pos0):
    b, t, _ = x.shape
    tt = _row_tile(t, 512)
    return pl.pallas_call(
        functools.partial(_pool_kernel, tt=tt, pos0=pos0),
        grid=(b, t // tt),
        in_specs=[pl.BlockSpec((1, tt, D_MODEL), lambda i, j: (i, j, 0)),
                  pl.BlockSpec((1, POOL_CTX, D_MODEL), lambda i, j: (i, 0, 0)),
                  _const_spec((1, D_MODEL))],
        out_specs=[pl.BlockSpec((1, tt, D_MODEL), lambda i, j: (i, j, 0)),
                   pl.BlockSpec((1, POOL_CTX, D_MODEL), lambda i, j: (i, 0, 0))],
        out_shape=[jax.ShapeDtypeStruct((b, t, D_MODEL), BF16),
                   jax.ShapeDtypeStruct((b, POOL_CTX, D_MODEL), F32)],
        scratch_shapes=[pltpu.VMEM((POOL_CTX + 1 + tt, D_MODEL), F32)],
        compiler_params=_params("arbitrary", "arbitrary"),
        name="pool_front",
    )(x, prev, g)


def _dsa_in_kernel(xn_ref, wq_ref, wk_ref, wv_ref, wiq_ref, wik_ref, wiw_ref, qg_ref, kg_ref,
                   q_ref, k_ref, v_ref, iq_ref, ik_ref, iw_ref):
    x = xn_ref[...]
    q = _dot(x, wq_ref[...])
    for h in range(N_HEADS):
        cols = slice(h * HEAD_DIM, (h + 1) * HEAD_DIM)
        q_ref[:, cols] = _rms(q[:, cols], qg_ref[...]).astype(BF16)
    k = _dot(x, wk_ref[...])
    for h in range(N_KV_HEADS):
        cols = slice(h * HEAD_DIM, (h + 1) * HEAD_DIM)
        k_ref[:, cols] = _rms(k[:, cols], kg_ref[...])
    v_ref[...] = _dot(x, wv_ref[...])
    iq_ref[...] = _dot(x, wiq_ref[...]).astype(BF16)
    ik_ref[...] = _dot(x, wik_ref[...])
    iw_ref[...] = _dot(x, wiw_ref[...])


def _dsa_in(xn, w):
    n = xn.shape[0]
    tm = _row_tile(n, 512)
    kv = N_KV_HEADS * HEAD_DIM
    iqd = IDX_HEADS * IDX_DIM
    row = lambda wd: pl.BlockSpec((tm, wd), lambda i: (i, 0))
    return pl.pallas_call(
        _dsa_in_kernel,
        grid=(n // tm,),
        in_specs=[row(D_MODEL), _const_spec((D_MODEL, D_MODEL)), _const_spec((D_MODEL, kv)),
                  _const_spec((D_MODEL, kv)), _const_spec((D_MODEL, iqd)), _const_spec((D_MODEL, IDX_DIM)),
                  _const_spec((D_MODEL, IDX_HEADS)), _const_spec((1, HEAD_DIM)), _const_spec((1, HEAD_DIM))],
        out_specs=[row(D_MODEL), row(kv), row(kv), row(iqd), row(IDX_DIM), row(IDX_HEADS)],
        out_shape=[jax.ShapeDtypeStruct((n, D_MODEL), BF16), jax.ShapeDtypeStruct((n, kv), F32),
                   jax.ShapeDtypeStruct((n, kv), F32), jax.ShapeDtypeStruct((n, iqd), BF16),
                   jax.ShapeDtypeStruct((n, IDX_DIM), F32), jax.ShapeDtypeStruct((n, IDX_HEADS), F32)],
        compiler_params=_params("parallel"),
        name="dsa_in",
    )(xn, w["wq"], w["wk"], w["wv"], w["wiq"], w["wik"], w["wiw"], w["qg"], w["kg"])


def _dsa_attn_kernel(rb_ref, q_ref, iq_ref, iwt_ref, kb_ref, vt_ref, ik_ref, o_ref,
                     key_ref, msk_ref, lg_ref, bias_ref, j_ref, *, past, n_keys, topk, idx_bits):
    kt_sz = KEY_TILE
    qb = QUERY_BLOCK
    b = pl.program_id(0)
    i = pl.program_id(1)
    q0 = past + i * qb
    kend = jnp.minimum(n_keys, (((q0 + qb - 1) >> CHUNK_SHIFT) + 1) * CHUNK)
    nkt = (kend + kt_sz - 1) // kt_sz
    row_i = lax.broadcasted_iota(I32, (kt_sz, qb), 0)
    col_i = lax.broadcasted_iota(I32, (kt_sz, qb), 1)

    @pl.when((b == 0) & (i == 0))
    def _():
        for w in range(1, NEAR_SLOTS):
            rel = row_i - col_i + (w - (NEAR_SLOTS - 1)) * qb
            n = jnp.abs(rel)
            log_bucket = jnp.full_like(n, REL_BUCKETS // 4)
            for edge in REL_LOG_EDGES:
                log_bucket = log_bucket + (n >= edge).astype(I32)
            bucket = jnp.where(n < REL_BUCKETS // 4, n, log_bucket) + jnp.where(rel > 0, REL_BUCKETS // 2, 0)
            for h in range(N_HEADS):
                val = jnp.zeros((kt_sz, qb), F32)
                for bk in range(REL_BUCKETS):
                    val = jnp.where(bucket == bk, rb_ref[bk, h], val)
                bias_ref[h, w] = val
        for h in range(N_HEADS):
            bias_ref[h, 0] = jnp.full((kt_sz, qb), rb_ref[REL_FAR_BUCKET, h], F32)

    iw = iwt_ref[0]
    iq_all = iq_ref[0, 0].reshape(IDX_HEADS * qb, IDX_DIM)
    q_chunk = (q0 + col_i) >> CHUNK_SHIFT

    def tile_rows(jt):
        return pl.ds(pl.multiple_of(jt * kt_sz, kt_sz), kt_sz)

    def score_tile(jt, carry):
        ikt = ik_ref[0, tile_rows(jt), :]
        s = jnp.zeros((kt_sz, qb), F32)
        for hp in range(IDX_HEADS // 2):
            s2 = _dot_nt(ikt, iq_all[2 * hp * qb:(2 * hp + 2) * qb, :])
            for h in (2 * hp, 2 * hp + 1):
                s = s + iw[h:h + 1, :] * jnp.maximum(s2[:, (h - 2 * hp) * qb:(h - 2 * hp + 1) * qb], 0.0)
        s = s * (IDX_DIM ** -0.5)
        bits = pltpu.bitcast(s, I32)
        key = bits ^ ((bits >> 31) & 0x7FFFFFFF)
        kpos = jt * kt_sz + row_i
        adm = ((kpos >> CHUNK_SHIFT) <= q_chunk) & (kpos < n_keys)
        key_ref[tile_rows(jt), :] = jnp.where(adm, key, INT_MIN)
        return carry

    lax.fori_loop(0, nkt, score_tile, 0)

    def count(pred):
        def body(jt, acc):
            hit = jnp.where(pred(key_ref[tile_rows(jt), :], jt), 1.0, 0.0)
            return acc + _fold_rows(hit, jnp.add)

        acc = lax.fori_loop(0, nkt, body, jnp.zeros((8, qb), F32))
        return jnp.sum(acc, axis=0, keepdims=True)

    kf = float(topk)
    thr0 = jnp.where(count(lambda kk, jt: kk >= 0) >= kf, 0, INT_MIN).astype(I32)

    def bisect(p, thr):
        cand = thr | jnp.left_shift(jnp.int32(1), 30 - p)
        return jnp.where(count(lambda kk, jt: kk >= cand) >= kf, cand, thr)

    thr = lax.fori_loop(0, 31, bisect, thr0)
    n_gt = count(lambda kk, jt: kk > thr)
    n_ge = count(lambda kk, jt: kk >= thr)
    need = kf - n_gt
    has_thr = thr > INT_MIN
    j_ref[...] = jnp.where(has_thr, n_keys, -1).astype(I32)
    tied_cut = jnp.max(jnp.where(has_thr & (n_ge > kf), 1, 0)) > 0

    @pl.when(tied_cut)
    def _():
        lo = jnp.zeros((1, qb), I32)
        for bit in range(idx_bits - 1, -1, -1):
            cand = lo + (1 << bit)
            below = count(lambda kk, jt: (kk == thr) & ((jt * kt_sz + row_i) < cand))
            lo = jnp.where(below < need, cand, lo)
        j_ref[...] = jnp.where(has_thr, lo, -1)

    j_last = j_ref[...]

    def mask_tile(jt, carry):
        kk = key_ref[tile_rows(jt), :]
        sel = (kk > thr) | ((kk == thr) & ((jt * kt_sz + row_i) <= j_last))
        msk_ref[tile_rows(jt), :] = jnp.where(sel, 0.0, -jnp.inf)
        return carry

    lax.fori_loop(0, nkt, mask_tile, 0)

    gq = GROUP * qb
    slot0 = (past // qb + i) - (NEAR_SLOTS - 1)

    def bias_slot(jt):
        w = jt * (kt_sz // qb) - slot0
        return jnp.where(w >= 1, w, 0)

    for kvh in range(N_KV_HEADS):
        q_g = q_ref[0, 0, kvh * GROUP:(kvh + 1) * GROUP].reshape(gq, HEAD_DIM)

        def logits_tile(jt, m):
            kt = kb_ref[0, tile_rows(jt), kvh * HEAD_DIM:(kvh + 1) * HEAD_DIM]
            lg = _dot_nt(kt, q_g) * (HEAD_DIM ** -0.5)
            msk = msk_ref[tile_rows(jt), :]
            w = bias_slot(jt)
            parts = []
            for g in range(GROUP):
                lgh = lg[:, g * qb:(g + 1) * qb] + bias_ref[kvh * GROUP + g, w] + msk
                lg_ref[tile_rows(jt), g * qb:(g + 1) * qb] = lgh
                parts.append(_fold_rows(lgh, jnp.maximum))
            return jnp.maximum(m, jnp.concatenate(parts, axis=1))

        m8 = lax.fori_loop(0, nkt, logits_tile, jnp.full((8, gq), -jnp.inf, F32))
        m = jnp.max(m8, axis=0, keepdims=True)

        def pv_tile(jt, carry):
            acc, l8 = carry
            p = jnp.exp(lg_ref[tile_rows(jt), :] - m)
            l8 = l8 + _fold_rows(p, jnp.add)
            return acc + _dot(vt_ref[0, kvh, jt], p.astype(BF16)), l8

        acc, l8 = lax.fori_loop(0, nkt, pv_tile, (jnp.zeros((HEAD_DIM, gq), F32), jnp.zeros((8, gq), F32)))
        o_t = acc / jnp.sum(l8, axis=0, keepdims=True)
        for g in range(GROUP):
            h = kvh * GROUP + g
            o_ref[0, :, h * HEAD_DIM:(h + 1) * HEAD_DIM] = o_t[:, g * qb:(g + 1) * qb].T.astype(BF16)


def _dsa_attn(rel_bias, q, iq, iwt, kb, vt, ikb, *, past, n_keys):
    b, nblk = q.shape[:2]
    lp = kb.shape[1]
    qb = QUERY_BLOCK
    assert lp % KEY_TILE == 0 and past % KEY_TILE == 0 and KEY_TILE == (NEAR_SLOTS - 2) * qb
    topk = min(TOPK_MAX, n_keys // 4)
    kv = N_KV_HEADS * HEAD_DIM
    return pl.pallas_call(
        functools.partial(_dsa_attn_kernel, past=past, n_keys=n_keys, topk=topk,
                          idx_bits=max(1, (lp - 1).bit_length())),
        grid=(b, nblk),
        in_specs=[pl.BlockSpec(memory_space=pltpu.SMEM),
                  pl.BlockSpec((1, 1, N_HEADS, qb, HEAD_DIM), lambda i, j: (i, j, 0, 0, 0)),
                  pl.BlockSpec((1, 1, IDX_HEADS, qb, IDX_DIM), lambda i, j: (i, j, 0, 0, 0)),
                  pl.BlockSpec((1, IDX_HEADS, qb), lambda i, j: (i, 0, j)),
                  pl.BlockSpec((1, lp, kv), lambda i, j: (i, 0, 0)),
                  pl.BlockSpec((1, N_KV_HEADS, lp // KEY_TILE, HEAD_DIM, KEY_TILE), lambda i, j: (i, 0, 0, 0, 0)),
                  pl.BlockSpec((1, lp, IDX_DIM), lambda i, j: (i, 0, 0))],
        out_specs=pl.BlockSpec((1, qb, D_MODEL), lambda i, j: (i, j, 0)),
        out_shape=jax.ShapeDtypeStruct((b, nblk * qb, D_MODEL), BF16),
        scratch_shapes=[pltpu.VMEM((lp, qb), I32), pltpu.VMEM((lp, qb), F32), pltpu.VMEM((lp, GROUP * qb), F32),
                        pltpu.VMEM((N_HEADS, NEAR_SLOTS, KEY_TILE, qb), F32), pltpu.VMEM((1, qb), I32)],
        compiler_params=_params("arbitrary", "arbitrary"),
        name="dsa_attn",
    )(rel_bias, q, iq, iwt, kb, vt, ikb)


def _dsa_mixer(xn, k_past, v_past, ik_past, w, b, t):
    past = k_past.shape[1]
    kv = N_KV_HEADS * HEAD_DIM
    q, k, v, iq, ik, iw = _dsa_in(xn, w)
    n_keys = past + t
    qb = QUERY_BLOCK
    tq = -(-t // qb) * qb
    lp = -(-n_keys // KEY_TILE) * KEY_TILE
    pad_q = lambda z: jnp.pad(z.reshape(b, t, z.shape[-1]), ((0, 0), (0, tq - t), (0, 0)))
    by_head = lambda z, nh: pad_q(z).reshape(b, tq // qb, qb, nh, z.shape[-1] // nh).transpose(0, 1, 3, 2, 4)
    keys = lambda old, new: jnp.pad(
        jnp.concatenate([old.reshape(b, past, new.shape[-1]).astype(BF16),
                         new.reshape(b, t, new.shape[-1]).astype(BF16)], axis=1),
        ((0, 0), (0, lp - n_keys), (0, 0)))
    kb = keys(k_past, k)
    vt = keys(v_past, v).reshape(b, lp // KEY_TILE, KEY_TILE, N_KV_HEADS, HEAD_DIM).transpose(0, 3, 1, 4, 2)
    ikb = keys(ik_past, ik)
    o = _dsa_attn(w["rel_bias"], by_head(q, N_HEADS), by_head(iq, IDX_HEADS), pad_q(iw).transpose(0, 2, 1),
                  kb, vt, ikb, past=past, n_keys=n_keys)
    o = o[:, :t].reshape(b * t, D_MODEL)
    return (o, k.reshape(b, t, N_KV_HEADS, HEAD_DIM), v.reshape(b, t, N_KV_HEADS, HEAD_DIM),
            ik.reshape(b, t, IDX_DIM))


def _gla_in_kernel(xn_ref, wq_ref, wk_ref, wv_ref, wr_ref, wa_ref, wa2_ref, ba_ref,
                   q_ref, k_ref, v_ref, r_ref, g_ref):
    x = xn_ref[...]
    q_ref[...] = _dot(x, wq_ref[...]) * (GLA_DK ** -0.5)
    k_ref[...] = _dot(x, wk_ref[...])
    v_ref[...] = _dot(x, wv_ref[...]).astype(BF16)
    r_ref[...] = _dot(x, wr_ref[...])
    z = _dot(_dot(x, wa_ref[...]).astype(BF16), wa2_ref[...]) + ba_ref[...]
    g_ref[...] = (jnp.minimum(z, 0.0) - jnp.log1p(jnp.exp(-jnp.abs(z)))) * (1.0 / GLA_GATE_NORM)


def _gla_in(xn, w):
    n = xn.shape[0]
    tm = _row_tile(n, 512)
    qk = GLA_HEADS * GLA_DK
    row = lambda wd: pl.BlockSpec((tm, wd), lambda i: (i, 0))
    return pl.pallas_call(
        _gla_in_kernel,
        grid=(n // tm,),
        in_specs=[row(D_MODEL), _const_spec((D_MODEL, qk)), _const_spec((D_MODEL, qk)),
                  _const_spec((D_MODEL, D_MODEL)), _const_spec((D_MODEL, D_MODEL)),
                  _const_spec((D_MODEL, GLA_RANK)), _const_spec((GLA_RANK, qk)), _const_spec((1, qk))],
        out_specs=[row(qk), row(qk), row(D_MODEL), row(D_MODEL), row(qk)],
        out_shape=[jax.ShapeDtypeStruct((n, qk), F32), jax.ShapeDtypeStruct((n, qk), F32),
                   jax.ShapeDtypeStruct((n, D_MODEL), BF16), jax.ShapeDtypeStruct((n, D_MODEL), F32),
                   jax.ShapeDtypeStruct((n, qk), F32)],
        compiler_params=_params("parallel"),
        name="gla_in",
    )(xn, w["wq"], w["wk"], w["wv"], w["wr"], w["wa"], w["wa2"], w["ba"])


def _gla_chunk(q, k, g, v, st, tri_c, tri_sb):
    c = q.shape[0]
    sb = GLA_SUB
    cum = jnp.dot(tri_c, g, precision=HIGHEST, preferred_element_type=F32)
    total = cum[c - 1:c, :]
    o_inter = _dot_nt((q * jnp.exp(cum)).astype(BF16), st.astype(BF16))
    st_new = st * jnp.exp(total) + _dot_tn(v, (k * jnp.exp(total - cum)).astype(BF16))
    o_rows = []
    cum2 = cum * LOG2_E
    for i in range(c // sb):
        r = slice(i * sb, (i + 1) * sb)
        ci = cum2[r]
        decay = jnp.exp2(ci[:, None, :] - ci[None, :, :])
        sc = jnp.sum(decay * q[r][:, None, :] * k[r][None, :, :], axis=-1)
        sc = jnp.where(tri_sb, sc, 0.0)
        o_rows.append(o_inter[r] + _dot(sc.astype(BF16), v[r]))
    width = sb
    while width < c:
        for p in range(c // (2 * width)):
            lo = 2 * p * width
            left = slice(lo, lo + width)
            right = slice(lo + width, lo + 2 * width)
            edge = cum[lo + width - 1:lo + width, :]
            q_hat = (q[right] * jnp.exp(cum[right] - edge)).astype(BF16)
            k_hat = (k[left] * jnp.exp(edge - cum[left])).astype(BF16)
            o_pair = _dot(_dot_nt(q_hat, k_hat).astype(BF16), v[left])
            for j in range(width // sb):
                o_rows[(lo + width) // sb + j] += o_pair[j * sb:(j + 1) * sb]
        width *= 2
    return jnp.concatenate(o_rows, axis=0), st_new


def _gla_core_kernel(q_ref, k_ref, g_ref, v_ref, r_ref, s0_ref, gain_ref, a_ref, sout_ref, st_ref, *, tt, c):
    sb = GLA_SUB
    t = pl.program_id(1)

    @pl.when(t == 0)
    def _():
        for h in range(GLA_HEADS):
            st_ref[h] = s0_ref[0, h].T

    tri_c = (lax.broadcasted_iota(I32, (c, c), 0) >= lax.broadcasted_iota(I32, (c, c), 1)).astype(F32)
    tri_sb = lax.broadcasted_iota(I32, (sb, sb), 0) >= lax.broadcasted_iota(I32, (sb, sb), 1)
    gain = gain_ref[...]

    def chunk(ci, carry):
        rows = pl.ds(pl.multiple_of(ci * c, c), c)
        for h in range(GLA_HEADS):
            dk = slice(h * GLA_DK, (h + 1) * GLA_DK)
            dv = slice(h * GLA_DV, (h + 1) * GLA_DV)
            o, st_ref[h] = _gla_chunk(q_ref[0, rows, dk], k_ref[0, rows, dk], g_ref[0, rows, dk],
                                      v_ref[0, rows, dv], st_ref[h], tri_c, tri_sb)
            r = r_ref[0, rows, dv]
            a_ref[0, rows, dv] = (_rms(o, gain) * (r * jax.nn.sigmoid(r))).astype(BF16)
        return carry

    lax.fori_loop(0, tt // c, chunk, 0)

    @pl.when(t == pl.num_programs(1) - 1)
    def _():
        for h in range(GLA_HEADS):
            sout_ref[0, h] = st_ref[h].T


def _gla_core(q, k, g, v, r, s0, gain):
    b, t, _ = q.shape
    tt = _row_tile(t, 256)
    c = min(tt, CHUNK)
    assert tt % c == 0 and c % GLA_SUB == 0 and (c // GLA_SUB) & (c // GLA_SUB - 1) == 0
    qk = GLA_HEADS * GLA_DK
    tok = lambda wd: pl.BlockSpec((1, tt, wd), lambda i, j: (i, j, 0))
    st_spec = pl.BlockSpec((1, GLA_HEADS, GLA_DK, GLA_DV), lambda i, j: (i, 0, 0, 0))
    return pl.pallas_call(
        functools.partial(_gla_core_kernel, tt=tt, c=c),
        grid=(b, t // tt),
        in_specs=[tok(qk), tok(qk), tok(qk), tok(D_MODEL), tok(D_MODEL), st_spec, _const_spec((1, GLA_DV))],
        out_specs=[tok(D_MODEL), st_spec],
        out_shape=[jax.ShapeDtypeStruct((b, t, D_MODEL), BF16),
                   jax.ShapeDtypeStruct((b, GLA_HEADS, GLA_DK, GLA_DV), F32)],
        scratch_shapes=[pltpu.VMEM((GLA_HEADS, GLA_DV, GLA_DK), F32)],
        compiler_params=_params("arbitrary", "arbitrary"),
        name="gla_core",
    )(q, k, g, v, r, s0, gain)


def _conv_kernel(xn_ref, wb_ref, wc_ref, wh_ref, prev_ref, cw_ref, cb_ref, a_ref, new_ref, ue_ref, *, tt):
    t = pl.program_id(1)
    pad = 8
    lo = pad - (CONV_W - 1)

    @pl.when(t == 0)
    def _():
        ue_ref[lo:pad, :] = prev_ref[0]

    x = xn_ref[0]
    u = _dot(x, wc_ref[...]) * _dot(x, wh_ref[...])
    ue_ref[pad:pad + tt, :] = u
    conv = cb_ref[...] + cw_ref[CONV_W - 1:CONV_W, :] * u
    for j in range(CONV_W - 1):
        conv = conv + cw_ref[j:j + 1, :] * ue_ref[lo + j:lo + j + tt, :]
    a_ref[0] = (_dot(x, wb_ref[...]) * conv).astype(BF16)
    tail = ue_ref[lo + tt:pad + tt, :]
    ue_ref[lo:pad, :] = tail

    @pl.when(t == pl.num_programs(1) - 1)
    def _():
        new_ref[0] = tail


def _conv_front(xn, prev, w):
    b, t, _ = xn.shape
    tt = _row_tile(t, 512)
    tok = pl.BlockSpec((1, tt, D_MODEL), lambda i, j: (i, j, 0))
    st = pl.BlockSpec((1, CONV_W - 1, D_MODEL), lambda i, j: (i, 0, 0))
    sq = _const_spec((D_MODEL, D_MODEL))
    return pl.pallas_call(
        functools.partial(_conv_kernel, tt=tt),
        grid=(b, t // tt),
        in_specs=[tok, sq, sq, sq, st, _const_spec((CONV_W, D_MODEL)), _const_spec((1, D_MODEL))],
        out_specs=[tok, st],
        out_shape=[jax.ShapeDtypeStruct((b, t, D_MODEL), BF16),
                   jax.ShapeDtypeStruct((b, CONV_W - 1, D_MODEL), F32)],
        scratch_shapes=[pltpu.VMEM((8 + tt, D_MODEL), F32)],
        compiler_params=_params("arbitrary", "arbitrary"),
        name="conv_front",
    )(xn, w["wb"], w["wc"], w["wh"], prev, w["cw"], w["cb"])


def _trunk(x, p, pool_prev, k_past, v_past, ik_past, gla_prev, conv_prev, w):
    b, t, _ = x.shape
    n = b * t
    past = k_past.shape[1]
    flat = lambda z: z.reshape(n, z.shape[-1])
    post = w["post"]

    a, pool_new = _pool_front(x, pool_prev, w["norm_mix"][0], past)
    h, xn = _post_block(flat(x), flat(a), flat(p[0]), post[0], w["norm_mix"][1])

    a, k_new, v_new, ik_new = _dsa_mixer(xn, k_past, v_past, ik_past, w["dsa"], b, t)
    h, xn = _post_block(h, a, flat(p[1]), post[1], w["norm_mix"][2])

    q, k, v, r, g = _gla_in(xn, w["gla"])
    seq = lambda z: z.reshape(b, t, z.shape[-1])
    a, gla_new = _gla_core(seq(q), seq(k), seq(g), seq(v), seq(r), gla_prev, w["gla"]["gain"])
    h, xn = _post_block(h, flat(a), flat(p[2]), post[2], w["norm_mix"][3])

    a, conv_new = _conv_front(seq(xn), conv_prev, w["conv"])
    h, _ = _post_block(h, flat(a), flat(p[3]), post[3], None)
    return h.reshape(b, t, D_MODEL), pool_new, k_new, v_new, ik_new, gla_new, conv_new


def _prepare_weights(norm_mix, norm_mlp, norm_ple, w_mlp1, w_mlp2, w_ple_proj, w_ple_gate, w_pool, b_pool,
                     pool_scale, w_dsa_in, w_dsa_out, q_norm, k_norm, rel_bias, w_gla_in, w_gla_a2, b_gla_a,
                     gla_norm, w_gla_out, w_conv_in, conv_w, conv_b, w_conv_out):
    bf = lambda z: z.astype(BF16)
    vec = lambda z: z.reshape(1, -1).astype(F32)
    depth = norm_mix.shape[0]
    w_pool_bd = jax.scipy.linalg.block_diag(*[w_pool[g] for g in range(w_pool.shape[0])])
    wouts = [w_pool_bd, w_dsa_out, w_gla_out, w_conv_out]
    zeros = jnp.zeros((1, D_MODEL), F32)
    ones = jnp.ones((1, D_MODEL), F32)
    bouts = [vec(b_pool), zeros, zeros, zeros]
    souts = [vec(pool_scale), ones, ones, ones]
    post = [dict(wout=bf(wouts[i]), bout=bouts[i], sout=souts[i], gmlp=vec(norm_mlp[i]), w1=bf(w_mlp1[i]),
                 w2=bf(w_mlp2[i]), gple=vec(norm_ple[i]), wg=bf(w_ple_gate[i]), wp=bf(w_ple_proj[i]))
            for i in range(depth)]

    def split(wm, sizes):
        out, o = [], 0
        for s in sizes:
            out.append(bf(wm[:, o:o + s]))
            o += s
        return out

    kv = N_KV_HEADS * HEAD_DIM
    wq, wk, wv, wiq, wik, wiw = split(w_dsa_in, (D_MODEL, kv, kv, IDX_HEADS * IDX_DIM, IDX_DIM, IDX_HEADS))
    dsa = dict(wq=wq, wk=wk, wv=wv, wiq=wiq, wik=wik, wiw=wiw, qg=vec(q_norm), kg=vec(k_norm),
               rel_bias=rel_bias.astype(F32))
    qk = GLA_HEADS * GLA_DK
    gq, gk, gv, gr, ga = split(w_gla_in, (qk, qk, D_MODEL, D_MODEL, GLA_RANK))
    gla = dict(wq=gq, wk=gk, wv=gv, wr=gr, wa=ga, wa2=bf(w_gla_a2), ba=vec(b_gla_a), gain=vec(gla_norm))
    cb_, cc_, ch_ = split(w_conv_in, (D_MODEL, D_MODEL, D_MODEL))
    conv = dict(wb=cb_, wc=cc_, wh=ch_, cw=conv_w.astype(F32), cb=vec(conv_b))
    return dict(post=post, norm_mix=[vec(norm_mix[i]) for i in range(depth)], dsa=dsa, gla=gla, conv=conv)


def kernel(x_prompt, x_sample, p_prompt, p_sample, state_pool, cache_k, cache_v, cache_idx_k, state_gla, state_conv, norm_mix, norm_mlp, norm_ple, w_mlp1, w_mlp2, w_ple_proj, w_ple_gate, w_pool, b_pool, pool_scale, w_dsa_in, w_dsa_out, q_norm, k_norm, rel_bias, w_gla_in, w_gla_a2, b_gla_a, gla_norm, w_gla_out, w_conv_in, conv_w, conv_b, w_conv_out):
    w = _prepare_weights(norm_mix, norm_mlp, norm_ple, w_mlp1, w_mlp2, w_ple_proj, w_ple_gate, w_pool, b_pool,
                         pool_scale, w_dsa_in, w_dsa_out, q_norm, k_norm, rel_bias, w_gla_in, w_gla_a2, b_gla_a,
                         gla_norm, w_gla_out, w_conv_in, conv_w, conv_b, w_conv_out)
    bp = x_prompt.shape[0]
    dt = x_prompt.dtype
    y_p, pool_p, k_p, v_p, ik_p, gla_p, conv_p = _trunk(
        x_prompt, p_prompt,
        jnp.zeros((bp, POOL_CTX, D_MODEL), dt),
        jnp.zeros((bp, 0, N_KV_HEADS, HEAD_DIM), dt),
        jnp.zeros((bp, 0, N_KV_HEADS, HEAD_DIM), dt),
        jnp.zeros((bp, 0, IDX_DIM), dt),
        jnp.zeros((bp, GLA_HEADS, GLA_DK, GLA_DV), dt),
        jnp.zeros((bp, CONV_W - 1, D_MODEL), dt),
        w)
    y_s, pool_s, k_s, v_s, ik_s, gla_s, conv_s = _trunk(
        x_sample, p_sample, state_pool, cache_k, cache_v, cache_idx_k, state_gla, state_conv, w)
    return (y_p, y_s, pool_p, pool_s, k_p, v_p, ik_p, k_s, v_s, ik_s, gla_p, gla_s, conv_p, conv_s)
```

```python
import functools

import jax
import jax.numpy as jnp
from jax import lax
from jax.experimental import pallas as pl
from jax.experimental.pallas import tpu as pltpu

F32 = jnp.float32
BF16 = jnp.bfloat16
I32 = jnp.int32

D_MODEL = 1024
D_FF = 4 * D_MODEL
PLE_DIM = 256
EPS = 1e-6
CHUNK = 64
CHUNK_SHIFT = 6
POOL_WINDOWS = (2, 4, 8, 16)
POOL_GC = D_MODEL // len(POOL_WINDOWS)
POOL_CTX = max(POOL_WINDOWS) - 1
N_HEADS = 8
N_KV_HEADS = 2
HEAD_DIM = D_MODEL // N_HEADS
GROUP = N_HEADS // N_KV_HEADS
IDX_HEADS = 8
IDX_DIM = 64
TOPK_MAX = 256
REL_BUCKETS = 32
REL_LOG_EDGES = (12, 16, 23, 32, 46, 64, 91)
REL_FAR_BUCKET = REL_BUCKETS // 2 - 1
GLA_HEADS = 4
GLA_DK = D_MODEL // 2 // GLA_HEADS
GLA_DV = D_MODEL // GLA_HEADS
GLA_RANK = 16
GLA_GATE_NORM = 16.0
GLA_SUB = 16
CONV_W = 3

V7X_VMEM_LIMIT_BYTES = 56 * 1024 * 1024
SUBLANES = 8
INT_MIN = -(2 ** 31)
QUERY_BLOCK = 128
KEY_TILE = 256
NEAR_SLOTS = KEY_TILE // QUERY_BLOCK + 2

HIGHEST = lax.Precision.HIGHEST
LOG2_E = 1.4426950408889634


def _params(*sem):
    return pltpu.CompilerParams(dimension_semantics=sem, vmem_limit_bytes=V7X_VMEM_LIMIT_BYTES)


def _const_spec(shape):
    nd = len(shape)
    return pl.BlockSpec(shape, lambda *_: (0,) * nd, pipeline_mode=pl.Buffered(1))


def _rms(x, g):
    return x * lax.rsqrt(jnp.mean(x * x, axis=-1, keepdims=True) + EPS) * g


def _dot(a, b):
    return jnp.dot(a, b, preferred_element_type=F32)


def _dot_nt(a, b):
    return lax.dot_general(a, b, (((1,), (1,)), ((), ())), preferred_element_type=F32)


def _dot_tn(a, b):
    return lax.dot_general(a, b, (((0,), (0,)), ((), ())), preferred_element_type=F32)


def _fold_rows(x, op):
    n = x.shape[0] // 8
    assert n & (n - 1) == 0
    parts = x.reshape(n, 8, x.shape[1])
    while n > 1:
        n //= 2
        parts = op(parts[:n], parts[n:])
    return parts[0]


def _row_tile(n, want):
    t = min(n, want)
    assert n % t == 0
    return t


def _post_kernel(h_ref, a_ref, p_ref, wout_ref, bout_ref, sout_ref, gmlp_ref, w1_ref, w2_ref, gple_ref, wg_ref,
                 wp_ref, *rest, ff_chunk, with_next):
    if with_next:
        gnext_ref, h_out_ref, xn_out_ref = rest
    else:
        (h_out_ref,) = rest
    y = (_dot(a_ref[...], wout_ref[...]) + bout_ref[...]) * sout_ref[...]
    h1 = h_ref[...] + y
    hn = _rms(h1, gmlp_ref[...]).astype(BF16)
    acc = h1
    for c in range(D_FF // ff_chunk):
        cols = slice(c * ff_chunk, (c + 1) * ff_chunk)
        hid = jnp.square(jnp.maximum(_dot(hn, w1_ref[:, cols]), 0.0)).astype(BF16)
        acc = acc + _dot(hid, w2_ref[cols, :])
    gate = jax.nn.sigmoid(_dot(_rms(acc, gple_ref[...]).astype(BF16), wg_ref[...]))
    h3 = acc + _dot(p_ref[...].astype(BF16), wp_ref[...]) * gate
    h_out_ref[...] = h3
    if with_next:
        xn_out_ref[...] = _rms(h3, gnext_ref[...]).astype(BF16)


def _layer_spec(shape, layer):
    nd = len(shape)
    return pl.BlockSpec((None,) + tuple(shape), lambda *_: (layer,) + (0,) * nd, pipeline_mode=pl.Buffered(1))


def _post_block(h, a, p, pw, layer, wout, bout, sout):
    n = h.shape[0]
    tm = _row_tile(n, 512)
    depth = pw["w1"].shape[0]
    with_next = layer + 1 < depth
    row = lambda w: pl.BlockSpec((tm, w), lambda i: (i, 0))
    vec = _const_spec((1, D_MODEL))
    lvec = lambda l: _layer_spec((1, D_MODEL), l)
    out_shape = [jax.ShapeDtypeStruct((n, D_MODEL), F32)]
    out_specs = [row(D_MODEL)]
    in_specs = [row(D_MODEL), row(D_MODEL), pl.BlockSpec((None, tm, PLE_DIM), lambda i: (layer, i, 0)),
                _const_spec((D_MODEL, D_MODEL)), vec, vec, lvec(layer),
                _layer_spec((D_MODEL, D_FF), layer), _layer_spec((D_FF, D_MODEL), layer), lvec(layer),
                _layer_spec((D_MODEL, D_MODEL), layer), _layer_spec((PLE_DIM, D_MODEL), layer)]
    args = [h, a, p, wout, bout, sout, pw["gmlp"], pw["w1"], pw["w2"], pw["gple"], pw["wg"], pw["wp"]]
    if with_next:
        out_shape.append(jax.ShapeDtypeStruct((n, D_MODEL), BF16))
        out_specs.append(row(D_MODEL))
        in_specs.append(lvec(layer + 1))
        args.append(pw["gmix"])
    outs = pl.pallas_call(
        functools.partial(_post_kernel, ff_chunk=1024, with_next=with_next),
        grid=(n // tm,),
        in_specs=in_specs,
        out_specs=out_specs,
        out_shape=out_shape,
        compiler_params=_params("parallel"),
        name="post_block",
    )(*args)
    return (outs[0], outs[1]) if with_next else (outs[0], None)


def _pool_kernel(x_ref, prev_ref, g_ref, a_ref, pool_ref, xe_ref, *, tt, pos0):
    t = pl.program_id(1)
    ctx = POOL_CTX + 1

    @pl.when(t == 0)
    def _():
        xe_ref[0:1, :] = jnp.zeros((1, D_MODEL), F32)
        xe_ref[1:ctx, :] = prev_ref[0]

    xn = _rms(x_ref[0], g_ref[...])
    xe_ref[ctx:ctx + tt, :] = xn
    pos1 = (pos0 + 1 + t * tt + lax.broadcasted_iota(I32, (tt, 1), 0)).astype(F32)
    for g, w in enumerate(POOL_WINDOWS):
        cols = slice(g * POOL_GC, (g + 1) * POOL_GC)
        win = xn[:, cols]
        for j in range(1, w):
            win = win + xe_ref[ctx - j:ctx - j + tt, cols]
        mean = win / jnp.minimum(float(w), pos1)
        a_ref[0, :, cols] = (mean - xn[:, cols]).astype(BF16)
    tail = xe_ref[tt:tt + ctx, :]
    xe_ref[0:ctx, :] = tail

    @pl.when(t == pl.num_programs(1) - 1)
    def _():
        pool_ref[0] = xe_ref[tt + 1:tt + ctx, :]


def _pool_front(x, prev, g, pos0):
    b, t, _ = x.shape
    tt = _row_tile(t, 512)
    return pl.pallas_call(
        functools.partial(_pool_kernel, tt=tt, pos0=pos0),
        grid=(b, t // tt),
        in_specs=[pl.BlockSpec((1, tt, D_MODEL), lambda i, j: (i, j, 0)),
                  pl.BlockSpec((1, POOL_CTX, D_MODEL), lambda i, j: (i, 0, 0)),
                  _const_spec((1, D_MODEL))],
        out_specs=[pl.BlockSpec((1, tt, D_MODEL), lambda i, j: (i, j, 0)),
                   pl.BlockSpec((1, POOL_CTX, D_MODEL), lambda i, j: (i, 0, 0))],
        out_shape=[jax.ShapeDtypeStruct((b, t, D_MODEL), BF16),
                   jax.ShapeDtypeStruct((b, POOL_CTX, D_MODEL), F32)],
        scratch_shapes=[pltpu.VMEM((POOL_CTX + 1 + tt, D_MODEL), F32)],
        compiler_params=_params("arbitrary", "arbitrary"),
        name="pool_front",
    )(x, prev, g)


DSA_KV = N_KV_HEADS * HEAD_DIM
DSA_IQ = IDX_HEADS * IDX_DIM
DSA_COLS = (0, D_MODEL, D_MODEL + DSA_KV, D_MODEL + 2 * DSA_KV, D_MODEL + 2 * DSA_KV + DSA_IQ,
            D_MODEL + 2 * DSA_KV + DSA_IQ + IDX_DIM, D_MODEL + 2 * DSA_KV + DSA_IQ + IDX_DIM + IDX_HEADS)


def _dsa_in_kernel(xn_ref, w_ref, qg_ref, kg_ref, q_ref, iq_ref, iw_ref, k2_ref, v2_ref, ik_ref, *key_refs,
                   tm, for_attn):
    x = xn_ref[...]
    qb = QUERY_BLOCK
    if for_attn:
        kb_ref, vt_ref, ikb_ref = key_refs
    col = lambda i: slice(DSA_COLS[i], DSA_COLS[i + 1])
    q = _dot(x, w_ref[:, col(0)])
    iq = _dot(x, w_ref[:, col(3)]).astype(BF16)
    for h in range(N_HEADS):
        qh = _rms(q[:, h * HEAD_DIM:(h + 1) * HEAD_DIM], qg_ref[...]).astype(BF16)
        iqh = iq[:, h * IDX_DIM:(h + 1) * IDX_DIM]
        if for_attn:
            for j in range(tm // qb):
                q_ref[j, h] = qh[j * qb:(j + 1) * qb]
                iq_ref[j, h] = iqh[j * qb:(j + 1) * qb]
        else:
            q_ref[:, h * HEAD_DIM:(h + 1) * HEAD_DIM] = qh
            iq_ref[:, h * IDX_DIM:(h + 1) * IDX_DIM] = iqh
    k = _dot(x, w_ref[:, col(1)])
    v = _dot(x, w_ref[:, col(2)])
    tail = _dot(x, w_ref[:, DSA_COLS[4]:DSA_COLS[6]])
    ik = tail[:, :IDX_DIM]
    ik_ref[...] = ik
    iw_ref[...] = tail[:, IDX_DIM:]
    for h in range(N_KV_HEADS):
        cols = slice(h * HEAD_DIM, (h + 1) * HEAD_DIM)
        kh = _rms(k[:, cols], kg_ref[...])
        k2_ref[pl.ds(h, tm, stride=N_KV_HEADS), :] = kh
        v2_ref[pl.ds(h, tm, stride=N_KV_HEADS), :] = v[:, cols]
        if for_attn:
            kb_ref[:, cols] = kh.astype(BF16)
            for j in range(tm // KEY_TILE):
                vt_ref[j, h] = v[j * KEY_TILE:(j + 1) * KEY_TILE, cols].T.astype(BF16)
    if for_attn:
        ikb_ref[...] = ik.astype(BF16)


def _dsa_in(xn, w, for_attn):
    n = xn.shape[0]
    tm = _row_tile(n, 512)
    qb = QUERY_BLOCK
    row = lambda wd: pl.BlockSpec((tm, wd), lambda i: (i, 0))
    lead = lambda shape: pl.BlockSpec(shape, lambda i: (i,) + (0,) * (len(shape) - 1))
    if for_attn:
        assert tm % KEY_TILE == 0 and tm % qb == 0
        q_specs = [lead((tm // qb, N_HEADS, qb, HEAD_DIM)), lead((tm // qb, IDX_HEADS, qb, IDX_DIM))]
        q_shapes = [jax.ShapeDtypeStruct((n // qb, N_HEADS, qb, HEAD_DIM), BF16),
                    jax.ShapeDtypeStruct((n // qb, IDX_HEADS, qb, IDX_DIM), BF16)]
        key_specs = [row(DSA_KV), lead((tm // KEY_TILE, N_KV_HEADS, HEAD_DIM, KEY_TILE)), row(IDX_DIM)]
        key_shapes = [jax.ShapeDtypeStruct((n, DSA_KV), BF16),
                      jax.ShapeDtypeStruct((n // KEY_TILE, N_KV_HEADS, HEAD_DIM, KEY_TILE), BF16),
                      jax.ShapeDtypeStruct((n, IDX_DIM), BF16)]
    else:
        q_specs = [row(D_MODEL), row(DSA_IQ)]
        q_shapes = [jax.ShapeDtypeStruct((n, D_MODEL), BF16), jax.ShapeDtypeStruct((n, DSA_IQ), BF16)]
        key_specs, key_shapes = [], []
    kv2 = pl.BlockSpec((N_KV_HEADS * tm, HEAD_DIM), lambda i: (i, 0))
    return pl.pallas_call(
        functools.partial(_dsa_in_kernel, tm=tm, for_attn=for_attn),
        grid=(n // tm,),
        in_specs=[row(D_MODEL), _const_spec((D_MODEL, DSA_COLS[-1])), _const_spec((1, HEAD_DIM)),
                  _const_spec((1, HEAD_DIM))],
        out_specs=q_specs + [row(IDX_HEADS), kv2, kv2, row(IDX_DIM)] + key_specs,
        out_shape=q_shapes + [jax.ShapeDtypeStruct((n, IDX_HEADS), F32),
                              jax.ShapeDtypeStruct((N_KV_HEADS * n, HEAD_DIM), F32),
                              jax.ShapeDtypeStruct((N_KV_HEADS * n, HEAD_DIM), F32),
                              jax.ShapeDtypeStruct((n, IDX_DIM), F32)] + key_shapes,
        compiler_params=_params("parallel"),
        name="dsa_in",
    )(xn, w["w_in"], w["qg"], w["kg"])


def _dsa_keys_kernel(ck_ref, cv_ref, cik_ref, k2_ref, v2_ref, ik_ref, kb_ref, vt_ref, ikb_ref, *, past, t):
    lp = kb_ref.shape[1]
    nh = N_KV_HEADS
    for h in range(nh):
        cols = slice(h * HEAD_DIM, (h + 1) * HEAD_DIM)
        kb_ref[0, 0:past, cols] = ck_ref[0, pl.ds(h, past, stride=nh), :].astype(BF16)
        kb_ref[0, past:past + t, cols] = k2_ref[0, pl.ds(h, t, stride=nh), :].astype(BF16)
        kb_ref[0, past + t:lp, cols] = jnp.zeros((lp - past - t, HEAD_DIM), BF16)
        for j in range(past // KEY_TILE):
            vt_ref[0, j, h] = cv_ref[0, pl.ds(h + nh * KEY_TILE * j, KEY_TILE, stride=nh), :].T.astype(BF16)
        v_new = jnp.concatenate([v2_ref[0, pl.ds(h, t, stride=nh), :], jnp.zeros((KEY_TILE - t, HEAD_DIM), F32)],
                                axis=0)
        vt_ref[0, past // KEY_TILE, h] = v_new.T.astype(BF16)
    ikb_ref[0, 0:past, :] = cik_ref[0].astype(BF16)
    ikb_ref[0, past:past + t, :] = ik_ref[0].astype(BF16)
    ikb_ref[0, past + t:lp, :] = jnp.zeros((lp - past - t, IDX_DIM), BF16)


def _dsa_keys(cache_k, cache_v, cache_ik, k2, v2, ik):
    b, past = cache_k.shape[:2]
    t = ik.shape[1]
    lp = past + KEY_TILE
    assert past % KEY_TILE == 0 and t <= KEY_TILE and t % 8 == 0
    nh = N_KV_HEADS
    per_b = lambda *shape: pl.BlockSpec((1,) + shape, lambda i: (i,) + (0,) * len(shape))
    return pl.pallas_call(
        functools.partial(_dsa_keys_kernel, past=past, t=t),
        grid=(b,),
        in_specs=[per_b(nh * past, HEAD_DIM), per_b(nh * past, HEAD_DIM), per_b(past, IDX_DIM),
                  per_b(nh * t, HEAD_DIM), per_b(nh * t, HEAD_DIM), per_b(t, IDX_DIM)],
        out_specs=[per_b(lp, DSA_KV), per_b(lp // KEY_TILE, nh, HEAD_DIM, KEY_TILE), per_b(lp, IDX_DIM)],
        out_shape=[jax.ShapeDtypeStruct((b, lp, DSA_KV), BF16),
                   jax.ShapeDtypeStruct((b, lp // KEY_TILE, nh, HEAD_DIM, KEY_TILE), BF16),
                   jax.ShapeDtypeStruct((b, lp, IDX_DIM), BF16)],
        compiler_params=_params("parallel"),
        name="dsa_keys",
    )(cache_k.reshape(b, nh * past, HEAD_DIM), cache_v.reshape(b, nh * past, HEAD_DIM), cache_ik, k2, v2, ik)


def _dsa_attn_kernel(rb_ref, q_ref, iq_ref, iwt_ref, kb_ref, vt_ref, ik_ref, o_ref,
                     key_ref, msk_ref, lg_ref, bias_ref, j_ref, *, past, n_keys, topk, idx_bits):
    kt_sz = KEY_TILE
    qb = QUERY_BLOCK
    b = pl.program_id(0)
    i = pl.program_id(1)
    q0 = past + i * qb
    kend = jnp.minimum(n_keys, (((q0 + qb - 1) >> CHUNK_SHIFT) + 1) * CHUNK)
    nkt = (kend + kt_sz - 1) // kt_sz
    row_i = lax.broadcasted_iota(I32, (kt_sz, qb), 0)
    col_i = lax.broadcasted_iota(I32, (kt_sz, qb), 1)

    @pl.when((b == 0) & (i == 0))
    def _():
        for w in range(1, NEAR_SLOTS):
            rel = row_i - col_i + (w - (NEAR_SLOTS - 1)) * qb
            n = jnp.abs(rel)
            log_bucket = jnp.full_like(n, REL_BUCKETS // 4)
            for edge in REL_LOG_EDGES:
                log_bucket = log_bucket + (n >= edge).astype(I32)
            bucket = jnp.where(n < REL_BUCKETS // 4, n, log_bucket) + jnp.where(rel > 0, REL_BUCKETS // 2, 0)
            for h in range(N_HEADS):
                val = jnp.zeros((kt_sz, qb), F32)
                for bk in range(REL_BUCKETS):
                    val = jnp.where(bucket == bk, rb_ref[bk, h], val)
                bias_ref[h, w] = val
        for h in range(N_HEADS):
            bias_ref[h, 0] = jnp.full((kt_sz, qb), rb_ref[REL_FAR_BUCKET, h], F32)

    iw = iwt_ref[0]
    iq_all = iq_ref[0, 0].reshape(IDX_HEADS * qb, IDX_DIM)
    q_chunk = (q0 + col_i) >> CHUNK_SHIFT

    def tile_rows(jt):
        return pl.ds(pl.multiple_of(jt * kt_sz, kt_sz), kt_sz)

    def score_tile(jt, carry):
        ikt = ik_ref[0, tile_rows(jt), :]
        s = jnp.zeros((kt_sz, qb), F32)
        for hp in range(IDX_HEADS // 2):
            s2 = _dot_nt(ikt, iq_all[2 * hp * qb:(2 * hp + 2) * qb, :])
            for h in (2 * hp, 2 * hp + 1):
                s = s + iw[h:h + 1, :] * jnp.maximum(s2[:, (h - 2 * hp) * qb:(h - 2 * hp + 1) * qb], 0.0)
        s = s * (IDX_DIM ** -0.5)
        bits = pltpu.bitcast(s, I32)
        key = bits ^ ((bits >> 31) & 0x7FFFFFFF)
        kpos = jt * kt_sz + row_i
        adm = ((kpos >> CHUNK_SHIFT) <= q_chunk) & (kpos < n_keys)
        key_ref[tile_rows(jt), :] = jnp.where(adm, key, INT_MIN)
        return carry

    lax.fori_loop(0, nkt, score_tile, 0)

    def count(pred):
        def body(jt, acc):
            hit = jnp.where(pred(key_ref[tile_rows(jt), :], jt), 1.0, 0.0)
            return acc + _fold_rows(hit, jnp.add)

        acc = lax.fori_loop(0, nkt, body, jnp.zeros((8, qb), F32))
        return jnp.sum(acc, axis=0, keepdims=True)

    kf = float(topk)
    n_pos = count(lambda kk, jt: kk >= 0)
    n_adm = count(lambda kk, jt: kk > INT_MIN)
    thr0 = jnp.where(n_pos >= kf, 0, INT_MIN).astype(I32)
    n_ge0 = jnp.where(n_pos >= kf, n_pos, n_adm)

    def bisect(p, state):
        thr, n_ge = state
        cand = thr | jnp.left_shift(jnp.int32(1), 30 - p)
        n_cand = count(lambda kk, jt: kk >= cand)
        take = n_cand >= kf
        return jnp.where(take, cand, thr), jnp.where(take, n_cand, n_ge)

    thr, n_ge = lax.fori_loop(0, 31, bisect, (thr0, n_ge0))
    has_thr = thr > INT_MIN
    j_ref[...] = jnp.where(has_thr, n_keys, -1).astype(I32)
    tied_cut = jnp.max(jnp.where(has_thr & (n_ge > kf), 1, 0)) > 0

    @pl.when(tied_cut)
    def _():
        need = kf - count(lambda kk, jt: kk > thr)
        lo = jnp.zeros((1, qb), I32)
        for bit in range(idx_bits - 1, -1, -1):
            cand = lo + (1 << bit)
            below = count(lambda kk, jt: (kk == thr) & ((jt * kt_sz + row_i) < cand))
            lo = jnp.where(below < need, cand, lo)
        j_ref[...] = jnp.where(has_thr, lo, -1)

    j_last = j_ref[...]

    def mask_tile(jt, carry):
        kk = key_ref[tile_rows(jt), :]
        sel = (kk > thr) | ((kk == thr) & ((jt * kt_sz + row_i) <= j_last))
        msk_ref[tile_rows(jt), :] = jnp.where(sel, 0.0, -jnp.inf)
        return carry

    lax.fori_loop(0, nkt, mask_tile, 0)

    gq = GROUP * qb
    slot0 = (past // qb + i) - (NEAR_SLOTS - 1)

    def bias_slot(jt):
        w = jt * (kt_sz // qb) - slot0
        return jnp.where(w >= 1, w, 0)

    kv_heads = range(N_KV_HEADS)
    q_groups = [q_ref[0, 0, kvh * GROUP:(kvh + 1) * GROUP].reshape(gq, HEAD_DIM) for kvh in kv_heads]

    def logits_tile(jt, m8):
        msk = msk_ref[tile_rows(jt), :]
        w = bias_slot(jt)
        new_m8 = []
        for kvh in kv_heads:
            kt = kb_ref[0, tile_rows(jt), kvh * HEAD_DIM:(kvh + 1) * HEAD_DIM]
            lg = _dot_nt(kt, q_groups[kvh]) * (HEAD_DIM ** -0.5)
            parts = []
            for g in range(GROUP):
                h = kvh * GROUP + g
                lgh = lg[:, g * qb:(g + 1) * qb] + bias_ref[h, w] + msk
                lg_ref[tile_rows(jt), h * qb:(h + 1) * qb] = lgh
                parts.append(_fold_rows(lgh, jnp.maximum))
            new_m8.append(jnp.maximum(m8[kvh], jnp.concatenate(parts, axis=1)))
        return tuple(new_m8)

    m8 = lax.fori_loop(0, nkt, logits_tile, tuple(jnp.full((8, gq), -jnp.inf, F32) for _ in kv_heads))
    m = [jnp.max(m8[kvh], axis=0, keepdims=True) for kvh in kv_heads]

    def pv_tile(jt, carry):
        new = []
        for kvh in kv_heads:
            acc, l8 = carry[kvh]
            p = jnp.exp(lg_ref[tile_rows(jt), kvh * gq:(kvh + 1) * gq] - m[kvh])
            new.append((acc + _dot(vt_ref[0, jt, kvh], p.astype(BF16)), l8 + _fold_rows(p, jnp.add)))
        return tuple(new)

    zero = (jnp.zeros((HEAD_DIM, gq), F32), jnp.zeros((8, gq), F32))
    acc_l8 = lax.fori_loop(0, nkt, pv_tile, tuple(zero for _ in kv_heads))
    for kvh in kv_heads:
        acc, l8 = acc_l8[kvh]
        o_t = acc / jnp.sum(l8, axis=0, keepdims=True)
        for g in range(GROUP):
            h = kvh * GROUP + g
            o_ref[0, :, h * HEAD_DIM:(h + 1) * HEAD_DIM] = o_t[:, g * qb:(g + 1) * qb].T.astype(BF16)


def _dsa_attn(rel_bias, q, iq, iwt, kb, vt, ikb, *, past, n_keys):
    b, nblk = q.shape[:2]
    lp = kb.shape[1]
    qb = QUERY_BLOCK
    assert lp % KEY_TILE == 0 and past % KEY_TILE == 0 and KEY_TILE == (NEAR_SLOTS - 2) * qb
    topk = min(TOPK_MAX, n_keys // 4)
    kv = N_KV_HEADS * HEAD_DIM
    return pl.pallas_call(
        functools.partial(_dsa_attn_kernel, past=past, n_keys=n_keys, topk=topk,
                          idx_bits=max(1, (lp - 1).bit_length())),
        grid=(b, nblk),
        in_specs=[pl.BlockSpec(memory_space=pltpu.SMEM),
                  pl.BlockSpec((1, 1, N_HEADS, qb, HEAD_DIM), lambda i, j: (i, j, 0, 0, 0)),
                  pl.BlockSpec((1, 1, IDX_HEADS, qb, IDX_DIM), lambda i, j: (i, j, 0, 0, 0)),
                  pl.BlockSpec((1, IDX_HEADS, qb), lambda i, j: (i, 0, j)),
                  pl.BlockSpec((1, lp, kv), lambda i, j: (i, 0, 0)),
                  pl.BlockSpec((1, lp // KEY_TILE, N_KV_HEADS, HEAD_DIM, KEY_TILE), lambda i, j: (i, 0, 0, 0, 0)),
                  pl.BlockSpec((1, lp, IDX_DIM), lambda i, j: (i, 0, 0))],
        out_specs=pl.BlockSpec((1, qb, D_MODEL), lambda i, j: (i, j, 0)),
        out_shape=jax.ShapeDtypeStruct((b, nblk * qb, D_MODEL), BF16),
        scratch_shapes=[pltpu.VMEM((lp, qb), I32), pltpu.VMEM((lp, qb), F32), pltpu.VMEM((lp, N_HEADS * qb), F32),
                        pltpu.VMEM((N_HEADS, NEAR_SLOTS, KEY_TILE, qb), F32), pltpu.VMEM((1, qb), I32)],
        compiler_params=_params("arbitrary", "arbitrary"),
        name="dsa_attn",
    )(rel_bias, q, iq, iwt, kb, vt, ikb)


def _dsa_mixer(xn, k_past, v_past, ik_past, w, b, t):
    past = k_past.shape[1]
    n_keys = past + t
    qb = QUERY_BLOCK
    nh = N_KV_HEADS
    if past == 0:
        assert t % KEY_TILE == 0
        q, iq, iw, k2, v2, ik, kb, vt, ikb = _dsa_in(xn, w, for_attn=True)
        q = q.reshape(b, t // qb, N_HEADS, qb, HEAD_DIM)
        iq = iq.reshape(b, t // qb, IDX_HEADS, qb, IDX_DIM)
        iwt = iw.reshape(b, t, IDX_HEADS).transpose(0, 2, 1)
        kb = kb.reshape(b, t, DSA_KV)
        vt = vt.reshape(b, t // KEY_TILE, nh, HEAD_DIM, KEY_TILE)
        ikb = ikb.reshape(b, t, IDX_DIM)
    else:
        q, iq, iw, k2, v2, ik = _dsa_in(xn, w, for_attn=False)
        tq = -(-t // qb) * qb
        pad_q = lambda z: jnp.pad(z.reshape(b, t, z.shape[-1]), ((0, 0), (0, tq - t), (0, 0)))
        by_head = lambda z, n_h: pad_q(z).reshape(b, tq // qb, qb, n_h, z.shape[-1] // n_h).transpose(0, 1, 3, 2, 4)
        q, iq, iwt = by_head(q, N_HEADS), by_head(iq, IDX_HEADS), pad_q(iw).transpose(0, 2, 1)
        kb, vt, ikb = _dsa_keys(k_past, v_past, ik_past, k2.reshape(b, nh * t, HEAD_DIM),
                                v2.reshape(b, nh * t, HEAD_DIM), ik.reshape(b, t, IDX_DIM))
    o = _dsa_attn(w["rel_bias"], q, iq, iwt, kb, vt, ikb, past=past, n_keys=n_keys)
    o = o[:, :t].reshape(b * t, D_MODEL)
    return (o, k2.reshape(b, t, nh, HEAD_DIM), v2.reshape(b, t, nh, HEAD_DIM), ik.reshape(b, t, IDX_DIM))


GLA_QK = GLA_HEADS * GLA_DK
GLA_COLS = (0, GLA_QK, 2 * GLA_QK, 2 * GLA_QK + D_MODEL, 2 * GLA_QK + 2 * D_MODEL, 2 * GLA_QK + 2 * D_MODEL + GLA_RANK)


def _gla_in_kernel(xn_ref, w_ref, wa2_ref, ba_ref, q_ref, k_ref, v_ref, r_ref, g_ref):
    x = xn_ref[...]
    proj = lambda i: _dot(x, w_ref[:, GLA_COLS[i]:GLA_COLS[i + 1]])
    q_ref[...] = proj(0) * (GLA_DK ** -0.5)
    k_ref[...] = proj(1)
    v_ref[...] = proj(2).astype(BF16)
    r_ref[...] = proj(3)
    z = _dot(proj(4).astype(BF16), wa2_ref[...]) + ba_ref[...]
    g_ref[...] = (jnp.minimum(z, 0.0) - jnp.log1p(jnp.exp(-jnp.abs(z)))) * (1.0 / GLA_GATE_NORM)


def _gla_in(xn, w):
    n = xn.shape[0]
    tm = _row_tile(n, 512)
    qk = GLA_HEADS * GLA_DK
    row = lambda wd: pl.BlockSpec((tm, wd), lambda i: (i, 0))
    return pl.pallas_call(
        _gla_in_kernel,
        grid=(n // tm,),
        in_specs=[row(D_MODEL), _const_spec((D_MODEL, GLA_COLS[-1])), _const_spec((GLA_RANK, qk)),
                  _const_spec((1, qk))],
        out_specs=[row(qk), row(qk), row(D_MODEL), row(D_MODEL), row(qk)],
        out_shape=[jax.ShapeDtypeStruct((n, qk), F32), jax.ShapeDtypeStruct((n, qk), F32),
                   jax.ShapeDtypeStruct((n, D_MODEL), BF16), jax.ShapeDtypeStruct((n, D_MODEL), F32),
                   jax.ShapeDtypeStruct((n, qk), F32)],
        compiler_params=_params("parallel"),
        name="gla_in",
    )(xn, w["w_in"], w["wa2"], w["ba"])


def _gla_chunk(q, k, g, v, st, tri_c, tri_sb):
    c = q.shape[0]
    sb = GLA_SUB
    cum = jnp.dot(tri_c, g, precision=HIGHEST, preferred_element_type=F32)
    total = cum[c - 1:c, :]
    o_inter = _dot_nt((q * jnp.exp(cum)).astype(BF16), st.astype(BF16))
    st_new = st * jnp.exp(total) + _dot_tn(v, (k * jnp.exp(total - cum)).astype(BF16))
    o_rows = []
    cum2 = cum * LOG2_E
    for i in range(c // sb):
        r = slice(i * sb, (i + 1) * sb)
        ci = cum2[r]
        decay = jnp.exp2(ci[:, None, :] - ci[None, :, :])
        sc = jnp.sum(decay * q[r][:, None, :] * k[r][None, :, :], axis=-1)
        sc = jnp.where(tri_sb, sc, 0.0)
        o_rows.append(o_inter[r] + _dot(sc.astype(BF16), v[r]))
    width = sb
    while width < c:
        for p in range(c // (2 * width)):
            lo = 2 * p * width
            left = slice(lo, lo + width)
            right = slice(lo + width, lo + 2 * width)
            edge = cum[lo + width - 1:lo + width, :]
            q_hat = (q[right] * jnp.exp(cum[right] - edge)).astype(BF16)
            k_hat = (k[left] * jnp.exp(edge - cum[left])).astype(BF16)
            o_pair = _dot(_dot_nt(q_hat, k_hat).astype(BF16), v[left])
            for j in range(width // sb):
                o_rows[(lo + width) // sb + j] += o_pair[j * sb:(j + 1) * sb]
        width *= 2
    return jnp.concatenate(o_rows, axis=0), st_new


def _gla_core_kernel(q_ref, k_ref, g_ref, v_ref, r_ref, s0_ref, gain_ref, a_ref, sout_ref, st_ref, *, tt, c):
    sb = GLA_SUB
    t = pl.program_id(1)

    @pl.when(t == 0)
    def _():
        for h in range(GLA_HEADS):
            st_ref[h] = s0_ref[0, h].T

    tri_c = (lax.broadcasted_iota(I32, (c, c), 0) >= lax.broadcasted_iota(I32, (c, c), 1)).astype(F32)
    tri_sb = lax.broadcasted_iota(I32, (sb, sb), 0) >= lax.broadcasted_iota(I32, (sb, sb), 1)
    gain = gain_ref[...]

    def chunk(ci, carry):
        rows = pl.ds(pl.multiple_of(ci * c, c), c)
        for h in range(GLA_HEADS):
            dk = slice(h * GLA_DK, (h + 1) * GLA_DK)
            dv = slice(h * GLA_DV, (h + 1) * GLA_DV)
            o, st_ref[h] = _gla_chunk(q_ref[0, rows, dk], k_ref[0, rows, dk], g_ref[0, rows, dk],
                                      v_ref[0, rows, dv], st_ref[h], tri_c, tri_sb)
            r = r_ref[0, rows, dv]
            a_ref[0, rows, dv] = (_rms(o, gain) * (r * jax.nn.sigmoid(r))).astype(BF16)
        return carry

    lax.fori_loop(0, tt // c, chunk, 0)

    @pl.when(t == pl.num_programs(1) - 1)
    def _():
        for h in range(GLA_HEADS):
            sout_ref[0, h] = st_ref[h].T


def _gla_core(q, k, g, v, r, s0, gain):
    b, t, _ = q.shape
    tt = _row_tile(t, 256)
    c = min(tt, CHUNK)
    assert tt % c == 0 and c % GLA_SUB == 0 and (c // GLA_SUB) & (c // GLA_SUB - 1) == 0
    qk = GLA_HEADS * GLA_DK
    tok = lambda wd: pl.BlockSpec((1, tt, wd), lambda i, j: (i, j, 0))
    st_spec = pl.BlockSpec((1, GLA_HEADS, GLA_DK, GLA_DV), lambda i, j: (i, 0, 0, 0))
    return pl.pallas_call(
        functools.partial(_gla_core_kernel, tt=tt, c=c),
        grid=(b, t // tt),
        in_specs=[tok(qk), tok(qk), tok(qk), tok(D_MODEL), tok(D_MODEL), st_spec, _const_spec((1, GLA_DV))],
        out_specs=[tok(D_MODEL), st_spec],
        out_shape=[jax.ShapeDtypeStruct((b, t, D_MODEL), BF16),
                   jax.ShapeDtypeStruct((b, GLA_HEADS, GLA_DK, GLA_DV), F32)],
        scratch_shapes=[pltpu.VMEM((GLA_HEADS, GLA_DV, GLA_DK), F32)],
        compiler_params=_params("arbitrary", "arbitrary"),
        name="gla_core",
    )(q, k, g, v, r, s0, gain)


def _conv_kernel(xn_ref, w_ref, prev_ref, cw_ref, cb_ref, a_ref, new_ref, ue_ref, *, tt):
    t = pl.program_id(1)
    pad = SUBLANES
    lo = pad - (CONV_W - 1)

    @pl.when(t == 0)
    def _():
        ue_ref[lo:pad, :] = prev_ref[0]

    x = xn_ref[0]
    proj = lambda i: _dot(x, w_ref[:, i * D_MODEL:(i + 1) * D_MODEL])
    u = proj(1) * proj(2)
    ue_ref[pad:pad + tt, :] = u
    conv = cb_ref[...] + cw_ref[CONV_W - 1:CONV_W, :] * u
    for j in range(CONV_W - 1):
        conv = conv + cw_ref[j:j + 1, :] * ue_ref[lo + j:lo + j + tt, :]
    a_ref[0] = (proj(0) * conv).astype(BF16)
    tail = ue_ref[lo + tt:pad + tt, :]
    ue_ref[lo:pad, :] = tail

    @pl.when(t == pl.num_programs(1) - 1)
    def _():
        new_ref[0] = tail


def _conv_front(xn, prev, w):
    b, t, _ = xn.shape
    tt = _row_tile(t, 512)
    tok = pl.BlockSpec((1, tt, D_MODEL), lambda i, j: (i, j, 0))
    st = pl.BlockSpec((1, CONV_W - 1, D_MODEL), lambda i, j: (i, 0, 0))
    w_spec = _const_spec((D_MODEL, 3 * D_MODEL))
    return pl.pallas_call(
        functools.partial(_conv_kernel, tt=tt),
        grid=(b, t // tt),
        in_specs=[tok, w_spec, st, _const_spec((CONV_W, D_MODEL)), _const_spec((1, D_MODEL))],
        out_specs=[tok, st],
        out_shape=[jax.ShapeDtypeStruct((b, t, D_MODEL), BF16),
                   jax.ShapeDtypeStruct((b, CONV_W - 1, D_MODEL), F32)],
        scratch_shapes=[pltpu.VMEM((SUBLANES + tt, D_MODEL), F32)],
        compiler_params=_params("arbitrary", "arbitrary"),
        name="conv_front",
    )(xn, w["w_in"], prev, w["cw"], w["cb"])


def _trunk(x, p, pool_prev, k_past, v_past, ik_past, gla_prev, conv_prev, w):
    b, t, _ = x.shape
    n = b * t
    past = k_past.shape[1]
    flat = lambda z: z.reshape(n, z.shape[-1])
    p = p.reshape(p.shape[0], n, PLE_DIM)
    post = lambda h, a, layer: _post_block(h, a, p, w["post"], layer, *w["mix_out"][layer])

    a, pool_new = _pool_front(x, pool_prev, w["gmix0"], past)
    h, xn = post(flat(x), flat(a), 0)

    a, k_new, v_new, ik_new = _dsa_mixer(xn, k_past, v_past, ik_past, w["dsa"], b, t)
    h, xn = post(h, a, 1)

    q, k, v, r, g = _gla_in(xn, w["gla"])
    seq = lambda z: z.reshape(b, t, z.shape[-1])
    a, gla_new = _gla_core(seq(q), seq(k), seq(g), seq(v), seq(r), gla_prev, w["gla"]["gain"])
    h, xn = post(h, flat(a), 2)

    a, conv_new = _conv_front(seq(xn), conv_prev, w["conv"])
    h, _ = post(h, flat(a), 3)
    return h.reshape(b, t, D_MODEL), pool_new, k_new, v_new, ik_new, gla_new, conv_new


def _prepare_weights(norm_mix, norm_mlp, norm_ple, w_mlp1, w_mlp2, w_ple_proj, w_ple_gate, w_pool, b_pool,
                     pool_scale, w_dsa_in, w_dsa_out, q_norm, k_norm, rel_bias, w_gla_in, w_gla_a2, b_gla_a,
                     gla_norm, w_gla_out, w_conv_in, conv_w, conv_b, w_conv_out):
    bf = lambda z: z.astype(BF16)
    vec = lambda z: z.reshape(1, -1).astype(F32)
    stack_vec = lambda z: z.reshape(z.shape[0], 1, z.shape[1]).astype(F32)
    w_pool_bd = jax.scipy.linalg.block_diag(*[w_pool[g] for g in range(w_pool.shape[0])])
    zeros = jnp.zeros((1, D_MODEL), F32)
    ones = jnp.ones((1, D_MODEL), F32)
    mix_out = [(bf(w_pool_bd), vec(b_pool), vec(pool_scale)), (bf(w_dsa_out), zeros, ones),
               (bf(w_gla_out), zeros, ones), (bf(w_conv_out), zeros, ones)]
    post = dict(gmix=stack_vec(norm_mix), gmlp=stack_vec(norm_mlp), gple=stack_vec(norm_ple), w1=bf(w_mlp1),
                w2=bf(w_mlp2), wg=bf(w_ple_gate), wp=bf(w_ple_proj))

    dsa = dict(w_in=bf(w_dsa_in), qg=vec(q_norm), kg=vec(k_norm), rel_bias=rel_bias.astype(F32))
    gla = dict(w_in=bf(w_gla_in), wa2=bf(w_gla_a2), ba=vec(b_gla_a), gain=vec(gla_norm))
    conv = dict(w_in=bf(w_conv_in), cw=conv_w.astype(F32), cb=vec(conv_b))
    return dict(post=post, mix_out=mix_out, gmix0=vec(norm_mix[0]), dsa=dsa, gla=gla, conv=conv)


def kernel(x_prompt, x_sample, p_prompt, p_sample, state_pool, cache_k, cache_v, cache_idx_k, state_gla, state_conv, norm_mix, norm_mlp, norm_ple, w_mlp1, w_mlp2, w_ple_proj, w_ple_gate, w_pool, b_pool, pool_scale, w_dsa_in, w_dsa_out, q_norm, k_norm, rel_bias, w_gla_in, w_gla_a2, b_gla_a, gla_norm, w_gla_out, w_conv_in, conv_w, conv_b, w_conv_out):
    w = _prepare_weights(norm_mix, norm_mlp, norm_ple, w_mlp1, w_mlp2, w_ple_proj, w_ple_gate, w_pool, b_pool,
                         pool_scale, w_dsa_in, w_dsa_out, q_norm, k_norm, rel_bias, w_gla_in, w_gla_a2, b_gla_a,
                         gla_norm, w_gla_out, w_conv_in, conv_w, conv_b, w_conv_out)
    bp = x_prompt.shape[0]
    dt = x_prompt.dtype
    y_p, pool_p, k_p, v_p, ik_p, gla_p, conv_p = _trunk(
        x_prompt, p_prompt,
        jnp.zeros((bp, POOL_CTX, D_MODEL), dt),
        jnp.zeros((bp, 0, N_KV_HEADS, HEAD_DIM), dt),
        jnp.zeros((bp, 0, N_KV_HEADS, HEAD_DIM), dt),
        jnp.zeros((bp, 0, IDX_DIM), dt),
        jnp.zeros((bp, GLA_HEADS, GLA_DK, GLA_DV), dt),
        jnp.zeros((bp, CONV_W - 1, D_MODEL), dt),
        w)
    y_s, pool_s, k_s, v_s, ik_s, gla_s, conv_s = _trunk(
        x_sample, p_sample, state_pool, cache_k, cache_v, cache_idx_k, state_gla, state_conv, w)
    return (y_p, y_s, pool_p, pool_s, k_p, v_p, ik_p, k_s, v_s, ik_s, gla_p, gla_s, conv_p, conv_s)
```

```python
import functools

import jax
import jax.numpy as jnp
from jax import lax
from jax.experimental import pallas as pl
from jax.experimental.pallas import tpu as pltpu

F32 = jnp.float32
BF16 = jnp.bfloat16
I32 = jnp.int32

D_MODEL = 1024
D_FF = 4 * D_MODEL
PLE_DIM = 256
EPS = 1e-6
CHUNK = 64
CHUNK_SHIFT = 6
POOL_WINDOWS = (2, 4, 8, 16)
POOL_GC = D_MODEL // len(POOL_WINDOWS)
POOL_CTX = max(POOL_WINDOWS) - 1
N_HEADS = 8
N_KV_HEADS = 2
HEAD_DIM = D_MODEL // N_HEADS
GROUP = N_HEADS // N_KV_HEADS
IDX_HEADS = 8
IDX_DIM = 64
TOPK_MAX = 256
REL_BUCKETS = 32
REL_LOG_EDGES = (12, 16, 23, 32, 46, 64, 91)
REL_FAR_BUCKET = REL_BUCKETS // 2 - 1
GLA_HEADS = 4
GLA_DK = D_MODEL // 2 // GLA_HEADS
GLA_DV = D_MODEL // GLA_HEADS
GLA_RANK = 16
GLA_GATE_NORM = 16.0
GLA_SUB = 16
CONV_W = 3

V7X_VMEM_LIMIT_BYTES = 56 * 1024 * 1024
SUBLANES = 8
INT_MIN = -(2 ** 31)
QUERY_BLOCK = 128
KEY_TILE = 256
NEAR_SLOTS = KEY_TILE // QUERY_BLOCK + 2

HIGHEST = lax.Precision.HIGHEST
LOG2_E = 1.4426950408889634


def _params(*sem):
    return pltpu.CompilerParams(dimension_semantics=sem, vmem_limit_bytes=V7X_VMEM_LIMIT_BYTES)


def _const_spec(shape):
    nd = len(shape)
    return pl.BlockSpec(shape, lambda *_: (0,) * nd, pipeline_mode=pl.Buffered(1))


def _rms(x, g):
    return x * lax.rsqrt(jnp.mean(x * x, axis=-1, keepdims=True) + EPS) * g


def _dot(a, b):
    return jnp.dot(a, b, preferred_element_type=F32)


def _dot_nt(a, b):
    return lax.dot_general(a, b, (((1,), (1,)), ((), ())), preferred_element_type=F32)


def _dot_tn(a, b):
    return lax.dot_general(a, b, (((0,), (0,)), ((), ())), preferred_element_type=F32)


def _fold_rows(x, op):
    n = x.shape[0] // 8
    assert n & (n - 1) == 0
    parts = x.reshape(n, 8, x.shape[1])
    while n > 1:
        n //= 2
        parts = op(parts[:n], parts[n:])
    return parts[0]


def _row_tile(n, want):
    t = min(n, want)
    assert n % t == 0
    return t


def _post_kernel(h_ref, a_ref, p_ref, wout_ref, bout_ref, sout_ref, gmlp_ref, w1_ref, w2_ref, gple_ref, wg_ref,
                 wp_ref, *rest, ff_chunk, with_next):
    if with_next:
        gnext_ref, h_out_ref, xn_out_ref = rest
    else:
        (h_out_ref,) = rest
    y = (_dot(a_ref[...], wout_ref[...]) + bout_ref[...]) * sout_ref[...]
    h1 = h_ref[...] + y
    hn = _rms(h1, gmlp_ref[...]).astype(BF16)
    acc = h1
    for c in range(D_FF // ff_chunk):
        cols = slice(c * ff_chunk, (c + 1) * ff_chunk)
        hid = jnp.square(jnp.maximum(_dot(hn, w1_ref[:, cols]), 0.0)).astype(BF16)
        acc = acc + _dot(hid, w2_ref[cols, :])
    gate = jax.nn.sigmoid(_dot(_rms(acc, gple_ref[...]).astype(BF16), wg_ref[...]))
    h3 = acc + _dot(p_ref[...].astype(BF16), wp_ref[...]) * gate
    h_out_ref[...] = h3
    if with_next:
        xn_out_ref[...] = _rms(h3, gnext_ref[...]).astype(BF16)


def _layer_spec(shape, layer):
    nd = len(shape)
    return pl.BlockSpec((None,) + tuple(shape), lambda *_: (layer,) + (0,) * nd, pipeline_mode=pl.Buffered(1))


def _post_block(h, a, p, pw, layer, wout, bout, sout):
    n = h.shape[0]
    tm = _row_tile(n, 512)
    depth = pw["w1"].shape[0]
    with_next = layer + 1 < depth
    row = lambda w: pl.BlockSpec((tm, w), lambda i: (i, 0))
    vec = _const_spec((1, D_MODEL))
    lvec = lambda l: _layer_spec((1, D_MODEL), l)
    out_shape = [jax.ShapeDtypeStruct((n, D_MODEL), F32)]
    out_specs = [row(D_MODEL)]
    in_specs = [row(D_MODEL), row(D_MODEL), pl.BlockSpec((None, tm, PLE_DIM), lambda i: (layer, i, 0)),
                _const_spec((D_MODEL, D_MODEL)), vec, vec, lvec(layer),
                _layer_spec((D_MODEL, D_FF), layer), _layer_spec((D_FF, D_MODEL), layer), lvec(layer),
                _layer_spec((D_MODEL, D_MODEL), layer), _layer_spec((PLE_DIM, D_MODEL), layer)]
    args = [h, a, p, wout, bout, sout, pw["gmlp"], pw["w1"], pw["w2"], pw["gple"], pw["wg"], pw["wp"]]
    if with_next:
        out_shape.append(jax.ShapeDtypeStruct((n, D_MODEL), BF16))
        out_specs.append(row(D_MODEL))
        in_specs.append(lvec(layer + 1))
        args.append(pw["gmix"])
    outs = pl.pallas_call(
        functools.partial(_post_kernel, ff_chunk=1024, with_next=with_next),
        grid=(n // tm,),
        in_specs=in_specs,
        out_specs=out_specs,
        out_shape=out_shape,
        compiler_params=_params("parallel"),
        name="post_block",
    )(*args)
    return (outs[0], outs[1]) if with_next else (outs[0], None)


def _pool_kernel(x_ref, prev_ref, g_ref, a_ref, pool_ref, xe_ref, *, tt, pos0):
    t = pl.program_id(1)
    ctx = POOL_CTX + 1

    @pl.when(t == 0)
    def _():
        xe_ref[0:1, :] = jnp.zeros((1, D_MODEL), F32)
        xe_ref[1:ctx, :] = prev_ref[0]

    xn = _rms(x_ref[0], g_ref[...])
    xe_ref[ctx:ctx + tt, :] = xn
    pos1 = (pos0 + 1 + t * tt + lax.broadcasted_iota(I32, (tt, 1), 0)).astype(F32)
    for g, w in enumerate(POOL_WINDOWS):
        cols = slice(g * POOL_GC, (g + 1) * POOL_GC)
        win = xn[:, cols]
        for j in range(1, w):
            win = win + xe_ref[ctx - j:ctx - j + tt, cols]
        mean = win / jnp.minimum(float(w), pos1)
        a_ref[0, :, cols] = (mean - xn[:, cols]).astype(BF16)
    tail = xe_ref[tt:tt + ctx, :]
    xe_ref[0:ctx, :] = tail

    @pl.when(t == pl.num_programs(1) - 1)
    def _():
        pool_ref[0] = xe_ref[tt + 1:tt + ctx, :]


def _pool_front(x, prev, g, pos0):
    b, t, _ = x.shape
    tt = _row_tile(t, 512)
    return pl.pallas_call(
        functools.partial(_pool_kernel, tt=tt, pos0=pos0),
        grid=(b, t // tt),
        in_specs=[pl.BlockSpec((1, tt, D_MODEL), lambda i, j: (i, j, 0)),
                  pl.BlockSpec((1, POOL_CTX, D_MODEL), lambda i, j: (i, 0, 0)),
                  _const_spec((1, D_MODEL))],
        out_specs=[pl.BlockSpec((1, tt, D_MODEL), lambda i, j: (i, j, 0)),
                   pl.BlockSpec((1, POOL_CTX, D_MODEL), lambda i, j: (i, 0, 0))],
        out_shape=[jax.ShapeDtypeStruct((b, t, D_MODEL), BF16),
                   jax.ShapeDtypeStruct((b, POOL_CTX, D_MODEL), F32)],
        scratch_shapes=[pltpu.VMEM((POOL_CTX + 1 + tt, D_MODEL), F32)],
        compiler_params=_params("arbitrary", "arbitrary"),
        name="pool_front",
    )(x, prev, g)


DSA_KV = N_KV_HEADS * HEAD_DIM
DSA_IQ = IDX_HEADS * IDX_DIM
DSA_COLS = (0, D_MODEL, D_MODEL + DSA_KV, D_MODEL + 2 * DSA_KV, D_MODEL + 2 * DSA_KV + DSA_IQ,
            D_MODEL + 2 * DSA_KV + DSA_IQ + IDX_DIM, D_MODEL + 2 * DSA_KV + DSA_IQ + IDX_DIM + IDX_HEADS)


def _dsa_in_kernel(xn_ref, w_ref, qg_ref, kg_ref, q_ref, iq_ref, iw_ref, k2_ref, v2_ref, ik_ref, *key_refs,
                   tm, for_attn):
    x = xn_ref[...]
    qb = QUERY_BLOCK
    if for_attn:
        kb_ref, vt_ref, ikb_ref = key_refs
    col = lambda i: slice(DSA_COLS[i], DSA_COLS[i + 1])
    q = _dot(x, w_ref[:, col(0)])
    iq = _dot(x, w_ref[:, col(3)]).astype(BF16)
    for h in range(N_HEADS):
        qh = _rms(q[:, h * HEAD_DIM:(h + 1) * HEAD_DIM], qg_ref[...]).astype(BF16)
        iqh = iq[:, h * IDX_DIM:(h + 1) * IDX_DIM]
        if for_attn:
            for j in range(tm // qb):
                q_ref[j, h] = qh[j * qb:(j + 1) * qb]
                iq_ref[j, h] = iqh[j * qb:(j + 1) * qb]
        else:
            q_ref[:, h * HEAD_DIM:(h + 1) * HEAD_DIM] = qh
            iq_ref[:, h * IDX_DIM:(h + 1) * IDX_DIM] = iqh
    k = _dot(x, w_ref[:, col(1)])
    v = _dot(x, w_ref[:, col(2)])
    tail = _dot(x, w_ref[:, DSA_COLS[4]:DSA_COLS[6]])
    ik = tail[:, :IDX_DIM]
    ik_ref[...] = ik
    iw_ref[...] = tail[:, IDX_DIM:]
    for h in range(N_KV_HEADS):
        cols = slice(h * HEAD_DIM, (h + 1) * HEAD_DIM)
        kh = _rms(k[:, cols], kg_ref[...])
        k2_ref[pl.ds(h, tm, stride=N_KV_HEADS), :] = kh
        v2_ref[pl.ds(h, tm, stride=N_KV_HEADS), :] = v[:, cols]
        if for_attn:
            kb_ref[:, cols] = kh.astype(BF16)
            for j in range(tm // KEY_TILE):
                vt_ref[j, h] = v[j * KEY_TILE:(j + 1) * KEY_TILE, cols].T.astype(BF16)
    if for_attn:
        ikb_ref[...] = ik.astype(BF16)


def _dsa_in(xn, w, for_attn):
    n = xn.shape[0]
    tm = _row_tile(n, 512)
    qb = QUERY_BLOCK
    row = lambda wd: pl.BlockSpec((tm, wd), lambda i: (i, 0))
    lead = lambda shape: pl.BlockSpec(shape, lambda i: (i,) + (0,) * (len(shape) - 1))
    if for_attn:
        assert tm % KEY_TILE == 0 and tm % qb == 0
        q_specs = [lead((tm // qb, N_HEADS, qb, HEAD_DIM)), lead((tm // qb, IDX_HEADS, qb, IDX_DIM))]
        q_shapes = [jax.ShapeDtypeStruct((n // qb, N_HEADS, qb, HEAD_DIM), BF16),
                    jax.ShapeDtypeStruct((n // qb, IDX_HEADS, qb, IDX_DIM), BF16)]
        key_specs = [row(DSA_KV), lead((tm // KEY_TILE, N_KV_HEADS, HEAD_DIM, KEY_TILE)), row(IDX_DIM)]
        key_shapes = [jax.ShapeDtypeStruct((n, DSA_KV), BF16),
                      jax.ShapeDtypeStruct((n // KEY_TILE, N_KV_HEADS, HEAD_DIM, KEY_TILE), BF16),
                      jax.ShapeDtypeStruct((n, IDX_DIM), BF16)]
    else:
        q_specs = [row(D_MODEL), row(DSA_IQ)]
        q_shapes = [jax.ShapeDtypeStruct((n, D_MODEL), BF16), jax.ShapeDtypeStruct((n, DSA_IQ), BF16)]
        key_specs, key_shapes = [], []
    kv2 = pl.BlockSpec((N_KV_HEADS * tm, HEAD_DIM), lambda i: (i, 0))
    return pl.pallas_call(
        functools.partial(_dsa_in_kernel, tm=tm, for_attn=for_attn),
        grid=(n // tm,),
        in_specs=[row(D_MODEL), _const_spec((D_MODEL, DSA_COLS[-1])), _const_spec((1, HEAD_DIM)),
                  _const_spec((1, HEAD_DIM))],
        out_specs=q_specs + [row(IDX_HEADS), kv2, kv2, row(IDX_DIM)] + key_specs,
        out_shape=q_shapes + [jax.ShapeDtypeStruct((n, IDX_HEADS), F32),
                              jax.ShapeDtypeStruct((N_KV_HEADS * n, HEAD_DIM), F32),
                              jax.ShapeDtypeStruct((N_KV_HEADS * n, HEAD_DIM), F32),
                              jax.ShapeDtypeStruct((n, IDX_DIM), F32)] + key_shapes,
        compiler_params=_params("parallel"),
        name="dsa_in",
    )(xn, w["w_in"], w["qg"], w["kg"])


def _dsa_keys_kernel(ck_ref, cv_ref, k2_ref, v2_ref, kb_ref, vt_ref, *, past, t, nb):
    lp = kb_ref.shape[1]
    nh = N_KV_HEADS
    for r in range(nb):
        for h in range(nh):
            cols = slice((h * nb + r) * HEAD_DIM, (h * nb + r + 1) * HEAD_DIM)
            keys = slice(r * KEY_TILE, (r + 1) * KEY_TILE)
            kb_ref[0, 0:past, cols] = ck_ref[r, pl.ds(h, past, stride=nh), :].astype(BF16)
            kb_ref[0, past:past + t, cols] = k2_ref[r, pl.ds(h, t, stride=nh), :].astype(BF16)
            kb_ref[0, past + t:lp, cols] = jnp.zeros((lp - past - t, HEAD_DIM), BF16)
            for j in range(past // KEY_TILE):
                v_old = cv_ref[r, pl.ds(h + nh * KEY_TILE * j, KEY_TILE, stride=nh), :]
                vt_ref[0, j, h, :, keys] = v_old.T.astype(BF16)
            v_new = jnp.concatenate([v2_ref[r, pl.ds(h, t, stride=nh), :],
                                     jnp.zeros((KEY_TILE - t, HEAD_DIM), F32)], axis=0)
            vt_ref[0, past // KEY_TILE, h, :, keys] = v_new.T.astype(BF16)


def _dsa_keys(cache_k, cache_v, k2, v2, nb):
    b, past = cache_k.shape[:2]
    nh = N_KV_HEADS
    t = k2.shape[1] // nh
    lp = past + KEY_TILE
    assert past % KEY_TILE == 0 and t <= KEY_TILE and t % SUBLANES == 0 and b % nb == 0
    rows = lambda *shape: pl.BlockSpec((nb,) + shape, lambda i: (i,) + (0,) * len(shape))
    group = lambda *shape: pl.BlockSpec((1,) + shape, lambda i: (i,) + (0,) * len(shape))
    return pl.pallas_call(
        functools.partial(_dsa_keys_kernel, past=past, t=t, nb=nb),
        grid=(b // nb,),
        in_specs=[rows(nh * past, HEAD_DIM), rows(nh * past, HEAD_DIM), rows(nh * t, HEAD_DIM),
                  rows(nh * t, HEAD_DIM)],
        out_specs=[group(lp, nb * DSA_KV), group(lp // KEY_TILE, nh, HEAD_DIM, nb * KEY_TILE)],
        out_shape=[jax.ShapeDtypeStruct((b // nb, lp, nb * DSA_KV), BF16),
                   jax.ShapeDtypeStruct((b // nb, lp // KEY_TILE, nh, HEAD_DIM, nb * KEY_TILE), BF16)],
        compiler_params=_params("parallel"),
        name="dsa_keys",
    )(cache_k.reshape(b, nh * past, HEAD_DIM), cache_v.reshape(b, nh * past, HEAD_DIM), k2, v2)


def _dsa_attn_kernel(rb_ref, q_ref, iq_ref, iwt_ref, kb_ref, vt_ref, ik_ref, o_ref,
                     key_ref, msk_ref, lg_ref, bias_ref, j_ref, *, past, n_keys, topk, idx_bits, nb):
    kt_sz = KEY_TILE
    qb = QUERY_BLOCK
    qpb = qb // nb
    b = pl.program_id(0)
    i = pl.program_id(1)
    q0 = past + i * qpb
    kend = jnp.minimum(n_keys, (((q0 + qpb - 1) >> CHUNK_SHIFT) + 1) * CHUNK)
    nkt = (kend + kt_sz - 1) // kt_sz
    row_i = lax.broadcasted_iota(I32, (kt_sz, qb), 0)
    lane_i = lax.broadcasted_iota(I32, (kt_sz, qb), 1)
    col_i = lane_i % qpb

    @pl.when((b == 0) & (i == 0))
    def _():
        for w in range(1, NEAR_SLOTS):
            rel = row_i - col_i + (w - (NEAR_SLOTS - 1)) * qb
            n = jnp.abs(rel)
            log_bucket = jnp.full_like(n, REL_BUCKETS // 4)
            for edge in REL_LOG_EDGES:
                log_bucket = log_bucket + (n >= edge).astype(I32)
            bucket = jnp.where(n < REL_BUCKETS // 4, n, log_bucket) + jnp.where(rel > 0, REL_BUCKETS // 2, 0)
            for h in range(N_HEADS):
                val = jnp.zeros((kt_sz, qb), F32)
                for bk in range(REL_BUCKETS):
                    val = jnp.where(bucket == bk, rb_ref[bk, h], val)
                bias_ref[h, w] = val * LOG2_E
        for h in range(N_HEADS):
            bias_ref[h, 0] = jnp.full((kt_sz, qb), rb_ref[REL_FAR_BUCKET, h], F32) * LOG2_E

    iw = iwt_ref[0]
    iq_all = iq_ref[0, 0].reshape(IDX_HEADS * qb, nb * IDX_DIM)
    q_chunk = (q0 + col_i) >> CHUNK_SHIFT

    def tile_rows(jt):
        return pl.ds(pl.multiple_of(jt * kt_sz, kt_sz), kt_sz)

    def score_tile(jt, carry):
        ikt = ik_ref[0, tile_rows(jt), :]
        s = jnp.zeros((kt_sz, qb), F32)
        for hp in range(IDX_HEADS // 2):
            s2 = _dot_nt(ikt, iq_all[2 * hp * qb:(2 * hp + 2) * qb, :])
            for h in (2 * hp, 2 * hp + 1):
                s = s + iw[h:h + 1, :] * jnp.maximum(s2[:, (h - 2 * hp) * qb:(h - 2 * hp + 1) * qb], 0.0)
        s = s * (IDX_DIM ** -0.5)
        bits = pltpu.bitcast(s, I32)
        key = bits ^ ((bits >> 31) & 0x7FFFFFFF)
        kpos = jt * kt_sz + row_i
        adm = ((kpos >> CHUNK_SHIFT) <= q_chunk) & (kpos < n_keys)
        key_ref[tile_rows(jt), :] = jnp.where(adm, key, INT_MIN)
        return carry

    lax.fori_loop(0, nkt, score_tile, 0)

    def count(pred):
        def body(jt, acc):
            hit = jnp.where(pred(key_ref[tile_rows(jt), :], jt), 1.0, 0.0)
            return acc + _fold_rows(hit, jnp.add)

        acc = lax.fori_loop(0, nkt, body, jnp.zeros((8, qb), F32))
        return jnp.sum(acc, axis=0, keepdims=True)

    kf = float(topk)
    n_pos = count(lambda kk, jt: kk >= 0)
    n_adm = count(lambda kk, jt: kk > INT_MIN)
    thr0 = jnp.where(n_pos >= kf, 0, INT_MIN).astype(I32)
    n_ge0 = jnp.where(n_pos >= kf, n_pos, n_adm)

    def bisect(p, state):
        thr, n_ge = state
        cand = thr | jnp.left_shift(jnp.int32(1), 30 - p)
        n_cand = count(lambda kk, jt: kk >= cand)
        take = n_cand >= kf
        return jnp.where(take, cand, thr), jnp.where(take, n_cand, n_ge)

    thr, n_ge = lax.fori_loop(0, 31, bisect, (thr0, n_ge0))
    has_thr = thr > INT_MIN
    j_ref[...] = jnp.where(has_thr, n_keys, -1).astype(I32)
    tied_cut = jnp.max(jnp.where(has_thr & (n_ge > kf), 1, 0)) > 0

    @pl.when(tied_cut)
    def _():
        need = kf - count(lambda kk, jt: kk > thr)
        lo = jnp.zeros((1, qb), I32)
        for bit in range(idx_bits - 1, -1, -1):
            cand = lo + (1 << bit)
            below = count(lambda kk, jt: (kk == thr) & ((jt * kt_sz + row_i) < cand))
            lo = jnp.where(below < need, cand, lo)
        j_ref[...] = jnp.where(has_thr, lo, -1)

    j_last = j_ref[...]

    def mask_tile(jt, carry):
        kk = key_ref[tile_rows(jt), :]
        sel = (kk > thr) | ((kk == thr) & ((jt * kt_sz + row_i) <= j_last))
        msk_ref[tile_rows(jt), :] = jnp.where(sel, 0.0, -jnp.inf)
        return carry

    lax.fori_loop(0, nkt, mask_tile, 0)

    gq = GROUP * qb
    kdim = nb * HEAD_DIM
    slot0 = q0 // qb - (NEAR_SLOTS - 1)

    def bias_slot(jt):
        w = jt * (kt_sz // qb) - slot0
        return jnp.where(w >= 1, w, 0)

    kv_heads = range(N_KV_HEADS)
    q_groups = [q_ref[0, 0, kvh * GROUP:(kvh + 1) * GROUP].reshape(gq, kdim) for kvh in kv_heads]

    def logits_tile(jt, m8):
        msk = msk_ref[tile_rows(jt), :]
        w = bias_slot(jt)
        new_m8 = []
        for kvh in kv_heads:
            kt = kb_ref[0, tile_rows(jt), kvh * kdim:(kvh + 1) * kdim]
            lg = _dot_nt(kt, q_groups[kvh]) * (HEAD_DIM ** -0.5 * LOG2_E)
            parts = []
            for g in range(GROUP):
                h = kvh * GROUP + g
                lgh = lg[:, g * qb:(g + 1) * qb] + bias_ref[h, w] + msk
                lg_ref[tile_rows(jt), h * qb:(h + 1) * qb] = lgh
                parts.append(_fold_rows(lgh, jnp.maximum))
            new_m8.append(jnp.maximum(m8[kvh], jnp.concatenate(parts, axis=1)))
        return tuple(new_m8)

    m8 = lax.fori_loop(0, nkt, logits_tile, tuple(jnp.full((8, gq), -jnp.inf, F32) for _ in kv_heads))
    m = [jnp.max(m8[kvh], axis=0, keepdims=True) for kvh in kv_heads]

    lane_row_g = lax.broadcasted_iota(I32, (HEAD_DIM, gq), 1) % qb // qpb

    def pv_tile(jt, carry):
        new = []
        for kvh in kv_heads:
            acc, l8 = carry[kvh]
            p = jnp.exp2(lg_ref[tile_rows(jt), kvh * gq:(kvh + 1) * gq] - m[kvh])
            pb = p.astype(BF16)
            pv = _dot(vt_ref[0, jt, kvh, :, 0:kt_sz], pb)
            for r in range(1, nb):
                pv = jnp.where(lane_row_g == r, _dot(vt_ref[0, jt, kvh, :, r * kt_sz:(r + 1) * kt_sz], pb), pv)
            new.append((acc + pv, l8 + _fold_rows(p, jnp.add)))
        return tuple(new)

    zero = (jnp.zeros((HEAD_DIM, gq), F32), jnp.zeros((8, gq), F32))
    acc_l8 = lax.fori_loop(0, nkt, pv_tile, tuple(zero for _ in kv_heads))
    for kvh in kv_heads:
        acc, l8 = acc_l8[kvh]
        o_t = acc / jnp.sum(l8, axis=0, keepdims=True)
        for g in range(GROUP):
            h = kvh * GROUP + g
            o_ref[0, :, h * HEAD_DIM:(h + 1) * HEAD_DIM] = o_t[:, g * qb:(g + 1) * qb].T.astype(BF16)


def _dsa_attn(rel_bias, q, iq, iwt, kb, vt, ikb, *, past, n_keys):
    g, nblk = q.shape[:2]
    nb = q.shape[-1] // HEAD_DIM
    lp = kb.shape[1]
    qb = QUERY_BLOCK
    assert lp % KEY_TILE == 0 and past % KEY_TILE == 0 and KEY_TILE == (NEAR_SLOTS - 2) * qb
    assert qb % nb == 0 and (nb == 1 or nblk == 1)
    topk = min(TOPK_MAX, n_keys // 4)
    return pl.pallas_call(
        functools.partial(_dsa_attn_kernel, past=past, n_keys=n_keys, topk=topk,
                          idx_bits=max(1, (lp - 1).bit_length()), nb=nb),
        grid=(g, nblk),
        in_specs=[pl.BlockSpec(memory_space=pltpu.SMEM),
                  pl.BlockSpec((1, 1, N_HEADS, qb, nb * HEAD_DIM), lambda i, j: (i, j, 0, 0, 0)),
                  pl.BlockSpec((1, 1, IDX_HEADS, qb, nb * IDX_DIM), lambda i, j: (i, j, 0, 0, 0)),
                  pl.BlockSpec((1, IDX_HEADS, qb), lambda i, j: (i, 0, j)),
                  pl.BlockSpec((1, lp, nb * DSA_KV), lambda i, j: (i, 0, 0)),
                  pl.BlockSpec((1, lp // KEY_TILE, N_KV_HEADS, HEAD_DIM, nb * KEY_TILE),
                               lambda i, j: (i, 0, 0, 0, 0)),
                  pl.BlockSpec((1, lp, nb * IDX_DIM), lambda i, j: (i, 0, 0))],
        out_specs=pl.BlockSpec((1, qb, D_MODEL), lambda i, j: (i, j, 0)),
        out_shape=jax.ShapeDtypeStruct((g, nblk * qb, D_MODEL), BF16),
        scratch_shapes=[pltpu.VMEM((lp, qb), I32), pltpu.VMEM((lp, qb), F32), pltpu.VMEM((lp, N_HEADS * qb), F32),
                        pltpu.VMEM((N_HEADS, NEAR_SLOTS, KEY_TILE, qb), F32), pltpu.VMEM((1, qb), I32)],
        compiler_params=_params("arbitrary", "arbitrary"),
        name="dsa_attn",
    )(rel_bias, q, iq, iwt, kb, vt, ikb)


def _dsa_mixer(xn, k_past, v_past, ik_past, w, b, t):
    past = k_past.shape[1]
    n_keys = past + t
    qb = QUERY_BLOCK
    nh = N_KV_HEADS
    if past == 0:
        assert t % KEY_TILE == 0
        q, iq, iw, k2, v2, ik, kb, vt, ikb = _dsa_in(xn, w, for_attn=True)
        q = q.reshape(b, t // qb, N_HEADS, qb, HEAD_DIM)
        iq = iq.reshape(b, t // qb, IDX_HEADS, qb, IDX_DIM)
        iwt = iw.reshape(b, t, IDX_HEADS).transpose(0, 2, 1)
        kb = kb.reshape(b, t, DSA_KV)
        vt = vt.reshape(b, t // KEY_TILE, nh, HEAD_DIM, KEY_TILE)
        ikb = ikb.reshape(b, t, IDX_DIM)
    else:
        assert qb % t == 0 and b % (qb // t) == 0
        nb = qb // t
        g = b // nb
        q, iq, iw, k2, v2, ik = _dsa_in(xn, w, for_attn=False)
        eye = jnp.eye(nb, dtype=BF16)

        def own_slice(z, n_h):
            d = z.shape[-1] // n_h
            z = z.reshape(g, nb, t, n_h, d).transpose(0, 3, 1, 2, 4)
            z = z[:, :, :, :, None, :] * eye[None, None, :, None, :, None]
            return z.reshape(g, 1, n_h, qb, nb * d)

        q, iq = own_slice(q, N_HEADS), own_slice(iq, IDX_HEADS)
        iwt = iw.reshape(g, qb, IDX_HEADS).transpose(0, 2, 1)
        kb, vt = _dsa_keys(k_past, v_past, k2.reshape(b, nh * t, HEAD_DIM), v2.reshape(b, nh * t, HEAD_DIM), nb)
        lp = kb.shape[1]
        ikb = jnp.concatenate([ik_past.astype(BF16), ik.reshape(b, t, IDX_DIM).astype(BF16),
                               jnp.zeros((b, lp - n_keys, IDX_DIM), BF16)], axis=1)
        ikb = ikb.reshape(g, nb, lp, IDX_DIM).transpose(0, 2, 1, 3).reshape(g, lp, nb * IDX_DIM)
    o = _dsa_attn(w["rel_bias"], q, iq, iwt, kb, vt, ikb, past=past, n_keys=n_keys)
    o = o.reshape(b * t, D_MODEL)
    return (o, k2.reshape(b, t, nh, HEAD_DIM), v2.reshape(b, t, nh, HEAD_DIM), ik.reshape(b, t, IDX_DIM))


GLA_QK = GLA_HEADS * GLA_DK
GLA_COLS = (0, GLA_QK, 2 * GLA_QK, 2 * GLA_QK + D_MODEL, 2 * GLA_QK + 2 * D_MODEL, 2 * GLA_QK + 2 * D_MODEL + GLA_RANK)


def _gla_in_kernel(xn_ref, w_ref, wa2_ref, ba_ref, q_ref, k_ref, v_ref, r_ref, g_ref):
    x = xn_ref[...]
    proj = lambda i: _dot(x, w_ref[:, GLA_COLS[i]:GLA_COLS[i + 1]])
    q_ref[...] = proj(0) * (GLA_DK ** -0.5)
    k_ref[...] = proj(1)
    v_ref[...] = proj(2).astype(BF16)
    r_ref[...] = proj(3)
    z = _dot(proj(4).astype(BF16), wa2_ref[...]) + ba_ref[...]
    g_ref[...] = (jnp.minimum(z, 0.0) - jnp.log1p(jnp.exp(-jnp.abs(z)))) * (1.0 / GLA_GATE_NORM)


def _gla_in(xn, w):
    n = xn.shape[0]
    tm = _row_tile(n, 512)
    qk = GLA_HEADS * GLA_DK
    row = lambda wd: pl.BlockSpec((tm, wd), lambda i: (i, 0))
    return pl.pallas_call(
        _gla_in_kernel,
        grid=(n // tm,),
        in_specs=[row(D_MODEL), _const_spec((D_MODEL, GLA_COLS[-1])), _const_spec((GLA_RANK, qk)),
                  _const_spec((1, qk))],
        out_specs=[row(qk), row(qk), row(D_MODEL), row(D_MODEL), row(qk)],
        out_shape=[jax.ShapeDtypeStruct((n, qk), F32), jax.ShapeDtypeStruct((n, qk), F32),
                   jax.ShapeDtypeStruct((n, D_MODEL), BF16), jax.ShapeDtypeStruct((n, D_MODEL), F32),
                   jax.ShapeDtypeStruct((n, qk), F32)],
        compiler_params=_params("parallel"),
        name="gla_in",
    )(xn, w["w_in"], w["wa2"], w["ba"])


def _gla_chunk(q, k, g, v, st, tri_c, tri_sb):
    c = q.shape[0]
    sb = GLA_SUB
    cum = jnp.dot(tri_c, g, precision=HIGHEST, preferred_element_type=F32)
    total = cum[c - 1:c, :]
    o_inter = _dot_nt((q * jnp.exp(cum)).astype(BF16), st.astype(BF16))
    st_new = st * jnp.exp(total) + _dot_tn(v, (k * jnp.exp(total - cum)).astype(BF16))
    o_rows = []
    cum2 = cum * LOG2_E
    for i in range(c // sb):
        r = slice(i * sb, (i + 1) * sb)
        ci = cum2[r]
        decay = jnp.exp2(ci[:, None, :] - ci[None, :, :])
        sc = jnp.sum(decay * q[r][:, None, :] * k[r][None, :, :], axis=-1)
        sc = jnp.where(tri_sb, sc, 0.0)
        o_rows.append(o_inter[r] + _dot(sc.astype(BF16), v[r]))
    width = sb
    while width < c:
        for p in range(c // (2 * width)):
            lo = 2 * p * width
            left = slice(lo, lo + width)
            right = slice(lo + width, lo + 2 * width)
            edge = cum[lo + width - 1:lo + width, :]
            q_hat = (q[right] * jnp.exp(cum[right] - edge)).astype(BF16)
            k_hat = (k[left] * jnp.exp(edge - cum[left])).astype(BF16)
            o_pair = _dot(_dot_nt(q_hat, k_hat).astype(BF16), v[left])
            for j in range(width // sb):
                o_rows[(lo + width) // sb + j] += o_pair[j * sb:(j + 1) * sb]
        width *= 2
    return jnp.concatenate(o_rows, axis=0), st_new


def _gla_core_kernel(q_ref, k_ref, g_ref, v_ref, r_ref, s0_ref, gain_ref, a_ref, sout_ref, st_ref, *, tt, c):
    sb = GLA_SUB
    t = pl.program_id(1)

    @pl.when(t == 0)
    def _():
        for h in range(GLA_HEADS):
            st_ref[h] = s0_ref[0, h].T

    tri_c = (lax.broadcasted_iota(I32, (c, c), 0) >= lax.broadcasted_iota(I32, (c, c), 1)).astype(F32)
    tri_sb = lax.broadcasted_iota(I32, (sb, sb), 0) >= lax.broadcasted_iota(I32, (sb, sb), 1)
    gain = gain_ref[...]

    def chunk(ci, carry):
        rows = pl.ds(pl.multiple_of(ci * c, c), c)
        for h in range(GLA_HEADS):
            dk = slice(h * GLA_DK, (h + 1) * GLA_DK)
            dv = slice(h * GLA_DV, (h + 1) * GLA_DV)
            o, st_ref[h] = _gla_chunk(q_ref[0, rows, dk], k_ref[0, rows, dk], g_ref[0, rows, dk],
                                      v_ref[0, rows, dv], st_ref[h], tri_c, tri_sb)
            r = r_ref[0, rows, dv]
            a_ref[0, rows, dv] = (_rms(o, gain) * (r * jax.nn.sigmoid(r))).astype(BF16)
        return carry

    lax.fori_loop(0, tt // c, chunk, 0)

    @pl.when(t == pl.num_programs(1) - 1)
    def _():
        for h in range(GLA_HEADS):
            sout_ref[0, h] = st_ref[h].T


def _gla_core(q, k, g, v, r, s0, gain):
    b, t, _ = q.shape
    tt = _row_tile(t, 256)
    c = min(tt, CHUNK)
    assert tt % c == 0 and c % GLA_SUB == 0 and (c // GLA_SUB) & (c // GLA_SUB - 1) == 0
    qk = GLA_HEADS * GLA_DK
    tok = lambda wd: pl.BlockSpec((1, tt, wd), lambda i, j: (i, j, 0))
    st_spec = pl.BlockSpec((1, GLA_HEADS, GLA_DK, GLA_DV), lambda i, j: (i, 0, 0, 0))
    return pl.pallas_call(
        functools.partial(_gla_core_kernel, tt=tt, c=c),
        grid=(b, t // tt),
        in_specs=[tok(qk), tok(qk), tok(qk), tok(D_MODEL), tok(D_MODEL), st_spec, _const_spec((1, GLA_DV))],
        out_specs=[tok(D_MODEL), st_spec],
        out_shape=[jax.ShapeDtypeStruct((b, t, D_MODEL), BF16),
                   jax.ShapeDtypeStruct((b, GLA_HEADS, GLA_DK, GLA_DV), F32)],
        scratch_shapes=[pltpu.VMEM((GLA_HEADS, GLA_DV, GLA_DK), F32)],
        compiler_params=_params("arbitrary", "arbitrary"),
        name="gla_core",
    )(q, k, g, v, r, s0, gain)


def _conv_kernel(xn_ref, w_ref, prev_ref, cw_ref, cb_ref, a_ref, new_ref, ue_ref, *, tt):
    t = pl.program_id(1)
    pad = SUBLANES
    lo = pad - (CONV_W - 1)

    @pl.when(t == 0)
    def _():
        ue_ref[lo:pad, :] = prev_ref[0]

    x = xn_ref[0]
    proj = lambda i: _dot(x, w_ref[:, i * D_MODEL:(i + 1) * D_MODEL])
    u = proj(1) * proj(2)
    ue_ref[pad:pad + tt, :] = u
    conv = cb_ref[...] + cw_ref[CONV_W - 1:CONV_W, :] * u
    for j in range(CONV_W - 1):
        conv = conv + cw_ref[j:j + 1, :] * ue_ref[lo + j:lo + j + tt, :]
    a_ref[0] = (proj(0) * conv).astype(BF16)
    tail = ue_ref[lo + tt:pad + tt, :]
    ue_ref[lo:pad, :] = tail

    @pl.when(t == pl.num_programs(1) - 1)
    def _():
        new_ref[0] = tail


def _conv_front(xn, prev, w):
    b, t, _ = xn.shape
    tt = _row_tile(t, 512)
    tok = pl.BlockSpec((1, tt, D_MODEL), lambda i, j: (i, j, 0))
    st = pl.BlockSpec((1, CONV_W - 1, D_MODEL), lambda i, j: (i, 0, 0))
    w_spec = _const_spec((D_MODEL, 3 * D_MODEL))
    return pl.pallas_call(
        functools.partial(_conv_kernel, tt=tt),
        grid=(b, t // tt),
        in_specs=[tok, w_spec, st, _const_spec((CONV_W, D_MODEL)), _const_spec((1, D_MODEL))],
        out_specs=[tok, st],
        out_shape=[jax.ShapeDtypeStruct((b, t, D_MODEL), BF16),
                   jax.ShapeDtypeStruct((b, CONV_W - 1, D_MODEL), F32)],
        scratch_shapes=[pltpu.VMEM((SUBLANES + tt, D_MODEL), F32)],
        compiler_params=_params("arbitrary", "arbitrary"),
        name="conv_front",
    )(xn, w["w_in"], prev, w["cw"], w["cb"])


def _trunk(x, p, pool_prev, k_past, v_past, ik_past, gla_prev, conv_prev, w):
    b, t, _ = x.shape
    n = b * t
    past = k_past.shape[1]
    flat = lambda z: z.reshape(n, z.shape[-1])
    p = p.reshape(p.shape[0], n, PLE_DIM)
    post = lambda h, a, layer: _post_block(h, a, p, w["post"], layer, *w["mix_out"][layer])

    a, pool_new = _pool_front(x, pool_prev, w["gmix0"], past)
    h, xn = post(flat(x), flat(a), 0)

    a, k_new, v_new, ik_new = _dsa_mixer(xn, k_past, v_past, ik_past, w["dsa"], b, t)
    h, xn = post(h, a, 1)

    q, k, v, r, g = _gla_in(xn, w["gla"])
    seq = lambda z: z.reshape(b, t, z.shape[-1])
    a, gla_new = _gla_core(seq(q), seq(k), seq(g), seq(v), seq(r), gla_prev, w["gla"]["gain"])
    h, xn = post(h, flat(a), 2)

    a, conv_new = _conv_front(seq(xn), conv_prev, w["conv"])
    h, _ = post(h, flat(a), 3)
    return h.reshape(b, t, D_MODEL), pool_new, k_new, v_new, ik_new, gla_new, conv_new


def _prepare_weights(norm_mix, norm_mlp, norm_ple, w_mlp1, w_mlp2, w_ple_proj, w_ple_gate, w_pool, b_pool,
                     pool_scale, w_dsa_in, w_dsa_out, q_norm, k_norm, rel_bias, w_gla_in, w_gla_a2, b_gla_a,
                     gla_norm, w_gla_out, w_conv_in, conv_w, conv_b, w_conv_out):
    bf = lambda z: z.astype(BF16)
    vec = lambda z: z.reshape(1, -1).astype(F32)
    stack_vec = lambda z: z.reshape(z.shape[0], 1, z.shape[1]).astype(F32)
    w_pool_bd = jax.scipy.linalg.block_diag(*[w_pool[g] for g in range(w_pool.shape[0])])
    zeros = jnp.zeros((1, D_MODEL), F32)
    ones = jnp.ones((1, D_MODEL), F32)
    mix_out = [(bf(w_pool_bd), vec(b_pool), vec(pool_scale)), (bf(w_dsa_out), zeros, ones),
               (bf(w_gla_out), zeros, ones), (bf(w_conv_out), zeros, ones)]
    post = dict(gmix=stack_vec(norm_mix), gmlp=stack_vec(norm_mlp), gple=stack_vec(norm_ple), w1=bf(w_mlp1),
                w2=bf(w_mlp2), wg=bf(w_ple_gate), wp=bf(w_ple_proj))

    dsa = dict(w_in=bf(w_dsa_in), qg=vec(q_norm), kg=vec(k_norm), rel_bias=rel_bias.astype(F32))
    gla = dict(w_in=bf(w_gla_in), wa2=bf(w_gla_a2), ba=vec(b_gla_a), gain=vec(gla_norm))
    conv = dict(w_in=bf(w_conv_in), cw=conv_w.astype(F32), cb=vec(conv_b))
    return dict(post=post, mix_out=mix_out, gmix0=vec(norm_mix[0]), dsa=dsa, gla=gla, conv=conv)


def kernel(x_prompt, x_sample, p_prompt, p_sample, state_pool, cache_k, cache_v, cache_idx_k, state_gla, state_conv, norm_mix, norm_mlp, norm_ple, w_mlp1, w_mlp2, w_ple_proj, w_ple_gate, w_pool, b_pool, pool_scale, w_dsa_in, w_dsa_out, q_norm, k_norm, rel_bias, w_gla_in, w_gla_a2, b_gla_a, gla_norm, w_gla_out, w_conv_in, conv_w, conv_b, w_conv_out):
    w = _prepare_weights(norm_mix, norm_mlp, norm_ple, w_mlp1, w_mlp2, w_ple_proj, w_ple_gate, w_pool, b_pool,
                         pool_scale, w_dsa_in, w_dsa_out, q_norm, k_norm, rel_bias, w_gla_in, w_gla_a2, b_gla_a,
                         gla_norm, w_gla_out, w_conv_in, conv_w, conv_b, w_conv_out)
    bp = x_prompt.shape[0]
    dt = x_prompt.dtype
    y_p, pool_p, k_p, v_p, ik_p, gla_p, conv_p = _trunk(
        x_prompt, p_prompt,
        jnp.zeros((bp, POOL_CTX, D_MODEL), dt),
        jnp.zeros((bp, 0, N_KV_HEADS, HEAD_DIM), dt),
        jnp.zeros((bp, 0, N_KV_HEADS, HEAD_DIM), dt),
        jnp.zeros((bp, 0, IDX_DIM), dt),
        jnp.zeros((bp, GLA_HEADS, GLA_DK, GLA_DV), dt),
        jnp.zeros((bp, CONV_W - 1, D_MODEL), dt),
        w)
    y_s, pool_s, k_s, v_s, ik_s, gla_s, conv_s = _trunk(
        x_sample, p_sample, state_pool, cache_k, cache_v, cache_idx_k, state_gla, state_conv, w)
    return (y_p, y_s, pool_p, pool_s, k_p, v_p, ik_p, k_s, v_s, ik_s, gla_p, gla_s, conv_p, conv_s)
```

```python
import functools

import jax
import jax.numpy as jnp
from jax import lax
from jax.experimental import pallas as pl
from jax.experimental.pallas import tpu as pltpu

F32 = jnp.float32
BF16 = jnp.bfloat16
I32 = jnp.int32

D_MODEL = 1024
D_FF = 4 * D_MODEL
PLE_DIM = 256
EPS = 1e-6
CHUNK = 64
CHUNK_SHIFT = 6
POOL_WINDOWS = (2, 4, 8, 16)
POOL_GC = D_MODEL // len(POOL_WINDOWS)
POOL_CTX = max(POOL_WINDOWS) - 1
N_HEADS = 8
N_KV_HEADS = 2
HEAD_DIM = D_MODEL // N_HEADS
GROUP = N_HEADS // N_KV_HEADS
IDX_HEADS = 8
IDX_DIM = 64
TOPK_MAX = 256
REL_BUCKETS = 32
REL_LOG_EDGES = (12, 16, 23, 32, 46, 64, 91)
REL_FAR_BUCKET = REL_BUCKETS // 2 - 1
GLA_HEADS = 4
GLA_DK = D_MODEL // 2 // GLA_HEADS
GLA_DV = D_MODEL // GLA_HEADS
GLA_RANK = 16
GLA_GATE_NORM = 16.0
GLA_SUB = 16
CONV_W = 3

V7X_VMEM_LIMIT_BYTES = 56 * 1024 * 1024
SUBLANES = 8
INT_MIN = -(2 ** 31)
QUERY_BLOCK = 256
SHARED_QUERY_BLOCK = 128
KEY_TILE = 256

HIGHEST = lax.Precision.HIGHEST
LOG2_E = 1.4426950408889634


def _params(*sem):
    return pltpu.CompilerParams(dimension_semantics=sem, vmem_limit_bytes=V7X_VMEM_LIMIT_BYTES)


def _const_spec(shape):
    nd = len(shape)
    return pl.BlockSpec(shape, lambda *_: (0,) * nd, pipeline_mode=pl.Buffered(1))


def _rms(x, g):
    return x * lax.rsqrt(jnp.mean(x * x, axis=-1, keepdims=True) + EPS) * g


def _dot(a, b):
    return jnp.dot(a, b, preferred_element_type=F32)


def _dot_nt(a, b):
    return lax.dot_general(a, b, (((1,), (1,)), ((), ())), preferred_element_type=F32)


def _dot_tn(a, b):
    return lax.dot_general(a, b, (((0,), (0,)), ((), ())), preferred_element_type=F32)


def _fold_rows(x, op):
    n = x.shape[0] // 8
    assert n & (n - 1) == 0
    parts = x.reshape(n, 8, x.shape[1])
    while n > 1:
        n //= 2
        parts = op(parts[:n], parts[n:])
    return parts[0]


def _row_tile(n, want):
    t = min(n, want)
    assert n % t == 0
    return t


def _post_kernel(h_ref, a_ref, p_ref, wout_ref, bout_ref, sout_ref, gmlp_ref, w1_ref, w2_ref, gple_ref, wg_ref,
                 wp_ref, *rest, ff_chunk, with_next):
    if with_next:
        gnext_ref, h_out_ref, xn_out_ref = rest
    else:
        (h_out_ref,) = rest
    y = (_dot(a_ref[...], wout_ref[...]) + bout_ref[...]) * sout_ref[...]
    h1 = h_ref[...] + y
    hn = _rms(h1, gmlp_ref[...]).astype(BF16)
    acc = h1
    for c in range(D_FF // ff_chunk):
        cols = slice(c * ff_chunk, (c + 1) * ff_chunk)
        hid = jnp.square(jnp.maximum(_dot(hn, w1_ref[:, cols]), 0.0)).astype(BF16)
        acc = acc + _dot(hid, w2_ref[cols, :])
    gate = jax.nn.sigmoid(_dot(_rms(acc, gple_ref[...]).astype(BF16), wg_ref[...]))
    h3 = acc + _dot(p_ref[...].astype(BF16), wp_ref[...]) * gate
    h_out_ref[...] = h3
    if with_next:
        xn_out_ref[...] = _rms(h3, gnext_ref[...]).astype(BF16)


def _layer_spec(shape, layer):
    nd = len(shape)
    return pl.BlockSpec((None,) + tuple(shape), lambda *_: (layer,) + (0,) * nd, pipeline_mode=pl.Buffered(1))


def _post_block(h, a, p, pw, layer, wout, bout, sout):
    n = h.shape[0]
    tm = _row_tile(n, 512)
    depth = pw["w1"].shape[0]
    with_next = layer + 1 < depth
    row = lambda w: pl.BlockSpec((tm, w), lambda i: (i, 0))
    vec = _const_spec((1, D_MODEL))
    lvec = lambda l: _layer_spec((1, D_MODEL), l)
    out_shape = [jax.ShapeDtypeStruct((n, D_MODEL), F32)]
    out_specs = [row(D_MODEL)]
    in_specs = [row(D_MODEL), row(D_MODEL), pl.BlockSpec((None, tm, PLE_DIM), lambda i: (layer, i, 0)),
                _const_spec((D_MODEL, D_MODEL)), vec, vec, lvec(layer),
                _layer_spec((D_MODEL, D_FF), layer), _layer_spec((D_FF, D_MODEL), layer), lvec(layer),
                _layer_spec((D_MODEL, D_MODEL), layer), _layer_spec((PLE_DIM, D_MODEL), layer)]
    args = [h, a, p, wout, bout, sout, pw["gmlp"], pw["w1"], pw["w2"], pw["gple"], pw["wg"], pw["wp"]]
    if with_next:
        out_shape.append(jax.ShapeDtypeStruct((n, D_MODEL), BF16))
        out_specs.append(row(D_MODEL))
        in_specs.append(lvec(layer + 1))
        args.append(pw["gmix"])
    outs = pl.pallas_call(
        functools.partial(_post_kernel, ff_chunk=1024, with_next=with_next),
        grid=(n // tm,),
        in_specs=in_specs,
        out_specs=out_specs,
        out_shape=out_shape,
        compiler_params=_params("parallel"),
        name="post_block",
    )(*args)
    return (outs[0], outs[1]) if with_next else (outs[0], None)


def _pool_kernel(x_ref, prev_ref, g_ref, a_ref, pool_ref, xe_ref, *, tt, pos0):
    t = pl.program_id(1)
    ctx = POOL_CTX + 1

    @pl.when(t == 0)
    def _():
        xe_ref[0:1, :] = jnp.zeros((1, D_MODEL), F32)
        xe_ref[1:ctx, :] = prev_ref[0]

    xn = _rms(x_ref[0], g_ref[...])
    xe_ref[ctx:ctx + tt, :] = xn
    pos1 = (pos0 + 1 + t * tt + lax.broadcasted_iota(I32, (tt, 1), 0)).astype(F32)
    for g, w in enumerate(POOL_WINDOWS):
        cols = slice(g * POOL_GC, (g + 1) * POOL_GC)
        win = xn[:, cols]
        for j in range(1, w):
            win = win + xe_ref[ctx - j:ctx - j + tt, cols]
        mean = win / jnp.minimum(float(w), pos1)
        a_ref[0, :, cols] = (mean - xn[:, cols]).astype(BF16)
    tail = xe_ref[tt:tt + ctx, :]
    xe_ref[0:ctx, :] = tail

    @pl.when(t == pl.num_programs(1) - 1)
    def _():
        pool_ref[0] = xe_ref[tt + 1:tt + ctx, :]


def _pool_front(x, prev, g, pos0):
    b, t, _ = x.shape
    tt = _row_tile(t, 512)
    return pl.pallas_call(
        functools.partial(_pool_kernel, tt=tt, pos0=pos0),
        grid=(b, t // tt),
        in_specs=[pl.BlockSpec((1, tt, D_MODEL), lambda i, j: (i, j, 0)),
                  pl.BlockSpec((1, POOL_CTX, D_MODEL), lambda i, j: (i, 0, 0)),
                  _const_spec((1, D_MODEL))],
        out_specs=[pl.BlockSpec((1, tt, D_MODEL), lambda i, j: (i, j, 0)),
                   pl.BlockSpec((1, POOL_CTX, D_MODEL), lambda i, j: (i, 0, 0))],
        out_shape=[jax.ShapeDtypeStruct((b, t, D_MODEL), BF16),
                   jax.ShapeDtypeStruct((b, POOL_CTX, D_MODEL), F32)],
        scratch_shapes=[pltpu.VMEM((POOL_CTX + 1 + tt, D_MODEL), F32)],
        compiler_params=_params("arbitrary", "arbitrary"),
        name="pool_front",
    )(x, prev, g)


DSA_KV = N_KV_HEADS * HEAD_DIM
DSA_IQ = IDX_HEADS * IDX_DIM
DSA_COLS = (0, D_MODEL, D_MODEL + DSA_KV, D_MODEL + 2 * DSA_KV, D_MODEL + 2 * DSA_KV + DSA_IQ,
            D_MODEL + 2 * DSA_KV + DSA_IQ + IDX_DIM, D_MODEL + 2 * DSA_KV + DSA_IQ + IDX_DIM + IDX_HEADS)


def _dsa_in_kernel(xn_ref, w_ref, qg_ref, kg_ref, q_ref, iq_ref, iw_ref, k2_ref, v2_ref, ik_ref, *key_refs,
                   tm, for_attn):
    x = xn_ref[...]
    qb = QUERY_BLOCK
    assert tm % qb == 0 or not for_attn
    if for_attn:
        kb_ref, vt_ref, ikb_ref = key_refs
    col = lambda i: slice(DSA_COLS[i], DSA_COLS[i + 1])
    q = _dot(x, w_ref[:, col(0)])
    iq = _dot(x, w_ref[:, col(3)]).astype(BF16)
    for h in range(N_HEADS):
        qh = _rms(q[:, h * HEAD_DIM:(h + 1) * HEAD_DIM], qg_ref[...]).astype(BF16)
        iqh = iq[:, h * IDX_DIM:(h + 1) * IDX_DIM]
        if for_attn:
            for j in range(tm // qb):
                q_ref[j, h] = qh[j * qb:(j + 1) * qb]
                iq_ref[j, h] = iqh[j * qb:(j + 1) * qb]
        else:
            q_ref[:, h * HEAD_DIM:(h + 1) * HEAD_DIM] = qh
            iq_ref[:, h * IDX_DIM:(h + 1) * IDX_DIM] = iqh
    k = _dot(x, w_ref[:, col(1)])
    v = _dot(x, w_ref[:, col(2)])
    tail = _dot(x, w_ref[:, DSA_COLS[4]:DSA_COLS[6]])
    ik = tail[:, :IDX_DIM]
    ik_ref[...] = ik
    iw_ref[...] = tail[:, IDX_DIM:]
    for h in range(N_KV_HEADS):
        cols = slice(h * HEAD_DIM, (h + 1) * HEAD_DIM)
        kh = _rms(k[:, cols], kg_ref[...])
        k2_ref[pl.ds(h, tm, stride=N_KV_HEADS), :] = kh
        v2_ref[pl.ds(h, tm, stride=N_KV_HEADS), :] = v[:, cols]
        if for_attn:
            kb_ref[:, cols] = kh.astype(BF16)
            for j in range(tm // KEY_TILE):
                vt_ref[j, h] = v[j * KEY_TILE:(j + 1) * KEY_TILE, cols].T.astype(BF16)
    if for_attn:
        ikb_ref[...] = ik.astype(BF16)


def _dsa_in(xn, w, for_attn):
    n = xn.shape[0]
    tm = _row_tile(n, 512)
    qb = QUERY_BLOCK
    row = lambda wd: pl.BlockSpec((tm, wd), lambda i: (i, 0))
    lead = lambda shape: pl.BlockSpec(shape, lambda i: (i,) + (0,) * (len(shape) - 1))
    if for_attn:
        assert tm % KEY_TILE == 0 and tm % qb == 0
        q_specs = [lead((tm // qb, N_HEADS, qb, HEAD_DIM)), lead((tm // qb, IDX_HEADS, qb, IDX_DIM))]
        q_shapes = [jax.ShapeDtypeStruct((n // qb, N_HEADS, qb, HEAD_DIM), BF16),
                    jax.ShapeDtypeStruct((n // qb, IDX_HEADS, qb, IDX_DIM), BF16)]
        key_specs = [row(DSA_KV), lead((tm // KEY_TILE, N_KV_HEADS, HEAD_DIM, KEY_TILE)), row(IDX_DIM)]
        key_shapes = [jax.ShapeDtypeStruct((n, DSA_KV), BF16),
                      jax.ShapeDtypeStruct((n // KEY_TILE, N_KV_HEADS, HEAD_DIM, KEY_TILE), BF16),
                      jax.ShapeDtypeStruct((n, IDX_DIM), BF16)]
    else:
        q_specs = [row(D_MODEL), row(DSA_IQ)]
        q_shapes = [jax.ShapeDtypeStruct((n, D_MODEL), BF16), jax.ShapeDtypeStruct((n, DSA_IQ), BF16)]
        key_specs, key_shapes = [], []
    kv2 = pl.BlockSpec((N_KV_HEADS * tm, HEAD_DIM), lambda i: (i, 0))
    return pl.pallas_call(
        functools.partial(_dsa_in_kernel, tm=tm, for_attn=for_attn),
        grid=(n // tm,),
        in_specs=[row(D_MODEL), _const_spec((D_MODEL, DSA_COLS[-1])), _const_spec((1, HEAD_DIM)),
                  _const_spec((1, HEAD_DIM))],
        out_specs=q_specs + [row(IDX_HEADS), kv2, kv2, row(IDX_DIM)] + key_specs,
        out_shape=q_shapes + [jax.ShapeDtypeStruct((n, IDX_HEADS), F32),
                              jax.ShapeDtypeStruct((N_KV_HEADS * n, HEAD_DIM), F32),
                              jax.ShapeDtypeStruct((N_KV_HEADS * n, HEAD_DIM), F32),
                              jax.ShapeDtypeStruct((n, IDX_DIM), F32)] + key_shapes,
        compiler_params=_params("parallel"),
        name="dsa_in",
    )(xn, w["w_in"], w["qg"], w["kg"])


def _dsa_keys_kernel(ck_ref, cv_ref, k2_ref, v2_ref, kb_ref, vt_ref, *, past, t, nb):
    lp = kb_ref.shape[1]
    nh = N_KV_HEADS
    for r in range(nb):
        for h in range(nh):
            cols = slice((h * nb + r) * HEAD_DIM, (h * nb + r + 1) * HEAD_DIM)
            keys = slice(r * KEY_TILE, (r + 1) * KEY_TILE)
            kb_ref[0, 0:past, cols] = ck_ref[r, pl.ds(h, past, stride=nh), :].astype(BF16)
            kb_ref[0, past:past + t, cols] = k2_ref[r, pl.ds(h, t, stride=nh), :].astype(BF16)
            kb_ref[0, past + t:lp, cols] = jnp.zeros((lp - past - t, HEAD_DIM), BF16)
            for j in range(past // KEY_TILE):
                v_old = cv_ref[r, pl.ds(h + nh * KEY_TILE * j, KEY_TILE, stride=nh), :]
                vt_ref[0, j, h, :, keys] = v_old.T.astype(BF16)
            v_new = jnp.concatenate([v2_ref[r, pl.ds(h, t, stride=nh), :],
                                     jnp.zeros((KEY_TILE - t, HEAD_DIM), F32)], axis=0)
            vt_ref[0, past // KEY_TILE, h, :, keys] = v_new.T.astype(BF16)


def _dsa_keys(cache_k, cache_v, k2, v2, nb):
    b, past = cache_k.shape[:2]
    nh = N_KV_HEADS
    t = k2.shape[1] // nh
    lp = past + KEY_TILE
    assert past % KEY_TILE == 0 and t <= KEY_TILE and t % SUBLANES == 0 and b % nb == 0
    rows = lambda *shape: pl.BlockSpec((nb,) + shape, lambda i: (i,) + (0,) * len(shape))
    group = lambda *shape: pl.BlockSpec((1,) + shape, lambda i: (i,) + (0,) * len(shape))
    return pl.pallas_call(
        functools.partial(_dsa_keys_kernel, past=past, t=t, nb=nb),
        grid=(b // nb,),
        in_specs=[rows(nh * past, HEAD_DIM), rows(nh * past, HEAD_DIM), rows(nh * t, HEAD_DIM),
                  rows(nh * t, HEAD_DIM)],
        out_specs=[group(lp, nb * DSA_KV), group(lp // KEY_TILE, nh, HEAD_DIM, nb * KEY_TILE)],
        out_shape=[jax.ShapeDtypeStruct((b // nb, lp, nb * DSA_KV), BF16),
                   jax.ShapeDtypeStruct((b // nb, lp // KEY_TILE, nh, HEAD_DIM, nb * KEY_TILE), BF16)],
        compiler_params=_params("parallel"),
        name="dsa_keys",
    )(cache_k.reshape(b, nh * past, HEAD_DIM), cache_v.reshape(b, nh * past, HEAD_DIM), k2, v2)


def _dsa_attn_kernel(rb_ref, q_ref, iq_ref, iwt_ref, kb_ref, vt_ref, ik_ref, o_ref,
                     key_ref, msk_ref, lg_ref, bias_ref, j_ref, *, past, n_keys, topk, idx_bits, nb):
    kt_sz = KEY_TILE
    qb = o_ref.shape[1]
    near_slots = bias_ref.shape[1]
    qpb = qb // nb
    b = pl.program_id(0)
    i = pl.program_id(1)
    q0 = past + i * qpb
    kend = jnp.minimum(n_keys, (((q0 + qpb - 1) >> CHUNK_SHIFT) + 1) * CHUNK)
    nkt = (kend + kt_sz - 1) // kt_sz
    row_i = lax.broadcasted_iota(I32, (kt_sz, qb), 0)
    lane_i = lax.broadcasted_iota(I32, (kt_sz, qb), 1)
    col_i = lane_i % qpb

    @pl.when((b == 0) & (i == 0))
    def _():
        for w in range(1, near_slots):
            rel = row_i - col_i + (w - (near_slots - 1)) * qb
            n = jnp.abs(rel)
            log_bucket = jnp.full_like(n, REL_BUCKETS // 4)
            for edge in REL_LOG_EDGES:
                log_bucket = log_bucket + (n >= edge).astype(I32)
            bucket = jnp.where(n < REL_BUCKETS // 4, n, log_bucket) + jnp.where(rel > 0, REL_BUCKETS // 2, 0)
            for h in range(N_HEADS):
                val = jnp.zeros((kt_sz, qb), F32)
                for bk in range(REL_BUCKETS):
                    val = jnp.where(bucket == bk, rb_ref[bk, h], val)
                bias_ref[h, w] = val * LOG2_E
        for h in range(N_HEADS):
            bias_ref[h, 0] = jnp.full((kt_sz, qb), rb_ref[REL_FAR_BUCKET, h], F32) * LOG2_E

    iw = iwt_ref[0]
    iq_all = iq_ref[0, 0].reshape(IDX_HEADS * qb, nb * IDX_DIM)
    q_chunk = (q0 + col_i) >> CHUNK_SHIFT

    def tile_rows(jt):
        return pl.ds(pl.multiple_of(jt * kt_sz, kt_sz), kt_sz)

    def score_tile(jt, carry):
        ikt = ik_ref[0, tile_rows(jt), :]
        s = jnp.zeros((kt_sz, qb), F32)
        for hp in range(IDX_HEADS // 2):
            s2 = _dot_nt(ikt, iq_all[2 * hp * qb:(2 * hp + 2) * qb, :])
            for h in (2 * hp, 2 * hp + 1):
                s = s + iw[h:h + 1, :] * jnp.maximum(s2[:, (h - 2 * hp) * qb:(h - 2 * hp + 1) * qb], 0.0)
        s = s * (IDX_DIM ** -0.5)
        bits = pltpu.bitcast(s, I32)
        key = bits ^ ((bits >> 31) & 0x7FFFFFFF)
        kpos = jt * kt_sz + row_i
        adm = ((kpos >> CHUNK_SHIFT) <= q_chunk) & (kpos < n_keys)
        key_ref[tile_rows(jt), :] = jnp.where(adm, key, INT_MIN)
        return carry

    lax.fori_loop(0, nkt, score_tile, 0)

    def count(pred):
        def body(jt, acc):
            hit = jnp.where(pred(key_ref[tile_rows(jt), :], jt), 1.0, 0.0)
            return acc + _fold_rows(hit, jnp.add)

        acc = lax.fori_loop(0, nkt, body, jnp.zeros((8, qb), F32))
        return jnp.sum(acc, axis=0, keepdims=True)

    kf = float(topk)
    n_pos = count(lambda kk, jt: kk >= 0)
    n_adm = count(lambda kk, jt: kk > INT_MIN)
    thr0 = jnp.where(n_pos >= kf, 0, INT_MIN).astype(I32)
    n_ge0 = jnp.where(n_pos >= kf, n_pos, n_adm)

    def bisect(p, state):
        thr, n_ge = state
        cand = thr | jnp.left_shift(jnp.int32(1), 30 - p)
        n_cand = count(lambda kk, jt: kk >= cand)
        take = n_cand >= kf
        return jnp.where(take, cand, thr), jnp.where(take, n_cand, n_ge)

    thr, n_ge = lax.fori_loop(0, 31, bisect, (thr0, n_ge0))
    has_thr = thr > INT_MIN
    j_ref[...] = jnp.where(has_thr, n_keys, -1).astype(I32)
    tied_cut = jnp.max(jnp.where(has_thr & (n_ge > kf), 1, 0)) > 0

    @pl.when(tied_cut)
    def _():
        need = kf - count(lambda kk, jt: kk > thr)
        lo = jnp.zeros((1, qb), I32)
        for bit in range(idx_bits - 1, -1, -1):
            cand = lo + (1 << bit)
            below = count(lambda kk, jt: (kk == thr) & ((jt * kt_sz + row_i) < cand))
            lo = jnp.where(below < need, cand, lo)
        j_ref[...] = jnp.where(has_thr, lo, -1)

    j_last = j_ref[...]

    def mask_tile(jt, carry):
        kk = key_ref[tile_rows(jt), :]
        sel = (kk > thr) | ((kk == thr) & ((jt * kt_sz + row_i) <= j_last))
        msk_ref[tile_rows(jt), :] = jnp.where(sel, 0.0, -jnp.inf)
        return carry

    lax.fori_loop(0, nkt, mask_tile, 0)

    gq = GROUP * qb
    kdim = nb * HEAD_DIM
    slot0 = q0 // qb - (near_slots - 1)

    def bias_slot(jt):
        w = jt * (kt_sz // qb) - slot0
        return jnp.where(w >= 1, w, 0)

    kv_heads = range(N_KV_HEADS)
    q_groups = [q_ref[0, 0, kvh * GROUP:(kvh + 1) * GROUP].reshape(gq, kdim) for kvh in kv_heads]

    def logits_tile(jt, m8):
        msk = msk_ref[tile_rows(jt), :]
        w = bias_slot(jt)
        new_m8 = []
        for kvh in kv_heads:
            kt = kb_ref[0, tile_rows(jt), kvh * kdim:(kvh + 1) * kdim]
            lg = _dot_nt(kt, q_groups[kvh]) * (HEAD_DIM ** -0.5 * LOG2_E)
            parts = []
            for g in range(GROUP):
                h = kvh * GROUP + g
                lgh = lg[:, g * qb:(g + 1) * qb] + bias_ref[h, w] + msk
                lg_ref[tile_rows(jt), h * qb:(h + 1) * qb] = lgh
                parts.append(_fold_rows(lgh, jnp.maximum))
            new_m8.append(jnp.maximum(m8[kvh], jnp.concatenate(parts, axis=1)))
        return tuple(new_m8)

    m8 = lax.fori_loop(0, nkt, logits_tile, tuple(jnp.full((8, gq), -jnp.inf, F32) for _ in kv_heads))
    m = [jnp.max(m8[kvh], axis=0, keepdims=True) for kvh in kv_heads]

    lane_row_g = lax.broadcasted_iota(I32, (HEAD_DIM, gq), 1) % qb // qpb

    def pv_tile(jt, carry):
        new = []
        for kvh in kv_heads:
            acc, l8 = carry[kvh]
            p = jnp.exp2(lg_ref[tile_rows(jt), kvh * gq:(kvh + 1) * gq] - m[kvh])
            pb = p.astype(BF16)
            pv = _dot(vt_ref[0, jt, kvh, :, 0:kt_sz], pb)
            for r in range(1, nb):
                pv = jnp.where(lane_row_g == r, _dot(vt_ref[0, jt, kvh, :, r * kt_sz:(r + 1) * kt_sz], pb), pv)
            new.append((acc + pv, l8 + _fold_rows(p, jnp.add)))
        return tuple(new)

    zero = (jnp.zeros((HEAD_DIM, gq), F32), jnp.zeros((8, gq), F32))
    acc_l8 = lax.fori_loop(0, nkt, pv_tile, tuple(zero for _ in kv_heads))
    for kvh in kv_heads:
        acc, l8 = acc_l8[kvh]
        o_t = acc / jnp.sum(l8, axis=0, keepdims=True)
        for g in range(GROUP):
            h = kvh * GROUP + g
            o_ref[0, :, h * HEAD_DIM:(h + 1) * HEAD_DIM] = o_t[:, g * qb:(g + 1) * qb].T.astype(BF16)


def _dsa_attn(rel_bias, q, iq, iwt, kb, vt, ikb, *, past, n_keys):
    g, nblk = q.shape[:2]
    nb = q.shape[-1] // HEAD_DIM
    lp = kb.shape[1]
    qb = q.shape[3]
    near_slots = KEY_TILE // qb + 2
    assert lp % KEY_TILE == 0 and past % KEY_TILE == 0 and KEY_TILE % qb == 0 and qb >= REL_LOG_EDGES[-1]
    assert qb % nb == 0 and (nb == 1 or nblk == 1)
    topk = min(TOPK_MAX, n_keys // 4)
    return pl.pallas_call(
        functools.partial(_dsa_attn_kernel, past=past, n_keys=n_keys, topk=topk,
                          idx_bits=max(1, (lp - 1).bit_length()), nb=nb),
        grid=(g, nblk),
        in_specs=[pl.BlockSpec(memory_space=pltpu.SMEM),
                  pl.BlockSpec((1, 1, N_HEADS, qb, nb * HEAD_DIM), lambda i, j: (i, j, 0, 0, 0)),
                  pl.BlockSpec((1, 1, IDX_HEADS, qb, nb * IDX_DIM), lambda i, j: (i, j, 0, 0, 0)),
                  pl.BlockSpec((1, IDX_HEADS, qb), lambda i, j: (i, 0, j)),
                  pl.BlockSpec((1, lp, nb * DSA_KV), lambda i, j: (i, 0, 0)),
                  pl.BlockSpec((1, lp // KEY_TILE, N_KV_HEADS, HEAD_DIM, nb * KEY_TILE),
                               lambda i, j: (i, 0, 0, 0, 0)),
                  pl.BlockSpec((1, lp, nb * IDX_DIM), lambda i, j: (i, 0, 0))],
        out_specs=pl.BlockSpec((1, qb, D_MODEL), lambda i, j: (i, j, 0)),
        out_shape=jax.ShapeDtypeStruct((g, nblk * qb, D_MODEL), BF16),
        scratch_shapes=[pltpu.VMEM((lp, qb), I32), pltpu.VMEM((lp, qb), F32), pltpu.VMEM((lp, N_HEADS * qb), F32),
                        pltpu.VMEM((N_HEADS, near_slots, KEY_TILE, qb), F32), pltpu.VMEM((1, qb), I32)],
        compiler_params=_params("arbitrary", "arbitrary"),
        name="dsa_attn",
    )(rel_bias, q, iq, iwt, kb, vt, ikb)


def _dsa_mixer(xn, k_past, v_past, ik_past, w, b, t):
    past = k_past.shape[1]
    n_keys = past + t
    nh = N_KV_HEADS
    if past == 0:
        qb = QUERY_BLOCK
        assert t % KEY_TILE == 0 and t % qb == 0
        q, iq, iw, k2, v2, ik, kb, vt, ikb = _dsa_in(xn, w, for_attn=True)
        q = q.reshape(b, t // qb, N_HEADS, qb, HEAD_DIM)
        iq = iq.reshape(b, t // qb, IDX_HEADS, qb, IDX_DIM)
        iwt = iw.reshape(b, t, IDX_HEADS).transpose(0, 2, 1)
        kb = kb.reshape(b, t, DSA_KV)
        vt = vt.reshape(b, t // KEY_TILE, nh, HEAD_DIM, KEY_TILE)
        ikb = ikb.reshape(b, t, IDX_DIM)
    else:
        qb = SHARED_QUERY_BLOCK
        assert qb % t == 0 and b % (qb // t) == 0
        nb = qb // t
        g = b // nb
        q, iq, iw, k2, v2, ik = _dsa_in(xn, w, for_attn=False)
        eye = jnp.eye(nb, dtype=BF16)

        def own_slice(z, n_h):
            d = z.shape[-1] // n_h
            z = z.reshape(g, nb, t, n_h, d).transpose(0, 3, 1, 2, 4)
            z = z[:, :, :, :, None, :] * eye[None, None, :, None, :, None]
            return z.reshape(g, 1, n_h, qb, nb * d)

        q, iq = own_slice(q, N_HEADS), own_slice(iq, IDX_HEADS)
        iwt = iw.reshape(g, qb, IDX_HEADS).transpose(0, 2, 1)
        kb, vt = _dsa_keys(k_past, v_past, k2.reshape(b, nh * t, HEAD_DIM), v2.reshape(b, nh * t, HEAD_DIM), nb)
        lp = kb.shape[1]
        ikb = jnp.concatenate([ik_past.astype(BF16), ik.reshape(b, t, IDX_DIM).astype(BF16),
                               jnp.zeros((b, lp - n_keys, IDX_DIM), BF16)], axis=1)
        ikb = ikb.reshape(g, nb, lp, IDX_DIM).transpose(0, 2, 1, 3).reshape(g, lp, nb * IDX_DIM)
    o = _dsa_attn(w["rel_bias"], q, iq, iwt, kb, vt, ikb, past=past, n_keys=n_keys)
    o = o.reshape(b * t, D_MODEL)
    return (o, k2.reshape(b, t, nh, HEAD_DIM), v2.reshape(b, t, nh, HEAD_DIM), ik.reshape(b, t, IDX_DIM))


GLA_QK = GLA_HEADS * GLA_DK
GLA_COLS = (0, GLA_QK, 2 * GLA_QK, 2 * GLA_QK + D_MODEL, 2 * GLA_QK + 2 * D_MODEL, 2 * GLA_QK + 2 * D_MODEL + GLA_RANK)


def _gla_in_kernel(xn_ref, w_ref, wa2_ref, ba_ref, q_ref, k_ref, v_ref, r_ref, g_ref):
    x = xn_ref[...]
    proj = lambda i: _dot(x, w_ref[:, GLA_COLS[i]:GLA_COLS[i + 1]])
    q_ref[...] = proj(0) * (GLA_DK ** -0.5)
    k_ref[...] = proj(1)
    v_ref[...] = proj(2).astype(BF16)
    r_ref[...] = proj(3)
    z = _dot(proj(4).astype(BF16), wa2_ref[...]) + ba_ref[...]
    g_ref[...] = (jnp.minimum(z, 0.0) - jnp.log1p(jnp.exp(-jnp.abs(z)))) * (1.0 / GLA_GATE_NORM)


def _gla_in(xn, w):
    n = xn.shape[0]
    tm = _row_tile(n, 512)
    qk = GLA_HEADS * GLA_DK
    row = lambda wd: pl.BlockSpec((tm, wd), lambda i: (i, 0))
    return pl.pallas_call(
        _gla_in_kernel,
        grid=(n // tm,),
        in_specs=[row(D_MODEL), _const_spec((D_MODEL, GLA_COLS[-1])), _const_spec((GLA_RANK, qk)),
                  _const_spec((1, qk))],
        out_specs=[row(qk), row(qk), row(D_MODEL), row(D_MODEL), row(qk)],
        out_shape=[jax.ShapeDtypeStruct((n, qk), F32), jax.ShapeDtypeStruct((n, qk), F32),
                   jax.ShapeDtypeStruct((n, D_MODEL), BF16), jax.ShapeDtypeStruct((n, D_MODEL), F32),
                   jax.ShapeDtypeStruct((n, qk), F32)],
        compiler_params=_params("parallel"),
        name="gla_in",
    )(xn, w["w_in"], w["wa2"], w["ba"])


def _gla_chunk(q, k, g, v, st, tri_c, tri_sb):
    c = q.shape[0]
    sb = GLA_SUB
    cum = jnp.dot(tri_c, g, precision=HIGHEST, preferred_element_type=F32)
    total = cum[c - 1:c, :]
    o_inter = _dot_nt((q * jnp.exp(cum)).astype(BF16), st.astype(BF16))
    st_new = st * jnp.exp(total) + _dot_tn(v, (k * jnp.exp(total - cum)).astype(BF16))
    o_rows = []
    cum2 = cum * LOG2_E
    for i in range(c // sb):
        r = slice(i * sb, (i + 1) * sb)
        ci = cum2[r]
        decay = jnp.exp2(ci[:, None, :] - ci[None, :, :])
        sc = jnp.sum(decay * q[r][:, None, :] * k[r][None, :, :], axis=-1)
        sc = jnp.where(tri_sb, sc, 0.0)
        o_rows.append(o_inter[r] + _dot(sc.astype(BF16), v[r]))
    width = sb
    while width < c:
        for p in range(c // (2 * width)):
            lo = 2 * p * width
            left = slice(lo, lo + width)
            right = slice(lo + width, lo + 2 * width)
            edge = cum[lo + width - 1:lo + width, :]
            q_hat = (q[right] * jnp.exp(cum[right] - edge)).astype(BF16)
            k_hat = (k[left] * jnp.exp(edge - cum[left])).astype(BF16)
            o_pair = _dot(_dot_nt(q_hat, k_hat).astype(BF16), v[left])
            for j in range(width // sb):
                o_rows[(lo + width) // sb + j] += o_pair[j * sb:(j + 1) * sb]
        width *= 2
    return jnp.concatenate(o_rows, axis=0), st_new


def _gla_core_kernel(q_ref, k_ref, g_ref, v_ref, r_ref, s0_ref, gain_ref, a_ref, sout_ref, st_ref, *, tt, c):
    sb = GLA_SUB
    t = pl.program_id(1)

    @pl.when(t == 0)
    def _():
        for h in range(GLA_HEADS):
            st_ref[h] = s0_ref[0, h].T

    tri_c = (lax.broadcasted_iota(I32, (c, c), 0) >= lax.broadcasted_iota(I32, (c, c), 1)).astype(F32)
    tri_sb = lax.broadcasted_iota(I32, (sb, sb), 0) >= lax.broadcasted_iota(I32, (sb, sb), 1)
    gain = gain_ref[...]

    per_trip = 2 if (tt // c) % 2 == 0 else 1

    def chunks(ci, carry):
        for u in range(per_trip):
            rows = pl.ds(pl.multiple_of((ci * per_trip + u) * c, c), c)
            for h in range(GLA_HEADS):
                dk = slice(h * GLA_DK, (h + 1) * GLA_DK)
                dv = slice(h * GLA_DV, (h + 1) * GLA_DV)
                o, st_ref[h] = _gla_chunk(q_ref[0, rows, dk], k_ref[0, rows, dk], g_ref[0, rows, dk],
                                          v_ref[0, rows, dv], st_ref[h], tri_c, tri_sb)
                r = r_ref[0, rows, dv]
                a_ref[0, rows, dv] = (_rms(o, gain) * (r * jax.nn.sigmoid(r))).astype(BF16)
        return carry

    lax.fori_loop(0, tt // (c * per_trip), chunks, 0)

    @pl.when(t == pl.num_programs(1) - 1)
    def _():
        for h in range(GLA_HEADS):
            sout_ref[0, h] = st_ref[h].T


def _gla_core(q, k, g, v, r, s0, gain):
    b, t, _ = q.shape
    tt = _row_tile(t, 256)
    c = min(tt, CHUNK)
    assert tt % c == 0 and c % GLA_SUB == 0 and (c // GLA_SUB) & (c // GLA_SUB - 1) == 0
    qk = GLA_HEADS * GLA_DK
    tok = lambda wd: pl.BlockSpec((1, tt, wd), lambda i, j: (i, j, 0))
    st_spec = pl.BlockSpec((1, GLA_HEADS, GLA_DK, GLA_DV), lambda i, j: (i, 0, 0, 0))
    return pl.pallas_call(
        functools.partial(_gla_core_kernel, tt=tt, c=c),
        grid=(b, t // tt),
        in_specs=[tok(qk), tok(qk), tok(qk), tok(D_MODEL), tok(D_MODEL), st_spec, _const_spec((1, GLA_DV))],
        out_specs=[tok(D_MODEL), st_spec],
        out_shape=[jax.ShapeDtypeStruct((b, t, D_MODEL), BF16),
                   jax.ShapeDtypeStruct((b, GLA_HEADS, GLA_DK, GLA_DV), F32)],
        scratch_shapes=[pltpu.VMEM((GLA_HEADS, GLA_DV, GLA_DK), F32)],
        compiler_params=_params("arbitrary", "arbitrary"),
        name="gla_core",
    )(q, k, g, v, r, s0, gain)


def _conv_kernel(xn_ref, w_ref, prev_ref, cw_ref, cb_ref, a_ref, new_ref, ue_ref, *, tt):
    t = pl.program_id(1)
    pad = SUBLANES
    lo = pad - (CONV_W - 1)

    @pl.when(t == 0)
    def _():
        ue_ref[lo:pad, :] = prev_ref[0]

    x = xn_ref[0]
    proj = lambda i: _dot(x, w_ref[:, i * D_MODEL:(i + 1) * D_MODEL])
    u = proj(1) * proj(2)
    ue_ref[pad:pad + tt, :] = u
    conv = cb_ref[...] + cw_ref[CONV_W - 1:CONV_W, :] * u
    for j in range(CONV_W - 1):
        conv = conv + cw_ref[j:j + 1, :] * ue_ref[lo + j:lo + j + tt, :]
    a_ref[0] = (proj(0) * conv).astype(BF16)
    tail = ue_ref[lo + tt:pad + tt, :]
    ue_ref[lo:pad, :] = tail

    @pl.when(t == pl.num_programs(1) - 1)
    def _():
        new_ref[0] = tail


def _conv_front(xn, prev, w):
    b, t, _ = xn.shape
    tt = _row_tile(t, 512)
    tok = pl.BlockSpec((1, tt, D_MODEL), lambda i, j: (i, j, 0))
    st = pl.BlockSpec((1, CONV_W - 1, D_MODEL), lambda i, j: (i, 0, 0))
    w_spec = _const_spec((D_MODEL, 3 * D_MODEL))
    return pl.pallas_call(
        functools.partial(_conv_kernel, tt=tt),
        grid=(b, t // tt),
        in_specs=[tok, w_spec, st, _const_spec((CONV_W, D_MODEL)), _const_spec((1, D_MODEL))],
        out_specs=[tok, st],
        out_shape=[jax.ShapeDtypeStruct((b, t, D_MODEL), BF16),
                   jax.ShapeDtypeStruct((b, CONV_W - 1, D_MODEL), F32)],
        scratch_shapes=[pltpu.VMEM((SUBLANES + tt, D_MODEL), F32)],
        compiler_params=_params("arbitrary", "arbitrary"),
        name="conv_front",
    )(xn, w["w_in"], prev, w["cw"], w["cb"])


def _trunk(x, p, pool_prev, k_past, v_past, ik_past, gla_prev, conv_prev, w):
    b, t, _ = x.shape
    n = b * t
    past = k_past.shape[1]
    flat = lambda z: z.reshape(n, z.shape[-1])
    p = p.reshape(p.shape[0], n, PLE_DIM)
    post = lambda h, a, layer: _post_block(h, a, p, w["post"], layer, *w["mix_out"][layer])

    a, pool_new = _pool_front(x, pool_prev, w["gmix0"], past)
    h, xn = post(flat(x), flat(a), 0)

    a, k_new, v_new, ik_new = _dsa_mixer(xn, k_past, v_past, ik_past, w["dsa"], b, t)
    h, xn = post(h, a, 1)

    q, k, v, r, g = _gla_in(xn, w["gla"])
    seq = lambda z: z.reshape(b, t, z.shape[-1])
    a, gla_new = _gla_core(seq(q), seq(k), seq(g), seq(v), seq(r), gla_prev, w["gla"]["gain"])
    h, xn = post(h, flat(a), 2)

    a, conv_new = _conv_front(seq(xn), conv_prev, w["conv"])
    h, _ = post(h, flat(a), 3)
    return h.reshape(b, t, D_MODEL), pool_new, k_new, v_new, ik_new, gla_new, conv_new


def _prepare_weights(norm_mix, norm_mlp, norm_ple, w_mlp1, w_mlp2, w_ple_proj, w_ple_gate, w_pool, b_pool,
                     pool_scale, w_dsa_in, w_dsa_out, q_norm, k_norm, rel_bias, w_gla_in, w_gla_a2, b_gla_a,
                     gla_norm, w_gla_out, w_conv_in, conv_w, conv_b, w_conv_out):
    bf = lambda z: z.astype(BF16)
    vec = lambda z: z.reshape(1, -1).astype(F32)
    stack_vec = lambda z: z.reshape(z.shape[0], 1, z.shape[1]).astype(F32)
    w_pool_bd = jax.scipy.linalg.block_diag(*[w_pool[g] for g in range(w_pool.shape[0])])
    zeros = jnp.zeros((1, D_MODEL), F32)
    ones = jnp.ones((1, D_MODEL), F32)
    mix_out = [(bf(w_pool_bd), vec(b_pool), vec(pool_scale)), (bf(w_dsa_out), zeros, ones),
               (bf(w_gla_out), zeros, ones), (bf(w_conv_out), zeros, ones)]
    post = dict(gmix=stack_vec(norm_mix), gmlp=stack_vec(norm_mlp), gple=stack_vec(norm_ple), w1=bf(w_mlp1),
                w2=bf(w_mlp2), wg=bf(w_ple_gate), wp=bf(w_ple_proj))

    dsa = dict(w_in=bf(w_dsa_in), qg=vec(q_norm), kg=vec(k_norm), rel_bias=rel_bias.astype(F32))
    gla = dict(w_in=bf(w_gla_in), wa2=bf(w_gla_a2), ba=vec(b_gla_a), gain=vec(gla_norm))
    conv = dict(w_in=bf(w_conv_in), cw=conv_w.astype(F32), cb=vec(conv_b))
    return dict(post=post, mix_out=mix_out, gmix0=vec(norm_mix[0]), dsa=dsa, gla=gla, conv=conv)


def kernel(x_prompt, x_sample, p_prompt, p_sample, state_pool, cache_k, cache_v, cache_idx_k, state_gla, state_conv, norm_mix, norm_mlp, norm_ple, w_mlp1, w_mlp2, w_ple_proj, w_ple_gate, w_pool, b_pool, pool_scale, w_dsa_in, w_dsa_out, q_norm, k_norm, rel_bias, w_gla_in, w_gla_a2, b_gla_a, gla_norm, w_gla_out, w_conv_in, conv_w, conv_b, w_conv_out):
    w = _prepare_weights(norm_mix, norm_mlp, norm_ple, w_mlp1, w_mlp2, w_ple_proj, w_ple_gate, w_pool, b_pool,
                         pool_scale, w_dsa_in, w_dsa_out, q_norm, k_norm, rel_bias, w_gla_in, w_gla_a2, b_gla_a,
                         gla_norm, w_gla_out, w_conv_in, conv_w, conv_b, w_conv_out)
    bp = x_prompt.shape[0]
    dt = x_prompt.dtype
    y_p, pool_p, k_p, v_p, ik_p, gla_p, conv_p = _trunk(
        x_prompt, p_prompt,
        jnp.zeros((bp, POOL_CTX, D_MODEL), dt),
        jnp.zeros((bp, 0, N_KV_HEADS, HEAD_DIM), dt),
        jnp.zeros((bp, 0, N_KV_HEADS, HEAD_DIM), dt),
        jnp.zeros((bp, 0, IDX_DIM), dt),
        jnp.zeros((bp, GLA_HEADS, GLA_DK, GLA_DV), dt),
        jnp.zeros((bp, CONV_W - 1, D_MODEL), dt),
        w)
    y_s, pool_s, k_s, v_s, ik_s, gla_s, conv_s = _trunk(
        x_sample, p_sample, state_pool, cache_k, cache_v, cache_idx_k, state_gla, state_conv, w)
    return (y_p, y_s, pool_p, pool_s, k_p, v_p, ik_p, k_s, v_s, ik_s, gla_p, gla_s, conv_p, conv_s)
```

```python
import functools

import jax
import jax.numpy as jnp
from jax import lax
from jax.experimental import pallas as pl
from jax.experimental.pallas import tpu as pltpu

F32 = jnp.float32
BF16 = jnp.bfloat16
I32 = jnp.int32

D_MODEL = 1024
D_FF = 4 * D_MODEL
PLE_DIM = 256
EPS = 1e-6
CHUNK = 64
CHUNK_SHIFT = 6
POOL_WINDOWS = (2, 4, 8, 16)
POOL_GC = D_MODEL // len(POOL_WINDOWS)
POOL_CTX = max(POOL_WINDOWS) - 1
POOL_ROW_BLOCK = 128
N_HEADS = 8
N_KV_HEADS = 2
HEAD_DIM = D_MODEL // N_HEADS
GROUP = N_HEADS // N_KV_HEADS
IDX_HEADS = 8
IDX_DIM = 64
TOPK_MAX = 256
REL_BUCKETS = 32
REL_LOG_EDGES = (12, 16, 23, 32, 46, 64, 91)
REL_FAR_BUCKET = REL_BUCKETS // 2 - 1
GLA_HEADS = 4
GLA_DK = D_MODEL // 2 // GLA_HEADS
GLA_DV = D_MODEL // GLA_HEADS
GLA_RANK = 16
GLA_GATE_NORM = 16.0
GLA_SUB = 16
CONV_W = 3

V7X_VMEM_LIMIT_BYTES = 56 * 1024 * 1024
SUBLANES = 8
INT_MIN = -(2 ** 31)
QUERY_BLOCK = 256
SHARED_QUERY_BLOCK = 128
KEY_TILE = 256

HIGHEST = lax.Precision.HIGHEST
LOG2_E = 1.4426950408889634


def _params(*sem):
    return pltpu.CompilerParams(dimension_semantics=sem, vmem_limit_bytes=V7X_VMEM_LIMIT_BYTES)


def _const_spec(shape):
    nd = len(shape)
    return pl.BlockSpec(shape, lambda *_: (0,) * nd, pipeline_mode=pl.Buffered(1))


def _rms(x, g):
    return x * lax.rsqrt(jnp.mean(x * x, axis=-1, keepdims=True) + EPS) * g


def _dot(a, b):
    return jnp.dot(a, b, preferred_element_type=F32)


def _dot_nt(a, b):
    return lax.dot_general(a, b, (((1,), (1,)), ((), ())), preferred_element_type=F32)


def _dot_tn(a, b):
    return lax.dot_general(a, b, (((0,), (0,)), ((), ())), preferred_element_type=F32)


def _fold_rows(x, op):
    n = x.shape[0] // 8
    assert n & (n - 1) == 0
    parts = x.reshape(n, 8, x.shape[1])
    while n > 1:
        n //= 2
        parts = op(parts[:n], parts[n:])
    return parts[0]


def _row_tile(n, want):
    t = min(n, want)
    assert n % t == 0
    return t


def _pool_step(x, g, prev_ref, xe_ref, is_first, pos_first):
    tt = x.shape[0]
    ctx = POOL_CTX + 1

    @pl.when(is_first)
    def _():
        xe_ref[0:1, :] = jnp.zeros((1, D_MODEL), F32)
        xe_ref[1:ctx, :] = prev_ref[0]

    xe_ref[ctx:ctx + tt, :] = _rms(x, g)
    rb = min(tt, POOL_ROW_BLOCK)
    kk = rb + ctx
    t_i = lax.broadcasted_iota(I32, (rb, kk), 0) + ctx
    c_i = lax.broadcasted_iota(I32, (rb, kk), 1)
    bands = [((c_i <= t_i) & (c_i > t_i - w)).astype(BF16) for w in POOL_WINDOWS]
    blocks = []
    for r in range(tt // rb):
        slab = xe_ref[r * rb:r * rb + kk, :]
        hi = slab.astype(BF16)
        lo = (slab - hi.astype(F32)).astype(BF16)
        pos1 = (pos_first + 1 + r * rb + lax.broadcasted_iota(I32, (rb, 1), 0)).astype(F32)
        parts = []
        for gi, w in enumerate(POOL_WINDOWS):
            cols = slice(gi * POOL_GC, (gi + 1) * POOL_GC)
            win = _dot(bands[gi], hi[:, cols]) + _dot(bands[gi], lo[:, cols])
            parts.append((win / jnp.minimum(float(w), pos1) - slab[ctx:, cols]).astype(BF16))
        blocks.append(jnp.concatenate(parts, axis=1))
    xe_ref[0:ctx, :] = xe_ref[tt:tt + ctx, :]
    return jnp.concatenate(blocks, axis=0)


def _post_kernel(h_ref, a_ref, p_ref, wout_ref, bout_ref, sout_ref, gmlp_ref, w1_ref, w2_ref, gple_ref, wg_ref,
                 wp_ref, *rest, ff_chunk, with_next):
    if with_next:
        gnext_ref, h_out_ref, xn_out_ref = rest
    else:
        (h_out_ref,) = rest
    y = (_dot(a_ref[...], wout_ref[...]) + bout_ref[...]) * sout_ref[...]
    h1 = h_ref[...] + y
    hn = _rms(h1, gmlp_ref[...]).astype(BF16)
    acc = h1
    for c in range(D_FF // ff_chunk):
        cols = slice(c * ff_chunk, (c + 1) * ff_chunk)
        hid = jnp.square(jnp.maximum(_dot(hn, w1_ref[:, cols]), 0.0)).astype(BF16)
        acc = acc + _dot(hid, w2_ref[cols, :])
    gate = jax.nn.sigmoid(_dot(_rms(acc, gple_ref[...]).astype(BF16), wg_ref[...]))
    h3 = acc + _dot(p_ref[...].astype(BF16), wp_ref[...]) * gate
    h_out_ref[...] = h3
    if with_next:
        xn_out_ref[...] = _rms(h3, gnext_ref[...]).astype(BF16)


def _layer_spec(shape, layer):
    nd = len(shape)
    return pl.BlockSpec((None,) + tuple(shape), lambda *_: (layer,) + (0,) * nd, pipeline_mode=pl.Buffered(1))


def _post_block(h, a, p, pw, layer, wout, bout, sout):
    n = h.shape[0]
    tm = _row_tile(n, 512)
    depth = pw["w1"].shape[0]
    with_next = layer + 1 < depth
    row = lambda w: pl.BlockSpec((tm, w), lambda i: (i, 0))
    vec = _const_spec((1, D_MODEL))
    lvec = lambda l: _layer_spec((1, D_MODEL), l)
    out_shape = [jax.ShapeDtypeStruct((n, D_MODEL), F32)]
    out_specs = [row(D_MODEL)]
    in_specs = [row(D_MODEL), row(D_MODEL), pl.BlockSpec((None, tm, PLE_DIM), lambda i: (layer, i, 0)),
                _const_spec((D_MODEL, D_MODEL)), vec, vec, lvec(layer),
                _layer_spec((D_MODEL, D_FF), layer), _layer_spec((D_FF, D_MODEL), layer), lvec(layer),
                _layer_spec((D_MODEL, D_MODEL), layer), _layer_spec((PLE_DIM, D_MODEL), layer)]
    args = [h, a, p, wout, bout, sout, pw["gmlp"], pw["w1"], pw["w2"], pw["gple"], pw["wg"], pw["wp"]]
    if with_next:
        out_shape.append(jax.ShapeDtypeStruct((n, D_MODEL), BF16))
        out_specs.append(row(D_MODEL))
        in_specs.append(lvec(layer + 1))
        args.append(pw["gmix"])
    outs = pl.pallas_call(
        functools.partial(_post_kernel, ff_chunk=1024, with_next=with_next),
        grid=(n // tm,),
        in_specs=in_specs,
        out_specs=out_specs,
        out_shape=out_shape,
        compiler_params=_params("parallel"),
        name="post_block",
    )(*args)
    return (outs[0], outs[1]) if with_next else (outs[0], None)


def _pool_kernel(x_ref, prev_ref, g_ref, a_ref, pool_ref, xe_ref, *, tt, pos0):
    t = pl.program_id(1)
    a_ref[0] = _pool_step(x_ref[0], g_ref[...], prev_ref, xe_ref, t == 0, pos0 + t * tt)

    @pl.when(t == pl.num_programs(1) - 1)
    def _():
        pool_ref[0] = xe_ref[1:POOL_CTX + 1, :]


def _pool_front(x, prev, g, pos0):
    b, t, _ = x.shape
    tt = _row_tile(t, 512)
    return pl.pallas_call(
        functools.partial(_pool_kernel, tt=tt, pos0=pos0),
        grid=(b, t // tt),
        in_specs=[pl.BlockSpec((1, tt, D_MODEL), lambda i, j: (i, j, 0)),
                  pl.BlockSpec((1, POOL_CTX, D_MODEL), lambda i, j: (i, 0, 0)),
                  _const_spec((1, D_MODEL))],
        out_specs=[pl.BlockSpec((1, tt, D_MODEL), lambda i, j: (i, j, 0)),
                   pl.BlockSpec((1, POOL_CTX, D_MODEL), lambda i, j: (i, 0, 0))],
        out_shape=[jax.ShapeDtypeStruct((b, t, D_MODEL), BF16),
                   jax.ShapeDtypeStruct((b, POOL_CTX, D_MODEL), F32)],
        scratch_shapes=[pltpu.VMEM((POOL_CTX + 1 + tt, D_MODEL), F32)],
        compiler_params=_params("arbitrary", "arbitrary"),
        name="pool_front",
    )(x, prev, g)


DSA_KV = N_KV_HEADS * HEAD_DIM
DSA_IQ = IDX_HEADS * IDX_DIM
DSA_COLS = (0, D_MODEL, D_MODEL + DSA_KV, D_MODEL + 2 * DSA_KV, D_MODEL + 2 * DSA_KV + DSA_IQ,
            D_MODEL + 2 * DSA_KV + DSA_IQ + IDX_DIM, D_MODEL + 2 * DSA_KV + DSA_IQ + IDX_DIM + IDX_HEADS)


def _dsa_in_kernel(xn_ref, w_ref, qg_ref, kg_ref, q_ref, iq_ref, iw_ref, k2_ref, v2_ref, ik_ref, *key_refs,
                   tm, for_attn):
    x = xn_ref[...]
    qb = QUERY_BLOCK
    assert tm % qb == 0 or not for_attn
    if for_attn:
        kb_ref, vt_ref, ikb_ref = key_refs
    col = lambda i: slice(DSA_COLS[i], DSA_COLS[i + 1])
    q = _dot(x, w_ref[:, col(0)])
    iq = _dot(x, w_ref[:, col(3)]).astype(BF16)
    for h in range(N_HEADS):
        qh = _rms(q[:, h * HEAD_DIM:(h + 1) * HEAD_DIM], qg_ref[...]).astype(BF16)
        iqh = iq[:, h * IDX_DIM:(h + 1) * IDX_DIM]
        if for_attn:
            for j in range(tm // qb):
                q_ref[j, h] = qh[j * qb:(j + 1) * qb]
                iq_ref[j, h] = iqh[j * qb:(j + 1) * qb]
        else:
            q_ref[:, h * HEAD_DIM:(h + 1) * HEAD_DIM] = qh
            iq_ref[:, h * IDX_DIM:(h + 1) * IDX_DIM] = iqh
    k = _dot(x, w_ref[:, col(1)])
    v = _dot(x, w_ref[:, col(2)])
    tail = _dot(x, w_ref[:, DSA_COLS[4]:DSA_COLS[6]])
    ik = tail[:, :IDX_DIM]
    ik_ref[...] = ik
    iw_ref[...] = tail[:, IDX_DIM:]
    for h in range(N_KV_HEADS):
        cols = slice(h * HEAD_DIM, (h + 1) * HEAD_DIM)
        kh = _rms(k[:, cols], kg_ref[...])
        k2_ref[pl.ds(h, tm, stride=N_KV_HEADS), :] = kh
        v2_ref[pl.ds(h, tm, stride=N_KV_HEADS), :] = v[:, cols]
        if for_attn:
            kb_ref[:, cols] = kh.astype(BF16)
            for j in range(tm // KEY_TILE):
                vt_ref[j, h] = v[j * KEY_TILE:(j + 1) * KEY_TILE, cols].T.astype(BF16)
    if for_attn:
        ikb_ref[...] = ik.astype(BF16)


def _dsa_in(xn, w, for_attn):
    n = xn.shape[0]
    tm = _row_tile(n, 512)
    qb = QUERY_BLOCK
    row = lambda wd: pl.BlockSpec((tm, wd), lambda i: (i, 0))
    lead = lambda shape: pl.BlockSpec(shape, lambda i: (i,) + (0,) * (len(shape) - 1))
    if for_attn:
        assert tm % KEY_TILE == 0 and tm % qb == 0
        q_specs = [lead((tm // qb, N_HEADS, qb, HEAD_DIM)), lead((tm // qb, IDX_HEADS, qb, IDX_DIM))]
        q_shapes = [jax.ShapeDtypeStruct((n // qb, N_HEADS, qb, HEAD_DIM), BF16),
                    jax.ShapeDtypeStruct((n // qb, IDX_HEADS, qb, IDX_DIM), BF16)]
        key_specs = [row(DSA_KV), lead((tm // KEY_TILE, N_KV_HEADS, HEAD_DIM, KEY_TILE)), row(IDX_DIM)]
        key_shapes = [jax.ShapeDtypeStruct((n, DSA_KV), BF16),
                      jax.ShapeDtypeStruct((n // KEY_TILE, N_KV_HEADS, HEAD_DIM, KEY_TILE), BF16),
                      jax.ShapeDtypeStruct((n, IDX_DIM), BF16)]
    else:
        q_specs = [row(D_MODEL), row(DSA_IQ)]
        q_shapes = [jax.ShapeDtypeStruct((n, D_MODEL), BF16), jax.ShapeDtypeStruct((n, DSA_IQ), BF16)]
        key_specs, key_shapes = [], []
    kv2 = pl.BlockSpec((N_KV_HEADS * tm, HEAD_DIM), lambda i: (i, 0))
    return pl.pallas_call(
        functools.partial(_dsa_in_kernel, tm=tm, for_attn=for_attn),
        grid=(n // tm,),
        in_specs=[row(D_MODEL), _const_spec((D_MODEL, DSA_COLS[-1])), _const_spec((1, HEAD_DIM)),
                  _const_spec((1, HEAD_DIM))],
        out_specs=q_specs + [row(IDX_HEADS), kv2, kv2, row(IDX_DIM)] + key_specs,
        out_shape=q_shapes + [jax.ShapeDtypeStruct((n, IDX_HEADS), F32),
                              jax.ShapeDtypeStruct((N_KV_HEADS * n, HEAD_DIM), F32),
                              jax.ShapeDtypeStruct((N_KV_HEADS * n, HEAD_DIM), F32),
                              jax.ShapeDtypeStruct((n, IDX_DIM), F32)] + key_shapes,
        compiler_params=_params("parallel"),
        name="dsa_in",
    )(xn, w["w_in"], w["qg"], w["kg"])


def _dsa_keys_kernel(ck_ref, cv_ref, k2_ref, v2_ref, kb_ref, vt_ref, *, past, t, nb):
    lp = kb_ref.shape[1]
    nh = N_KV_HEADS
    for r in range(nb):
        for h in range(nh):
            cols = slice((h * nb + r) * HEAD_DIM, (h * nb + r + 1) * HEAD_DIM)
            keys = slice(r * KEY_TILE, (r + 1) * KEY_TILE)
            kb_ref[0, 0:past, cols] = ck_ref[r, pl.ds(h, past, stride=nh), :].astype(BF16)
            kb_ref[0, past:past + t, cols] = k2_ref[r, pl.ds(h, t, stride=nh), :].astype(BF16)
            kb_ref[0, past + t:lp, cols] = jnp.zeros((lp - past - t, HEAD_DIM), BF16)
            for j in range(past // KEY_TILE):
                v_old = cv_ref[r, pl.ds(h + nh * KEY_TILE * j, KEY_TILE, stride=nh), :]
                vt_ref[0, j, h, :, keys] = v_old.T.astype(BF16)
            v_new = jnp.concatenate([v2_ref[r, pl.ds(h, t, stride=nh), :],
                                     jnp.zeros((KEY_TILE - t, HEAD_DIM), F32)], axis=0)
            vt_ref[0, past // KEY_TILE, h, :, keys] = v_new.T.astype(BF16)


def _dsa_keys(cache_k, cache_v, k2, v2, nb):
    b, past = cache_k.shape[:2]
    nh = N_KV_HEADS
    t = k2.shape[1] // nh
    lp = past + KEY_TILE
    assert past % KEY_TILE == 0 and t <= KEY_TILE and t % SUBLANES == 0 and b % nb == 0
    rows = lambda *shape: pl.BlockSpec((nb,) + shape, lambda i: (i,) + (0,) * len(shape))
    group = lambda *shape: pl.BlockSpec((1,) + shape, lambda i: (i,) + (0,) * len(shape))
    return pl.pallas_call(
        functools.partial(_dsa_keys_kernel, past=past, t=t, nb=nb),
        grid=(b // nb,),
        in_specs=[rows(nh * past, HEAD_DIM), rows(nh * past, HEAD_DIM), rows(nh * t, HEAD_DIM),
                  rows(nh * t, HEAD_DIM)],
        out_specs=[group(lp, nb * DSA_KV), group(lp // KEY_TILE, nh, HEAD_DIM, nb * KEY_TILE)],
        out_shape=[jax.ShapeDtypeStruct((b // nb, lp, nb * DSA_KV), BF16),
                   jax.ShapeDtypeStruct((b // nb, lp // KEY_TILE, nh, HEAD_DIM, nb * KEY_TILE), BF16)],
        compiler_params=_params("parallel"),
        name="dsa_keys",
    )(cache_k.reshape(b, nh * past, HEAD_DIM), cache_v.reshape(b, nh * past, HEAD_DIM), k2, v2)


def _dsa_attn_kernel(rb_ref, q_ref, iq_ref, iwt_ref, kb_ref, vt_ref, ik_ref, o_ref,
                     key_ref, msk_ref, lg_ref, bias_ref, j_ref, *, past, n_keys, topk, idx_bits, nb):
    kt_sz = KEY_TILE
    qb = o_ref.shape[1]
    near_slots = bias_ref.shape[1]
    qpb = qb // nb
    b = pl.program_id(0)
    i = pl.program_id(1)
    q0 = past + i * qpb
    kend = jnp.minimum(n_keys, (((q0 + qpb - 1) >> CHUNK_SHIFT) + 1) * CHUNK)
    nkt = (kend + kt_sz - 1) // kt_sz
    row_i = lax.broadcasted_iota(I32, (kt_sz, qb), 0)
    lane_i = lax.broadcasted_iota(I32, (kt_sz, qb), 1)
    col_i = lane_i % qpb

    @pl.when((b == 0) & (i == 0))
    def _():
        for w in range(1, near_slots):
            rel = row_i - col_i + (w - (near_slots - 1)) * qb
            n = jnp.abs(rel)
            log_bucket = jnp.full_like(n, REL_BUCKETS // 4)
            for edge in REL_LOG_EDGES:
                log_bucket = log_bucket + (n >= edge).astype(I32)
            bucket = jnp.where(n < REL_BUCKETS // 4, n, log_bucket) + jnp.where(rel > 0, REL_BUCKETS // 2, 0)
            for h in range(N_HEADS):
                val = jnp.zeros((kt_sz, qb), F32)
                for bk in range(REL_BUCKETS):
                    val = jnp.where(bucket == bk, rb_ref[bk, h], val)
                bias_ref[h, w] = val * LOG2_E
        for h in range(N_HEADS):
            bias_ref[h, 0] = jnp.full((kt_sz, qb), rb_ref[REL_FAR_BUCKET, h], F32) * LOG2_E

    iw = iwt_ref[0]
    iq_all = iq_ref[0, 0].reshape(IDX_HEADS * qb, nb * IDX_DIM)
    q_chunk = (q0 + col_i) >> CHUNK_SHIFT

    def tile_rows(jt):
        return pl.ds(pl.multiple_of(jt * kt_sz, kt_sz), kt_sz)

    def score_tile(jt, carry):
        ikt = ik_ref[0, tile_rows(jt), :]
        s = jnp.zeros((kt_sz, qb), F32)
        for hp in range(IDX_HEADS // 2):
            s2 = _dot_nt(ikt, iq_all[2 * hp * qb:(2 * hp + 2) * qb, :])
            for h in (2 * hp, 2 * hp + 1):
                s = s + iw[h:h + 1, :] * jnp.maximum(s2[:, (h - 2 * hp) * qb:(h - 2 * hp + 1) * qb], 0.0)
        s = s * (IDX_DIM ** -0.5)
        bits = pltpu.bitcast(s, I32)
        key = bits ^ ((bits >> 31) & 0x7FFFFFFF)
        kpos = jt * kt_sz + row_i
        adm = ((kpos >> CHUNK_SHIFT) <= q_chunk) & (kpos < n_keys)
        key_ref[tile_rows(jt), :] = jnp.where(adm, key, INT_MIN)
        return carry

    lax.fori_loop(0, nkt, score_tile, 0)

    def count(pred):
        def body(jt, acc):
            hit = jnp.where(pred(key_ref[tile_rows(jt), :], jt), 1.0, 0.0)
            return acc + _fold_rows(hit, jnp.add)

        acc = lax.fori_loop(0, nkt, body, jnp.zeros((8, qb), F32))
        return jnp.sum(acc, axis=0, keepdims=True)

    kf = float(topk)
    n_pos = count(lambda kk, jt: kk >= 0)
    n_adm = count(lambda kk, jt: kk > INT_MIN)
    thr0 = jnp.where(n_pos >= kf, 0, INT_MIN).astype(I32)
    n_ge0 = jnp.where(n_pos >= kf, n_pos, n_adm)

    def bisect(p, state):
        thr, n_ge = state
        cand = thr | jnp.left_shift(jnp.int32(1), 30 - p)
        n_cand = count(lambda kk, jt: kk >= cand)
        take = n_cand >= kf
        return jnp.where(take, cand, thr), jnp.where(take, n_cand, n_ge)

    thr, n_ge = lax.fori_loop(0, 31, bisect, (thr0, n_ge0))
    has_thr = thr > INT_MIN
    j_ref[...] = jnp.where(has_thr, n_keys, -1).astype(I32)
    tied_cut = jnp.max(jnp.where(has_thr & (n_ge > kf), 1, 0)) > 0

    @pl.when(tied_cut)
    def _():
        need = kf - count(lambda kk, jt: kk > thr)
        lo = jnp.zeros((1, qb), I32)
        for bit in range(idx_bits - 1, -1, -1):
            cand = lo + (1 << bit)
            below = count(lambda kk, jt: (kk == thr) & ((jt * kt_sz + row_i) < cand))
            lo = jnp.where(below < need, cand, lo)
        j_ref[...] = jnp.where(has_thr, lo, -1)

    j_last = j_ref[...]

    def mask_tile(jt, carry):
        kk = key_ref[tile_rows(jt), :]
        sel = (kk > thr) | ((kk == thr) & ((jt * kt_sz + row_i) <= j_last))
        msk_ref[tile_rows(jt), :] = jnp.where(sel, 0.0, -jnp.inf)
        return carry

    lax.fori_loop(0, nkt, mask_tile, 0)

    gq = GROUP * qb
    kdim = nb * HEAD_DIM
    slot0 = q0 // qb - (near_slots - 1)

    def bias_slot(jt):
        w = jt * (kt_sz // qb) - slot0
        return jnp.where(w >= 1, w, 0)

    kv_heads = range(N_KV_HEADS)
    q_groups = [q_ref[0, 0, kvh * GROUP:(kvh + 1) * GROUP].reshape(gq, kdim) for kvh in kv_heads]

    def logits_tile(jt, m8):
        msk = msk_ref[tile_rows(jt), :]
        w = bias_slot(jt)
        new_m8 = []
        for kvh in kv_heads:
            kt = kb_ref[0, tile_rows(jt), kvh * kdim:(kvh + 1) * kdim]
            lg = _dot_nt(kt, q_groups[kvh]) * (HEAD_DIM ** -0.5 * LOG2_E)
            parts = []
            for g in range(GROUP):
                h = kvh * GROUP + g
                lgh = lg[:, g * qb:(g + 1) * qb] + bias_ref[h, w] + msk
                lg_ref[tile_rows(jt), h * qb:(h + 1) * qb] = lgh
                parts.append(_fold_rows(lgh, jnp.maximum))
            new_m8.append(jnp.maximum(m8[kvh], jnp.concatenate(parts, axis=1)))
        return tuple(new_m8)

    m8 = lax.fori_loop(0, nkt, logits_tile, tuple(jnp.full((8, gq), -jnp.inf, F32) for _ in kv_heads))
    m = [jnp.max(m8[kvh], axis=0, keepdims=True) for kvh in kv_heads]

    lane_row_g = lax.broadcasted_iota(I32, (HEAD_DIM, gq), 1) % qb // qpb

    def pv_tile(jt, carry):
        new = []
        for kvh in kv_heads:
            acc, l8 = carry[kvh]
            p = jnp.exp2(lg_ref[tile_rows(jt), kvh * gq:(kvh + 1) * gq] - m[kvh])
            pb = p.astype(BF16)
            pv = _dot(vt_ref[0, jt, kvh, :, 0:kt_sz], pb)
            for r in range(1, nb):
                pv = jnp.where(lane_row_g == r, _dot(vt_ref[0, jt, kvh, :, r * kt_sz:(r + 1) * kt_sz], pb), pv)
            new.append((acc + pv, l8 + _fold_rows(p, jnp.add)))
        return tuple(new)

    zero = (jnp.zeros((HEAD_DIM, gq), F32), jnp.zeros((8, gq), F32))
    acc_l8 = lax.fori_loop(0, nkt, pv_tile, tuple(zero for _ in kv_heads))
    for kvh in kv_heads:
        acc, l8 = acc_l8[kvh]
        o_t = acc / jnp.sum(l8, axis=0, keepdims=True)
        for g in range(GROUP):
            h = kvh * GROUP + g
            o_ref[0, :, h * HEAD_DIM:(h + 1) * HEAD_DIM] = o_t[:, g * qb:(g + 1) * qb].T.astype(BF16)


def _dsa_attn(rel_bias, q, iq, iwt, kb, vt, ikb, *, past, n_keys):
    g, nblk = q.shape[:2]
    nb = q.shape[-1] // HEAD_DIM
    lp = kb.shape[1]
    qb = q.shape[3]
    near_slots = KEY_TILE // qb + 2
    assert lp % KEY_TILE == 0 and past % KEY_TILE == 0 and KEY_TILE % qb == 0 and qb >= REL_LOG_EDGES[-1]
    assert qb % nb == 0 and (nb == 1 or nblk == 1)
    topk = min(TOPK_MAX, n_keys // 4)
    return pl.pallas_call(
        functools.partial(_dsa_attn_kernel, past=past, n_keys=n_keys, topk=topk,
                          idx_bits=max(1, (lp - 1).bit_length()), nb=nb),
        grid=(g, nblk),
        in_specs=[pl.BlockSpec(memory_space=pltpu.SMEM),
                  pl.BlockSpec((1, 1, N_HEADS, qb, nb * HEAD_DIM), lambda i, j: (i, j, 0, 0, 0)),
                  pl.BlockSpec((1, 1, IDX_HEADS, qb, nb * IDX_DIM), lambda i, j: (i, j, 0, 0, 0)),
                  pl.BlockSpec((1, IDX_HEADS, qb), lambda i, j: (i, 0, j)),
                  pl.BlockSpec((1, lp, nb * DSA_KV), lambda i, j: (i, 0, 0)),
                  pl.BlockSpec((1, lp // KEY_TILE, N_KV_HEADS, HEAD_DIM, nb * KEY_TILE),
                               lambda i, j: (i, 0, 0, 0, 0)),
                  pl.BlockSpec((1, lp, nb * IDX_DIM), lambda i, j: (i, 0, 0))],
        out_specs=pl.BlockSpec((1, qb, D_MODEL), lambda i, j: (i, j, 0)),
        out_shape=jax.ShapeDtypeStruct((g, nblk * qb, D_MODEL), BF16),
        scratch_shapes=[pltpu.VMEM((lp, qb), I32), pltpu.VMEM((lp, qb), F32), pltpu.VMEM((lp, N_HEADS * qb), F32),
                        pltpu.VMEM((N_HEADS, near_slots, KEY_TILE, qb), F32), pltpu.VMEM((1, qb), I32)],
        compiler_params=_params("arbitrary", "arbitrary"),
        name="dsa_attn",
    )(rel_bias, q, iq, iwt, kb, vt, ikb)


def _dsa_mixer(xn, k_past, v_past, ik_past, w, b, t):
    past = k_past.shape[1]
    n_keys = past + t
    nh = N_KV_HEADS
    if past == 0:
        qb = QUERY_BLOCK
        assert t % KEY_TILE == 0 and t % qb == 0
        q, iq, iw, k2, v2, ik, kb, vt, ikb = _dsa_in(xn, w, for_attn=True)
        q = q.reshape(b, t // qb, N_HEADS, qb, HEAD_DIM)
        iq = iq.reshape(b, t // qb, IDX_HEADS, qb, IDX_DIM)
        iwt = iw.reshape(b, t, IDX_HEADS).transpose(0, 2, 1)
        kb = kb.reshape(b, t, DSA_KV)
        vt = vt.reshape(b, t // KEY_TILE, nh, HEAD_DIM, KEY_TILE)
        ikb = ikb.reshape(b, t, IDX_DIM)
    else:
        qb = SHARED_QUERY_BLOCK
        assert qb % t == 0 and b % (qb // t) == 0
        nb = qb // t
        g = b // nb
        q, iq, iw, k2, v2, ik = _dsa_in(xn, w, for_attn=False)
        eye = jnp.eye(nb, dtype=BF16)

        def own_slice(z, n_h):
            d = z.shape[-1] // n_h
            z = z.reshape(g, nb, t, n_h, d).transpose(0, 3, 1, 2, 4)
            z = z[:, :, :, :, None, :] * eye[None, None, :, None, :, None]
            return z.reshape(g, 1, n_h, qb, nb * d)

        q, iq = own_slice(q, N_HEADS), own_slice(iq, IDX_HEADS)
        iwt = iw.reshape(g, qb, IDX_HEADS).transpose(0, 2, 1)
        kb, vt = _dsa_keys(k_past, v_past, k2.reshape(b, nh * t, HEAD_DIM), v2.reshape(b, nh * t, HEAD_DIM), nb)
        lp = kb.shape[1]
        ikb = jnp.concatenate([ik_past.astype(BF16), ik.reshape(b, t, IDX_DIM).astype(BF16),
                               jnp.zeros((b, lp - n_keys, IDX_DIM), BF16)], axis=1)
        ikb = ikb.reshape(g, nb, lp, IDX_DIM).transpose(0, 2, 1, 3).reshape(g, lp, nb * IDX_DIM)
    o = _dsa_attn(w["rel_bias"], q, iq, iwt, kb, vt, ikb, past=past, n_keys=n_keys)
    o = o.reshape(b * t, D_MODEL)
    return (o, k2.reshape(b, t, nh, HEAD_DIM), v2.reshape(b, t, nh, HEAD_DIM), ik.reshape(b, t, IDX_DIM))


GLA_QK = GLA_HEADS * GLA_DK
GLA_COLS = (0, GLA_QK, 2 * GLA_QK, 2 * GLA_QK + D_MODEL, 2 * GLA_QK + 2 * D_MODEL, 2 * GLA_QK + 2 * D_MODEL + GLA_RANK)


def _gla_in_kernel(xn_ref, w_ref, wa2_ref, ba_ref, q_ref, k_ref, v_ref, r_ref, g_ref):
    x = xn_ref[...]
    proj = lambda i: _dot(x, w_ref[:, GLA_COLS[i]:GLA_COLS[i + 1]])
    q_ref[...] = proj(0) * (GLA_DK ** -0.5)
    k_ref[...] = proj(1)
    v_ref[...] = proj(2).astype(BF16)
    r_ref[...] = proj(3)
    z = _dot(proj(4).astype(BF16), wa2_ref[...]) + ba_ref[...]
    g_ref[...] = (jnp.minimum(z, 0.0) - jnp.log1p(jnp.exp(-jnp.abs(z)))) * (1.0 / GLA_GATE_NORM)


def _gla_in(xn, w):
    n = xn.shape[0]
    tm = _row_tile(n, 512)
    qk = GLA_HEADS * GLA_DK
    row = lambda wd: pl.BlockSpec((tm, wd), lambda i: (i, 0))
    return pl.pallas_call(
        _gla_in_kernel,
        grid=(n // tm,),
        in_specs=[row(D_MODEL), _const_spec((D_MODEL, GLA_COLS[-1])), _const_spec((GLA_RANK, qk)),
                  _const_spec((1, qk))],
        out_specs=[row(qk), row(qk), row(D_MODEL), row(D_MODEL), row(qk)],
        out_shape=[jax.ShapeDtypeStruct((n, qk), F32), jax.ShapeDtypeStruct((n, qk), F32),
                   jax.ShapeDtypeStruct((n, D_MODEL), BF16), jax.ShapeDtypeStruct((n, D_MODEL), F32),
                   jax.ShapeDtypeStruct((n, qk), F32)],
        compiler_params=_params("parallel"),
        name="gla_in",
    )(xn, w["w_in"], w["wa2"], w["ba"])


def _gla_chunk(q, k, g, v, st, tri_c, tri_sb):
    c = q.shape[0]
    sb = GLA_SUB
    cum = jnp.dot(tri_c, g, precision=HIGHEST, preferred_element_type=F32)
    total = cum[c - 1:c, :]
    o_inter = _dot_nt((q * jnp.exp(cum)).astype(BF16), st.astype(BF16))
    st_new = st * jnp.exp(total) + _dot_tn(v, (k * jnp.exp(total - cum)).astype(BF16))
    o_rows = []
    cum2 = cum * LOG2_E
    for i in range(c // sb):
        r = slice(i * sb, (i + 1) * sb)
        ci = cum2[r]
        decay = jnp.exp2(ci[:, None, :] - ci[None, :, :])
        sc = jnp.sum(decay * q[r][:, None, :] * k[r][None, :, :], axis=-1)
        sc = jnp.where(tri_sb, sc, 0.0)
        o_rows.append(o_inter[r] + _dot(sc.astype(BF16), v[r]))
    width = sb
    while width < c:
        for p in range(c // (2 * width)):
            lo = 2 * p * width
            left = slice(lo, lo + width)
            right = slice(lo + width, lo + 2 * width)
            edge = cum[lo + width - 1:lo + width, :]
            q_hat = (q[right] * jnp.exp(cum[right] - edge)).astype(BF16)
            k_hat = (k[left] * jnp.exp(edge - cum[left])).astype(BF16)
            o_pair = _dot(_dot_nt(q_hat, k_hat).astype(BF16), v[left])
            for j in range(width // sb):
                o_rows[(lo + width) // sb + j] += o_pair[j * sb:(j + 1) * sb]
        width *= 2
    return jnp.concatenate(o_rows, axis=0), st_new


def _gla_core_kernel(q_ref, k_ref, g_ref, v_ref, r_ref, s0_ref, gain_ref, a_ref, sout_ref, st_ref, *, tt, c):
    sb = GLA_SUB
    t = pl.program_id(1)

    @pl.when(t == 0)
    def _():
        for h in range(GLA_HEADS):
            st_ref[h] = s0_ref[0, h].T

    tri_c = (lax.broadcasted_iota(I32, (c, c), 0) >= lax.broadcasted_iota(I32, (c, c), 1)).astype(F32)
    tri_sb = lax.broadcasted_iota(I32, (sb, sb), 0) >= lax.broadcasted_iota(I32, (sb, sb), 1)
    gain = gain_ref[...]

    per_trip = 4 if (tt // c) % 4 == 0 else 1

    def chunks(ci, carry):
        for u in range(per_trip):
            rows = pl.ds(pl.multiple_of((ci * per_trip + u) * c, c), c)
            for h in range(GLA_HEADS):
                dk = slice(h * GLA_DK, (h + 1) * GLA_DK)
                dv = slice(h * GLA_DV, (h + 1) * GLA_DV)
                o, st_ref[h] = _gla_chunk(q_ref[0, rows, dk], k_ref[0, rows, dk], g_ref[0, rows, dk],
                                          v_ref[0, rows, dv], st_ref[h], tri_c, tri_sb)
                r = r_ref[0, rows, dv]
                a_ref[0, rows, dv] = (_rms(o, gain) * (r * jax.nn.sigmoid(r))).astype(BF16)
        return carry

    lax.fori_loop(0, tt // (c * per_trip), chunks, 0)

    @pl.when(t == pl.num_programs(1) - 1)
    def _():
        for h in range(GLA_HEADS):
            sout_ref[0, h] = st_ref[h].T


def _gla_core(q, k, g, v, r, s0, gain):
    b, t, _ = q.shape
    tt = _row_tile(t, 256)
    c = min(tt, CHUNK)
    assert tt % c == 0 and c % GLA_SUB == 0 and (c // GLA_SUB) & (c // GLA_SUB - 1) == 0
    qk = GLA_HEADS * GLA_DK
    tok = lambda wd: pl.BlockSpec((1, tt, wd), lambda i, j: (i, j, 0))
    st_spec = pl.BlockSpec((1, GLA_HEADS, GLA_DK, GLA_DV), lambda i, j: (i, 0, 0, 0))
    return pl.pallas_call(
        functools.partial(_gla_core_kernel, tt=tt, c=c),
        grid=(b, t // tt),
        in_specs=[tok(qk), tok(qk), tok(qk), tok(D_MODEL), tok(D_MODEL), st_spec, _const_spec((1, GLA_DV))],
        out_specs=[tok(D_MODEL), st_spec],
        out_shape=[jax.ShapeDtypeStruct((b, t, D_MODEL), BF16),
                   jax.ShapeDtypeStruct((b, GLA_HEADS, GLA_DK, GLA_DV), F32)],
        scratch_shapes=[pltpu.VMEM((GLA_HEADS, GLA_DV, GLA_DK), F32)],
        compiler_params=_params("arbitrary", "arbitrary"),
        name="gla_core",
    )(q, k, g, v, r, s0, gain)


def _conv_kernel(xn_ref, w_ref, prev_ref, cw_ref, cb_ref, a_ref, new_ref, ue_ref, *, tt):
    t = pl.program_id(1)
    pad = SUBLANES
    lo = pad - (CONV_W - 1)

    @pl.when(t == 0)
    def _():
        ue_ref[lo:pad, :] = prev_ref[0]

    x = xn_ref[0]
    proj = lambda i: _dot(x, w_ref[:, i * D_MODEL:(i + 1) * D_MODEL])
    u = proj(1) * proj(2)
    ue_ref[pad:pad + tt, :] = u
    conv = cb_ref[...] + cw_ref[CONV_W - 1:CONV_W, :] * u
    for j in range(CONV_W - 1):
        conv = conv + cw_ref[j:j + 1, :] * ue_ref[lo + j:lo + j + tt, :]
    a_ref[0] = (proj(0) * conv).astype(BF16)
    tail = ue_ref[lo + tt:pad + tt, :]
    ue_ref[lo:pad, :] = tail

    @pl.when(t == pl.num_programs(1) - 1)
    def _():
        new_ref[0] = tail


def _conv_front(xn, prev, w):
    b, t, _ = xn.shape
    tt = _row_tile(t, 512)
    tok = pl.BlockSpec((1, tt, D_MODEL), lambda i, j: (i, j, 0))
    st = pl.BlockSpec((1, CONV_W - 1, D_MODEL), lambda i, j: (i, 0, 0))
    w_spec = _const_spec((D_MODEL, 3 * D_MODEL))
    return pl.pallas_call(
        functools.partial(_conv_kernel, tt=tt),
        grid=(b, t // tt),
        in_specs=[tok, w_spec, st, _const_spec((CONV_W, D_MODEL)), _const_spec((1, D_MODEL))],
        out_specs=[tok, st],
        out_shape=[jax.ShapeDtypeStruct((b, t, D_MODEL), BF16),
                   jax.ShapeDtypeStruct((b, CONV_W - 1, D_MODEL), F32)],
        scratch_shapes=[pltpu.VMEM((SUBLANES + tt, D_MODEL), F32)],
        compiler_params=_params("arbitrary", "arbitrary"),
        name="conv_front",
    )(xn, w["w_in"], prev, w["cw"], w["cb"])


def _trunk(x, p, pool_prev, k_past, v_past, ik_past, gla_prev, conv_prev, w):
    b, t, _ = x.shape
    n = b * t
    past = k_past.shape[1]
    flat = lambda z: z.reshape(n, z.shape[-1])
    p = p.reshape(p.shape[0], n, PLE_DIM)
    post = lambda h, a, layer: _post_block(h, a, p, w["post"], layer, *w["mix_out"][layer])

    a, pool_new = _pool_front(x, pool_prev, w["gmix0"], past)
    h, xn = post(flat(x), flat(a), 0)

    a, k_new, v_new, ik_new = _dsa_mixer(xn, k_past, v_past, ik_past, w["dsa"], b, t)
    h, xn = post(h, a, 1)

    q, k, v, r, g = _gla_in(xn, w["gla"])
    seq = lambda z: z.reshape(b, t, z.shape[-1])
    a, gla_new = _gla_core(seq(q), seq(k), seq(g), seq(v), seq(r), gla_prev, w["gla"]["gain"])
    h, xn = post(h, flat(a), 2)

    a, conv_new = _conv_front(seq(xn), conv_prev, w["conv"])
    h, _ = post(h, flat(a), 3)
    return h.reshape(b, t, D_MODEL), pool_new, k_new, v_new, ik_new, gla_new, conv_new


def _prepare_weights(norm_mix, norm_mlp, norm_ple, w_mlp1, w_mlp2, w_ple_proj, w_ple_gate, w_pool, b_pool,
                     pool_scale, w_dsa_in, w_dsa_out, q_norm, k_norm, rel_bias, w_gla_in, w_gla_a2, b_gla_a,
                     gla_norm, w_gla_out, w_conv_in, conv_w, conv_b, w_conv_out):
    bf = lambda z: z.astype(BF16)
    vec = lambda z: z.reshape(1, -1).astype(F32)
    stack_vec = lambda z: z.reshape(z.shape[0], 1, z.shape[1]).astype(F32)
    w_pool_bd = jax.scipy.linalg.block_diag(*[w_pool[g] for g in range(w_pool.shape[0])])
    zeros = jnp.zeros((1, D_MODEL), F32)
    ones = jnp.ones((1, D_MODEL), F32)
    mix_out = [(bf(w_pool_bd), vec(b_pool), vec(pool_scale)), (bf(w_dsa_out), zeros, ones),
               (bf(w_gla_out), zeros, ones), (bf(w_conv_out), zeros, ones)]
    post = dict(gmix=stack_vec(norm_mix), gmlp=stack_vec(norm_mlp), gple=stack_vec(norm_ple), w1=bf(w_mlp1),
                w2=bf(w_mlp2), wg=bf(w_ple_gate), wp=bf(w_ple_proj))

    dsa = dict(w_in=bf(w_dsa_in), qg=vec(q_norm), kg=vec(k_norm), rel_bias=rel_bias.astype(F32))
    gla = dict(w_in=bf(w_gla_in), wa2=bf(w_gla_a2), ba=vec(b_gla_a), gain=vec(gla_norm))
    conv = dict(w_in=bf(w_conv_in), cw=conv_w.astype(F32), cb=vec(conv_b))
    return dict(post=post, mix_out=mix_out, gmix0=vec(norm_mix[0]), dsa=dsa, gla=gla, conv=conv)


def kernel(x_prompt, x_sample, p_prompt, p_sample, state_pool, cache_k, cache_v, cache_idx_k, state_gla, state_conv, norm_mix, norm_mlp, norm_ple, w_mlp1, w_mlp2, w_ple_proj, w_ple_gate, w_pool, b_pool, pool_scale, w_dsa_in, w_dsa_out, q_norm, k_norm, rel_bias, w_gla_in, w_gla_a2, b_gla_a, gla_norm, w_gla_out, w_conv_in, conv_w, conv_b, w_conv_out):
    w = _prepare_weights(norm_mix, norm_mlp, norm_ple, w_mlp1, w_mlp2, w_ple_proj, w_ple_gate, w_pool, b_pool,
                         pool_scale, w_dsa_in, w_dsa_out, q_norm, k_norm, rel_bias, w_gla_in, w_gla_a2, b_gla_a,
                         gla_norm, w_gla_out, w_conv_in, conv_w, conv_b, w_conv_out)
    bp = x_prompt.shape[0]
    dt = x_prompt.dtype
    y_p, pool_p, k_p, v_p, ik_p, gla_p, conv_p = _trunk(
        x_prompt, p_prompt,
        jnp.zeros((bp, POOL_CTX, D_MODEL), dt),
        jnp.zeros((bp, 0, N_KV_HEADS, HEAD_DIM), dt),
        jnp.zeros((bp, 0, N_KV_HEADS, HEAD_DIM), dt),
        jnp.zeros((bp, 0, IDX_DIM), dt),
        jnp.zeros((bp, GLA_HEADS, GLA_DK, GLA_DV), dt),
        jnp.zeros((bp, CONV_W - 1, D_MODEL), dt),
        w)
    y_s, pool_s, k_s, v_s, ik_s, gla_s, conv_s = _trunk(
        x_sample, p_sample, state_pool, cache_k, cache_v, cache_idx_k, state_gla, state_conv, w)
    return (y_p, y_s, pool_p, pool_s, k_p, v_p, ik_p, k_s, v_s, ik_s, gla_p, gla_s, conv_p, conv_s)
```

```python
import functools

import jax
import jax.numpy as jnp
from jax import lax
from jax.experimental import pallas as pl
from jax.experimental.pallas import tpu as pltpu

F32 = jnp.float32
BF16 = jnp.bfloat16
I32 = jnp.int32

D_MODEL = 1024
D_FF = 4 * D_MODEL
PLE_DIM = 256
EPS = 1e-6
CHUNK = 64
CHUNK_SHIFT = 6
POOL_WINDOWS = (2, 4, 8, 16)
POOL_GC = D_MODEL // len(POOL_WINDOWS)
POOL_CTX = max(POOL_WINDOWS) - 1
POOL_ROW_BLOCK = 128
N_HEADS = 8
N_KV_HEADS = 2
HEAD_DIM = D_MODEL // N_HEADS
GROUP = N_HEADS // N_KV_HEADS
IDX_HEADS = 8
IDX_DIM = 64
TOPK_MAX = 256
REL_BUCKETS = 32
REL_LOG_EDGES = (12, 16, 23, 32, 46, 64, 91)
REL_FAR_BUCKET = REL_BUCKETS // 2 - 1
GLA_HEADS = 4
GLA_DK = D_MODEL // 2 // GLA_HEADS
GLA_DV = D_MODEL // GLA_HEADS
GLA_RANK = 16
GLA_GATE_NORM = 16.0
GLA_SUB = 16
CONV_W = 3

V7X_VMEM_LIMIT_BYTES = 56 * 1024 * 1024
SUBLANES = 8
INT_MIN = -(2 ** 31)
QUERY_BLOCK = 256
SHARED_QUERY_BLOCK = 128
KEY_TILE = 256

HIGHEST = lax.Precision.HIGHEST
LOG2_E = 1.4426950408889634


def _params(*sem):
    return pltpu.CompilerParams(dimension_semantics=sem, vmem_limit_bytes=V7X_VMEM_LIMIT_BYTES)


def _const_spec(shape):
    nd = len(shape)
    return pl.BlockSpec(shape, lambda *_: (0,) * nd, pipeline_mode=pl.Buffered(1))


def _rms(x, g):
    return x * lax.rsqrt(jnp.mean(x * x, axis=-1, keepdims=True) + EPS) * g


def _dot(a, b):
    return jnp.dot(a, b, preferred_element_type=F32)


def _dot_nt(a, b):
    return lax.dot_general(a, b, (((1,), (1,)), ((), ())), preferred_element_type=F32)


def _dot_tn(a, b):
    return lax.dot_general(a, b, (((0,), (0,)), ((), ())), preferred_element_type=F32)


def _fold_rows(x, op):
    n = x.shape[0] // 8
    assert n & (n - 1) == 0
    parts = x.reshape(n, 8, x.shape[1])
    while n > 1:
        n //= 2
        parts = op(parts[:n], parts[n:])
    return parts[0]


def _row_tile(n, want):
    t = min(n, want)
    assert n % t == 0
    return t


def _rows_per_step(b, t, tt, max_tokens):
    if t != tt:
        return 1
    return max(d for d in range(1, b + 1) if b % d == 0 and d * tt <= max(tt, max_tokens))


def _pool_step(x, g, prev_ref, row, xe_ref, is_first, pos_first):
    tt = x.shape[0]
    ctx = POOL_CTX + 1

    @pl.when(is_first)
    def _():
        xe_ref[0:1, :] = jnp.zeros((1, D_MODEL), F32)
        xe_ref[1:ctx, :] = prev_ref[row]

    xe_ref[ctx:ctx + tt, :] = _rms(x, g)
    rb = min(tt, POOL_ROW_BLOCK)
    kk = rb + ctx
    t_i = lax.broadcasted_iota(I32, (rb, kk), 0) + ctx
    c_i = lax.broadcasted_iota(I32, (rb, kk), 1)
    bands = [((c_i <= t_i) & (c_i > t_i - w)).astype(BF16) for w in POOL_WINDOWS]
    blocks = []
    for r in range(tt // rb):
        slab = xe_ref[r * rb:r * rb + kk, :]
        hi = slab.astype(BF16)
        lo = (slab - hi.astype(F32)).astype(BF16)
        pos1 = (pos_first + 1 + r * rb + lax.broadcasted_iota(I32, (rb, 1), 0)).astype(F32)
        parts = []
        for gi, w in enumerate(POOL_WINDOWS):
            cols = slice(gi * POOL_GC, (gi + 1) * POOL_GC)
            win = _dot(bands[gi], hi[:, cols]) + _dot(bands[gi], lo[:, cols])
            parts.append((win / jnp.minimum(float(w), pos1) - slab[ctx:, cols]).astype(BF16))
        blocks.append(jnp.concatenate(parts, axis=1))
    xe_ref[0:ctx, :] = xe_ref[tt:tt + ctx, :]
    return jnp.concatenate(blocks, axis=0)


def _post_kernel(h_ref, a_ref, p_ref, wout_ref, bout_ref, sout_ref, gmlp_ref, w1_ref, w2_ref, gple_ref, wg_ref,
                 wp_ref, *rest, ff_chunk, with_next):
    if with_next:
        gnext_ref, h_out_ref, xn_out_ref = rest
    else:
        (h_out_ref,) = rest
    y = (_dot(a_ref[...], wout_ref[...]) + bout_ref[...]) * sout_ref[...]
    h1 = h_ref[...] + y
    hn = _rms(h1, gmlp_ref[...]).astype(BF16)
    acc = h1
    for c in range(D_FF // ff_chunk):
        cols = slice(c * ff_chunk, (c + 1) * ff_chunk)
        hid = jnp.square(jnp.maximum(_dot(hn, w1_ref[:, cols]), 0.0)).astype(BF16)
        acc = acc + _dot(hid, w2_ref[cols, :])
    gate = jax.nn.sigmoid(_dot(_rms(acc, gple_ref[...]).astype(BF16), wg_ref[...]))
    h3 = acc + _dot(p_ref[...].astype(BF16), wp_ref[...]) * gate
    h_out_ref[...] = h3
    if with_next:
        xn_out_ref[...] = _rms(h3, gnext_ref[...]).astype(BF16)


def _layer_spec(shape, layer):
    nd = len(shape)
    return pl.BlockSpec((None,) + tuple(shape), lambda *_: (layer,) + (0,) * nd, pipeline_mode=pl.Buffered(1))


def _post_block(h, a, p, pw, layer, wout, bout, sout):
    n = h.shape[0]
    tm = _row_tile(n, 512)
    depth = pw["w1"].shape[0]
    with_next = layer + 1 < depth
    row = lambda w: pl.BlockSpec((tm, w), lambda i: (i, 0))
    vec = _const_spec((1, D_MODEL))
    lvec = lambda l: _layer_spec((1, D_MODEL), l)
    out_shape = [jax.ShapeDtypeStruct((n, D_MODEL), F32)]
    out_specs = [row(D_MODEL)]
    in_specs = [row(D_MODEL), row(D_MODEL), pl.BlockSpec((None, tm, PLE_DIM), lambda i: (layer, i, 0)),
                _const_spec((D_MODEL, D_MODEL)), vec, vec, lvec(layer),
                _layer_spec((D_MODEL, D_FF), layer), _layer_spec((D_FF, D_MODEL), layer), lvec(layer),
                _layer_spec((D_MODEL, D_MODEL), layer), _layer_spec((PLE_DIM, D_MODEL), layer)]
    args = [h, a, p, wout, bout, sout, pw["gmlp"], pw["w1"], pw["w2"], pw["gple"], pw["wg"], pw["wp"]]
    if with_next:
        out_shape.append(jax.ShapeDtypeStruct((n, D_MODEL), BF16))
        out_specs.append(row(D_MODEL))
        in_specs.append(lvec(layer + 1))
        args.append(pw["gmix"])
    outs = pl.pallas_call(
        functools.partial(_post_kernel, ff_chunk=1024, with_next=with_next),
        grid=(n // tm,),
        in_specs=in_specs,
        out_specs=out_specs,
        out_shape=out_shape,
        compiler_params=_params("parallel"),
        name="post_block",
    )(*args)
    return (outs[0], outs[1]) if with_next else (outs[0], None)


def _pool_kernel(x_ref, prev_ref, g_ref, a_ref, pool_ref, xe_ref, *, tt, pos0):
    t = pl.program_id(1)
    for row in range(x_ref.shape[0]):
        a_ref[row] = _pool_step(x_ref[row], g_ref[...], prev_ref, row, xe_ref, t == 0, pos0 + t * tt)

        @pl.when(t == pl.num_programs(1) - 1)
        def _():
            pool_ref[row] = xe_ref[1:POOL_CTX + 1, :]


def _pool_front(x, prev, g, pos0):
    b, t, _ = x.shape
    tt = _row_tile(t, 512)
    bb = _rows_per_step(b, t, tt, 256)
    return pl.pallas_call(
        functools.partial(_pool_kernel, tt=tt, pos0=pos0),
        grid=(b // bb, t // tt),
        in_specs=[pl.BlockSpec((bb, tt, D_MODEL), lambda i, j: (i, j, 0)),
                  pl.BlockSpec((bb, POOL_CTX, D_MODEL), lambda i, j: (i, 0, 0)),
                  _const_spec((1, D_MODEL))],
        out_specs=[pl.BlockSpec((bb, tt, D_MODEL), lambda i, j: (i, j, 0)),
                   pl.BlockSpec((bb, POOL_CTX, D_MODEL), lambda i, j: (i, 0, 0))],
        out_shape=[jax.ShapeDtypeStruct((b, t, D_MODEL), BF16),
                   jax.ShapeDtypeStruct((b, POOL_CTX, D_MODEL), F32)],
        scratch_shapes=[pltpu.VMEM((POOL_CTX + 1 + tt, D_MODEL), F32)],
        compiler_params=_params("arbitrary", "arbitrary"),
        name="pool_front",
    )(x, prev, g)


DSA_KV = N_KV_HEADS * HEAD_DIM
DSA_IQ = IDX_HEADS * IDX_DIM
DSA_COLS = (0, D_MODEL, D_MODEL + DSA_KV, D_MODEL + 2 * DSA_KV, D_MODEL + 2 * DSA_KV + DSA_IQ,
            D_MODEL + 2 * DSA_KV + DSA_IQ + IDX_DIM, D_MODEL + 2 * DSA_KV + DSA_IQ + IDX_DIM + IDX_HEADS)


def _dsa_in_kernel(xn_ref, w_ref, qg_ref, kg_ref, q_ref, iq_ref, iw_ref, k2_ref, v2_ref, ik_ref, *key_refs,
                   tm, for_attn):
    x = xn_ref[...]
    qb = QUERY_BLOCK
    assert tm % qb == 0 or not for_attn
    if for_attn:
        kb_ref, vt_ref, ikb_ref = key_refs
    col = lambda i: slice(DSA_COLS[i], DSA_COLS[i + 1])
    for h in range(N_HEADS):
        if h % 2 == 0:
            q2 = _dot(x, w_ref[:, DSA_COLS[0] + h * HEAD_DIM:DSA_COLS[0] + (h + 2) * HEAD_DIM])
            iq2 = _dot(x, w_ref[:, DSA_COLS[3] + h * IDX_DIM:DSA_COLS[3] + (h + 2) * IDX_DIM]).astype(BF16)
        qh = _rms(q2[:, (h % 2) * HEAD_DIM:(h % 2 + 1) * HEAD_DIM], qg_ref[...]).astype(BF16)
        iqh = iq2[:, (h % 2) * IDX_DIM:(h % 2 + 1) * IDX_DIM]
        if for_attn:
            for j in range(tm // qb):
                q_ref[j, h] = qh[j * qb:(j + 1) * qb]
                iq_ref[j, h] = iqh[j * qb:(j + 1) * qb]
        else:
            q_ref[:, h * HEAD_DIM:(h + 1) * HEAD_DIM] = qh
            iq_ref[:, h * IDX_DIM:(h + 1) * IDX_DIM] = iqh
    k = _dot(x, w_ref[:, col(1)])
    v = _dot(x, w_ref[:, col(2)])
    tail = _dot(x, w_ref[:, DSA_COLS[4]:DSA_COLS[6]])
    ik = tail[:, :IDX_DIM]
    ik_ref[...] = ik
    iw_ref[...] = tail[:, IDX_DIM:]
    for h in range(N_KV_HEADS):
        cols = slice(h * HEAD_DIM, (h + 1) * HEAD_DIM)
        kh = _rms(k[:, cols], kg_ref[...])
        k2_ref[pl.ds(h, tm, stride=N_KV_HEADS), :] = kh
        v2_ref[pl.ds(h, tm, stride=N_KV_HEADS), :] = v[:, cols]
        if for_attn:
            kb_ref[:, cols] = kh.astype(BF16)
            for j in range(tm // KEY_TILE):
                vt_ref[j, h] = v[j * KEY_TILE:(j + 1) * KEY_TILE, cols].T.astype(BF16)
    if for_attn:
        ikb_ref[...] = ik.astype(BF16)


def _dsa_in(xn, w, for_attn):
    n = xn.shape[0]
    tm = _row_tile(n, 512)
    qb = QUERY_BLOCK
    row = lambda wd: pl.BlockSpec((tm, wd), lambda i: (i, 0))
    lead = lambda shape: pl.BlockSpec(shape, lambda i: (i,) + (0,) * (len(shape) - 1))
    if for_attn:
        assert tm % KEY_TILE == 0 and tm % qb == 0
        q_specs = [lead((tm // qb, N_HEADS, qb, HEAD_DIM)), lead((tm // qb, IDX_HEADS, qb, IDX_DIM))]
        q_shapes = [jax.ShapeDtypeStruct((n // qb, N_HEADS, qb, HEAD_DIM), BF16),
                    jax.ShapeDtypeStruct((n // qb, IDX_HEADS, qb, IDX_DIM), BF16)]
        key_specs = [row(DSA_KV), lead((tm // KEY_TILE, N_KV_HEADS, HEAD_DIM, KEY_TILE)), row(IDX_DIM)]
        key_shapes = [jax.ShapeDtypeStruct((n, DSA_KV), BF16),
                      jax.ShapeDtypeStruct((n // KEY_TILE, N_KV_HEADS, HEAD_DIM, KEY_TILE), BF16),
                      jax.ShapeDtypeStruct((n, IDX_DIM), BF16)]
    else:
        q_specs = [row(D_MODEL), row(DSA_IQ)]
        q_shapes = [jax.ShapeDtypeStruct((n, D_MODEL), BF16), jax.ShapeDtypeStruct((n, DSA_IQ), BF16)]
        key_specs, key_shapes = [], []
    kv2 = pl.BlockSpec((N_KV_HEADS * tm, HEAD_DIM), lambda i: (i, 0))
    return pl.pallas_call(
        functools.partial(_dsa_in_kernel, tm=tm, for_attn=for_attn),
        grid=(n // tm,),
        in_specs=[row(D_MODEL), _const_spec((D_MODEL, DSA_COLS[-1])), _const_spec((1, HEAD_DIM)),
                  _const_spec((1, HEAD_DIM))],
        out_specs=q_specs + [row(IDX_HEADS), kv2, kv2, row(IDX_DIM)] + key_specs,
        out_shape=q_shapes + [jax.ShapeDtypeStruct((n, IDX_HEADS), F32),
                              jax.ShapeDtypeStruct((N_KV_HEADS * n, HEAD_DIM), F32),
                              jax.ShapeDtypeStruct((N_KV_HEADS * n, HEAD_DIM), F32),
                              jax.ShapeDtypeStruct((n, IDX_DIM), F32)] + key_shapes,
        compiler_params=_params("parallel"),
        name="dsa_in",
    )(xn, w["w_in"], w["qg"], w["kg"])


def _dsa_keys_kernel(ck_ref, cv_ref, k2_ref, v2_ref, kb_ref, vt_ref, *, past, t, nb):
    lp = kb_ref.shape[1]
    nh = N_KV_HEADS
    for r in range(nb):
        for h in range(nh):
            cols = slice((h * nb + r) * HEAD_DIM, (h * nb + r + 1) * HEAD_DIM)
            keys = slice(r * KEY_TILE, (r + 1) * KEY_TILE)
            kb_ref[0, 0:past, cols] = ck_ref[r, pl.ds(h, past, stride=nh), :].astype(BF16)
            kb_ref[0, past:past + t, cols] = k2_ref[r, pl.ds(h, t, stride=nh), :].astype(BF16)
            kb_ref[0, past + t:lp, cols] = jnp.zeros((lp - past - t, HEAD_DIM), BF16)
            for j in range(past // KEY_TILE):
                v_old = cv_ref[r, pl.ds(h + nh * KEY_TILE * j, KEY_TILE, stride=nh), :]
                vt_ref[0, j, h, :, keys] = v_old.T.astype(BF16)
            v_new = jnp.concatenate([v2_ref[r, pl.ds(h, t, stride=nh), :],
                                     jnp.zeros((KEY_TILE - t, HEAD_DIM), F32)], axis=0)
            vt_ref[0, past // KEY_TILE, h, :, keys] = v_new.T.astype(BF16)


def _dsa_keys(cache_k, cache_v, k2, v2, nb):
    b, past = cache_k.shape[:2]
    nh = N_KV_HEADS
    t = k2.shape[1] // nh
    lp = past + KEY_TILE
    assert past % KEY_TILE == 0 and t <= KEY_TILE and t % SUBLANES == 0 and b % nb == 0
    rows = lambda *shape: pl.BlockSpec((nb,) + shape, lambda i: (i,) + (0,) * len(shape))
    group = lambda *shape: pl.BlockSpec((1,) + shape, lambda i: (i,) + (0,) * len(shape))
    return pl.pallas_call(
        functools.partial(_dsa_keys_kernel, past=past, t=t, nb=nb),
        grid=(b // nb,),
        in_specs=[rows(nh * past, HEAD_DIM), rows(nh * past, HEAD_DIM), rows(nh * t, HEAD_DIM),
                  rows(nh * t, HEAD_DIM)],
        out_specs=[group(lp, nb * DSA_KV), group(lp // KEY_TILE, nh, HEAD_DIM, nb * KEY_TILE)],
        out_shape=[jax.ShapeDtypeStruct((b // nb, lp, nb * DSA_KV), BF16),
                   jax.ShapeDtypeStruct((b // nb, lp // KEY_TILE, nh, HEAD_DIM, nb * KEY_TILE), BF16)],
        compiler_params=_params("parallel"),
        name="dsa_keys",
    )(cache_k.reshape(b, nh * past, HEAD_DIM), cache_v.reshape(b, nh * past, HEAD_DIM), k2, v2)


def _dsa_attn_kernel(rb_ref, q_ref, iq_ref, iwt_ref, kb_ref, vt_ref, ik_ref, o_ref,
                     key_ref, msk_ref, lg_ref, bias_ref, j_ref, *, past, n_keys, topk, idx_bits, nb):
    kt_sz = KEY_TILE
    qb = o_ref.shape[1]
    near_slots = bias_ref.shape[1]
    qpb = qb // nb
    b = pl.program_id(0)
    i = pl.program_id(1)
    q0 = past + i * qpb
    kend = jnp.minimum(n_keys, (((q0 + qpb - 1) >> CHUNK_SHIFT) + 1) * CHUNK)
    nkt = (kend + kt_sz - 1) // kt_sz
    row_i = lax.broadcasted_iota(I32, (kt_sz, qb), 0)
    lane_i = lax.broadcasted_iota(I32, (kt_sz, qb), 1)
    col_i = lane_i % qpb

    @pl.when((b == 0) & (i == 0))
    def _():
        for w in range(1, near_slots):
            rel = row_i - col_i + (w - (near_slots - 1)) * qb
            n = jnp.abs(rel)
            log_bucket = jnp.full_like(n, REL_BUCKETS // 4)
            for edge in REL_LOG_EDGES:
                log_bucket = log_bucket + (n >= edge).astype(I32)
            bucket = jnp.where(n < REL_BUCKETS // 4, n, log_bucket) + jnp.where(rel > 0, REL_BUCKETS // 2, 0)
            for h in range(N_HEADS):
                val = jnp.zeros((kt_sz, qb), F32)
                for bk in range(REL_BUCKETS):
                    val = jnp.where(bucket == bk, rb_ref[bk, h], val)
                bias_ref[h, w] = val * LOG2_E
        for h in range(N_HEADS):
            bias_ref[h, 0] = jnp.full((kt_sz, qb), rb_ref[REL_FAR_BUCKET, h], F32) * LOG2_E

    iw = iwt_ref[0] * (IDX_DIM ** -0.5)
    iq_all = iq_ref[0, 0].reshape(IDX_HEADS * qb, nb * IDX_DIM)
    q_chunk = (q0 + col_i) >> CHUNK_SHIFT

    def tile_rows(jt):
        return pl.ds(pl.multiple_of(jt * kt_sz, kt_sz), kt_sz)

    def score_tile(jt, carry):
        ikt = ik_ref[0, tile_rows(jt), :]
        s = jnp.zeros((kt_sz, qb), F32)
        for hp in range(IDX_HEADS // 2):
            s2 = _dot_nt(ikt, iq_all[2 * hp * qb:(2 * hp + 2) * qb, :])
            for h in (2 * hp, 2 * hp + 1):
                s = s + iw[h:h + 1, :] * jnp.maximum(s2[:, (h - 2 * hp) * qb:(h - 2 * hp + 1) * qb], 0.0)
        bits = pltpu.bitcast(s, I32)
        key = bits ^ ((bits >> 31) & 0x7FFFFFFF)
        kpos = jt * kt_sz + row_i
        adm = ((kpos >> CHUNK_SHIFT) <= q_chunk) & (kpos < n_keys)
        key_ref[tile_rows(jt), :] = jnp.where(adm, key, INT_MIN)
        return carry

    lax.fori_loop(0, nkt, score_tile, 0)

    def count(pred):
        def body(jt, acc):
            hit = jnp.where(pred(key_ref[tile_rows(jt), :], jt), 1.0, 0.0)
            return acc + _fold_rows(hit, jnp.add)

        acc = lax.fori_loop(0, nkt, body, jnp.zeros((8, qb), F32))
        return jnp.sum(acc, axis=0, keepdims=True)

    kf = float(topk)
    n_pos = count(lambda kk, jt: kk >= 0)
    thr0 = jnp.where(n_pos >= kf, 0, INT_MIN).astype(I32)
    n_ge0 = n_pos

    def bisect(p, state):
        thr, n_ge = state
        cand = thr | jnp.left_shift(jnp.int32(1), 30 - p)
        n_cand = count(lambda kk, jt: kk >= cand)
        take = n_cand >= kf
        return jnp.where(take, cand, thr), jnp.where(take, n_cand, n_ge)

    thr, n_ge = lax.fori_loop(0, 31, bisect, (thr0, n_ge0))
    has_thr = thr > INT_MIN
    j_ref[...] = jnp.where(has_thr, n_keys, -1).astype(I32)
    tied_cut = jnp.max(jnp.where(has_thr & (n_ge > kf), 1, 0)) > 0

    @pl.when(tied_cut)
    def _():
        need = kf - count(lambda kk, jt: kk > thr)
        lo = jnp.zeros((1, qb), I32)
        for bit in range(idx_bits - 1, -1, -1):
            cand = lo + (1 << bit)
            below = count(lambda kk, jt: (kk == thr) & ((jt * kt_sz + row_i) < cand))
            lo = jnp.where(below < need, cand, lo)
        j_ref[...] = jnp.where(has_thr, lo, -1)

    j_last = j_ref[...]

    def mask_tile(jt, carry):
        kk = key_ref[tile_rows(jt), :]
        sel = (kk > thr) | ((kk == thr) & ((jt * kt_sz + row_i) <= j_last))
        msk_ref[tile_rows(jt), :] = jnp.where(sel, 0.0, -jnp.inf)
        return carry

    lax.fori_loop(0, nkt, mask_tile, 0)

    gq = GROUP * qb
    kdim = nb * HEAD_DIM
    slot0 = q0 // qb - (near_slots - 1)

    def bias_slot(jt):
        w = jt * (kt_sz // qb) - slot0
        return jnp.where(w >= 1, w, 0)

    kv_heads = range(N_KV_HEADS)
    q_groups = [q_ref[0, 0, kvh * GROUP:(kvh + 1) * GROUP].reshape(gq, kdim) for kvh in kv_heads]

    def logits_tile(jt, m8):
        msk = msk_ref[tile_rows(jt), :]
        w = bias_slot(jt)
        new_m8 = []
        for kvh in kv_heads:
            kt = kb_ref[0, tile_rows(jt), kvh * kdim:(kvh + 1) * kdim]
            lg = _dot_nt(kt, q_groups[kvh]) * (HEAD_DIM ** -0.5 * LOG2_E)
            parts = []
            for g in range(GROUP):
                h = kvh * GROUP + g
                lgh = lg[:, g * qb:(g + 1) * qb] + bias_ref[h, w] + msk
                lg_ref[tile_rows(jt), h * qb:(h + 1) * qb] = lgh
                parts.append(_fold_rows(lgh, jnp.maximum))
            new_m8.append(jnp.maximum(m8[kvh], jnp.concatenate(parts, axis=1)))
        return tuple(new_m8)

    m8 = lax.fori_loop(0, nkt, logits_tile, tuple(jnp.full((8, gq), -jnp.inf, F32) for _ in kv_heads))
    m = [jnp.max(m8[kvh], axis=0, keepdims=True) for kvh in kv_heads]

    lane_row_g = lax.broadcasted_iota(I32, (HEAD_DIM, gq), 1) % qb // qpb

    def pv_tile(jt, carry):
        new = []
        for kvh in kv_heads:
            acc, l8 = carry[kvh]
            p = jnp.exp2(lg_ref[tile_rows(jt), kvh * gq:(kvh + 1) * gq] - m[kvh])
            pb = p.astype(BF16)
            pv = _dot(vt_ref[0, jt, kvh, :, 0:kt_sz], pb)
            for r in range(1, nb):
                pv = jnp.where(lane_row_g == r, _dot(vt_ref[0, jt, kvh, :, r * kt_sz:(r + 1) * kt_sz], pb), pv)
            new.append((acc + pv, l8 + _fold_rows(p, jnp.add)))
        return tuple(new)

    zero = (jnp.zeros((HEAD_DIM, gq), F32), jnp.zeros((8, gq), F32))
    acc_l8 = lax.fori_loop(0, nkt, pv_tile, tuple(zero for _ in kv_heads))
    for kvh in kv_heads:
        acc, l8 = acc_l8[kvh]
        o_t = acc / jnp.sum(l8, axis=0, keepdims=True)
        for g in range(GROUP):
            h = kvh * GROUP + g
            o_ref[0, :, h * HEAD_DIM:(h + 1) * HEAD_DIM] = o_t[:, g * qb:(g + 1) * qb].T.astype(BF16)


def _dsa_attn(rel_bias, q, iq, iwt, kb, vt, ikb, *, past, n_keys):
    g, nblk = q.shape[:2]
    nb = q.shape[-1] // HEAD_DIM
    lp = kb.shape[1]
    qb = q.shape[3]
    near_slots = KEY_TILE // qb + 2
    assert lp % KEY_TILE == 0 and past % KEY_TILE == 0 and KEY_TILE % qb == 0 and qb >= REL_LOG_EDGES[-1]
    assert qb % nb == 0 and (nb == 1 or nblk == 1)
    topk = min(TOPK_MAX, n_keys // 4)
    return pl.pallas_call(
        functools.partial(_dsa_attn_kernel, past=past, n_keys=n_keys, topk=topk,
                          idx_bits=max(1, (lp - 1).bit_length()), nb=nb),
        grid=(g, nblk),
        in_specs=[pl.BlockSpec(memory_space=pltpu.SMEM),
                  pl.BlockSpec((1, 1, N_HEADS, qb, nb * HEAD_DIM), lambda i, j: (i, j, 0, 0, 0)),
                  pl.BlockSpec((1, 1, IDX_HEADS, qb, nb * IDX_DIM), lambda i, j: (i, j, 0, 0, 0)),
                  pl.BlockSpec((1, IDX_HEADS, qb), lambda i, j: (i, 0, j)),
                  pl.BlockSpec((1, lp, nb * DSA_KV), lambda i, j: (i, 0, 0)),
                  pl.BlockSpec((1, lp // KEY_TILE, N_KV_HEADS, HEAD_DIM, nb * KEY_TILE),
                               lambda i, j: (i, 0, 0, 0, 0)),
                  pl.BlockSpec((1, lp, nb * IDX_DIM), lambda i, j: (i, 0, 0))],
        out_specs=pl.BlockSpec((1, qb, D_MODEL), lambda i, j: (i, j, 0)),
        out_shape=jax.ShapeDtypeStruct((g, nblk * qb, D_MODEL), BF16),
        scratch_shapes=[pltpu.VMEM((lp, qb), I32), pltpu.VMEM((lp, qb), F32), pltpu.VMEM((lp, N_HEADS * qb), F32),
                        pltpu.VMEM((N_HEADS, near_slots, KEY_TILE, qb), F32), pltpu.VMEM((1, qb), I32)],
        compiler_params=_params("arbitrary", "arbitrary"),
        name="dsa_attn",
    )(rel_bias, q, iq, iwt, kb, vt, ikb)


def _dsa_mixer(xn, k_past, v_past, ik_past, w, b, t):
    past = k_past.shape[1]
    n_keys = past + t
    nh = N_KV_HEADS
    if past == 0:
        qb = QUERY_BLOCK
        assert t % KEY_TILE == 0 and t % qb == 0
        q, iq, iw, k2, v2, ik, kb, vt, ikb = _dsa_in(xn, w, for_attn=True)
        q = q.reshape(b, t // qb, N_HEADS, qb, HEAD_DIM)
        iq = iq.reshape(b, t // qb, IDX_HEADS, qb, IDX_DIM)
        iwt = iw.reshape(b, t, IDX_HEADS).transpose(0, 2, 1)
        kb = kb.reshape(b, t, DSA_KV)
        vt = vt.reshape(b, t // KEY_TILE, nh, HEAD_DIM, KEY_TILE)
        ikb = ikb.reshape(b, t, IDX_DIM)
    else:
        qb = SHARED_QUERY_BLOCK
        assert qb % t == 0 and b % (qb // t) == 0
        nb = qb // t
        g = b // nb
        q, iq, iw, k2, v2, ik = _dsa_in(xn, w, for_attn=False)
        eye = jnp.eye(nb, dtype=BF16)

        def own_slice(z, n_h):
            d = z.shape[-1] // n_h
            z = z.reshape(g, nb, t, n_h, d).transpose(0, 3, 1, 2, 4)
            z = z[:, :, :, :, None, :] * eye[None, None, :, None, :, None]
            return z.reshape(g, 1, n_h, qb, nb * d)

        q, iq = own_slice(q, N_HEADS), own_slice(iq, IDX_HEADS)
        iwt = iw.reshape(g, qb, IDX_HEADS).transpose(0, 2, 1)
        kb, vt = _dsa_keys(k_past, v_past, k2.reshape(b, nh * t, HEAD_DIM), v2.reshape(b, nh * t, HEAD_DIM), nb)
        lp = kb.shape[1]
        ikb = jnp.concatenate([ik_past.astype(BF16), ik.reshape(b, t, IDX_DIM).astype(BF16),
                               jnp.zeros((b, lp - n_keys, IDX_DIM), BF16)], axis=1)
        ikb = ikb.reshape(g, nb, lp, IDX_DIM).transpose(0, 2, 1, 3).reshape(g, lp, nb * IDX_DIM)
    o = _dsa_attn(w["rel_bias"], q, iq, iwt, kb, vt, ikb, past=past, n_keys=n_keys)
    o = o.reshape(b * t, D_MODEL)
    return (o, k2.reshape(b, t, nh, HEAD_DIM), v2.reshape(b, t, nh, HEAD_DIM), ik.reshape(b, t, IDX_DIM))


GLA_QK = GLA_HEADS * GLA_DK
GLA_COLS = (0, GLA_QK, 2 * GLA_QK, 2 * GLA_QK + D_MODEL, 2 * GLA_QK + 2 * D_MODEL, 2 * GLA_QK + 2 * D_MODEL + GLA_RANK)


def _gla_in_kernel(xn_ref, w_ref, wa2_ref, ba_ref, q_ref, k_ref, v_ref, r_ref, g_ref):
    x = xn_ref[...]
    proj = lambda i: _dot(x, w_ref[:, GLA_COLS[i]:GLA_COLS[i + 1]])
    q_ref[...] = proj(0) * (GLA_DK ** -0.5)
    k_ref[...] = proj(1)
    v_ref[...] = proj(2).astype(BF16)
    r_ref[...] = proj(3)
    z = _dot(proj(4).astype(BF16), wa2_ref[...]) + ba_ref[...]
    g_ref[...] = (jnp.minimum(z, 0.0) - jnp.log1p(jnp.exp(-jnp.abs(z)))) * (1.0 / GLA_GATE_NORM)


def _gla_in(xn, w):
    n = xn.shape[0]
    tm = _row_tile(n, 512)
    qk = GLA_HEADS * GLA_DK
    row = lambda wd: pl.BlockSpec((tm, wd), lambda i: (i, 0))
    return pl.pallas_call(
        _gla_in_kernel,
        grid=(n // tm,),
        in_specs=[row(D_MODEL), _const_spec((D_MODEL, GLA_COLS[-1])), _const_spec((GLA_RANK, qk)),
                  _const_spec((1, qk))],
        out_specs=[row(qk), row(qk), row(D_MODEL), row(D_MODEL), row(qk)],
        out_shape=[jax.ShapeDtypeStruct((n, qk), F32), jax.ShapeDtypeStruct((n, qk), F32),
                   jax.ShapeDtypeStruct((n, D_MODEL), BF16), jax.ShapeDtypeStruct((n, D_MODEL), F32),
                   jax.ShapeDtypeStruct((n, qk), F32)],
        compiler_params=_params("parallel"),
        name="gla_in",
    )(xn, w["w_in"], w["wa2"], w["ba"])


def _gla_chunk(q, k, g, v, st, tri_c, tri_sb):
    c = q.shape[0]
    sb = GLA_SUB
    cum = jnp.dot(tri_c, g, precision=HIGHEST, preferred_element_type=F32)
    total = cum[c - 1:c, :]
    o_inter = _dot_nt((q * jnp.exp(cum)).astype(BF16), st.astype(BF16))
    st_new = st * jnp.exp(total) + _dot_tn(v, (k * jnp.exp(total - cum)).astype(BF16))
    o_rows = []
    cum2 = cum * LOG2_E
    for i in range(c // sb):
        r = slice(i * sb, (i + 1) * sb)
        ci = cum2[r]
        decay = jnp.exp2(ci[:, None, :] - ci[None, :, :])
        sc = jnp.sum(decay * q[r][:, None, :] * k[r][None, :, :], axis=-1)
        sc = jnp.where(tri_sb, sc, 0.0)
        o_rows.append(o_inter[r] + _dot(sc.astype(BF16), v[r]))
    width = sb
    while width < c:
        for p in range(c // (2 * width)):
            lo = 2 * p * width
            left = slice(lo, lo + width)
            right = slice(lo + width, lo + 2 * width)
            edge = cum[lo + width - 1:lo + width, :]
            q_hat = (q[right] * jnp.exp(cum[right] - edge)).astype(BF16)
            k_hat = (k[left] * jnp.exp(edge - cum[left])).astype(BF16)
            o_pair = _dot(_dot_nt(q_hat, k_hat).astype(BF16), v[left])
            for j in range(width // sb):
                o_rows[(lo + width) // sb + j] += o_pair[j * sb:(j + 1) * sb]
        width *= 2
    return jnp.concatenate(o_rows, axis=0), st_new


def _gla_core_kernel(q_ref, k_ref, g_ref, v_ref, r_ref, s0_ref, gain_ref, a_ref, sout_ref, st_ref, *, tt, c):
    sb = GLA_SUB
    t = pl.program_id(1)
    bb = q_ref.shape[0]
    batch_rows = range(bb)

    @pl.when(t == 0)
    def _():
        for row in batch_rows:
            for h in range(GLA_HEADS):
                st_ref[row * GLA_HEADS + h] = s0_ref[row, h].T

    tri_c = (lax.broadcasted_iota(I32, (c, c), 0) >= lax.broadcasted_iota(I32, (c, c), 1)).astype(F32)
    tri_sb = lax.broadcasted_iota(I32, (sb, sb), 0) >= lax.broadcasted_iota(I32, (sb, sb), 1)
    gain = gain_ref[...]

    per_trip = 4 if (tt // c) % 4 == 0 else 1

    def chunks(ci, carry):
        for u in range(per_trip):
            rows = pl.ds(pl.multiple_of((ci * per_trip + u) * c, c), c)
            for row in batch_rows:
                for h in range(GLA_HEADS):
                    dk = slice(h * GLA_DK, (h + 1) * GLA_DK)
                    dv = slice(h * GLA_DV, (h + 1) * GLA_DV)
                    si = row * GLA_HEADS + h
                    o, st_ref[si] = _gla_chunk(q_ref[row, rows, dk], k_ref[row, rows, dk], g_ref[row, rows, dk],
                                               v_ref[row, rows, dv], st_ref[si], tri_c, tri_sb)
                    r = r_ref[row, rows, dv]
                    a_ref[row, rows, dv] = (_rms(o, gain) * (r * jax.nn.sigmoid(r))).astype(BF16)
        return carry

    lax.fori_loop(0, tt // (c * per_trip), chunks, 0)

    @pl.when(t == pl.num_programs(1) - 1)
    def _():
        for row in batch_rows:
            for h in range(GLA_HEADS):
                sout_ref[row, h] = st_ref[row * GLA_HEADS + h].T


def _gla_core(q, k, g, v, r, s0, gain):
    b, t, _ = q.shape
    tt = _row_tile(t, 256)
    c = min(tt, CHUNK)
    assert tt % c == 0 and c % GLA_SUB == 0 and (c // GLA_SUB) & (c // GLA_SUB - 1) == 0
    bb = _rows_per_step(b, t, tt, 128)
    qk = GLA_HEADS * GLA_DK
    tok = lambda wd: pl.BlockSpec((bb, tt, wd), lambda i, j: (i, j, 0))
    st_spec = pl.BlockSpec((bb, GLA_HEADS, GLA_DK, GLA_DV), lambda i, j: (i, 0, 0, 0))
    return pl.pallas_call(
        functools.partial(_gla_core_kernel, tt=tt, c=c),
        grid=(b // bb, t // tt),
        in_specs=[tok(qk), tok(qk), tok(qk), tok(D_MODEL), tok(D_MODEL), st_spec, _const_spec((1, GLA_DV))],
        out_specs=[tok(D_MODEL), st_spec],
        out_shape=[jax.ShapeDtypeStruct((b, t, D_MODEL), BF16),
                   jax.ShapeDtypeStruct((b, GLA_HEADS, GLA_DK, GLA_DV), F32)],
        scratch_shapes=[pltpu.VMEM((bb * GLA_HEADS, GLA_DV, GLA_DK), F32)],
        compiler_params=_params("arbitrary", "arbitrary"),
        name="gla_core",
    )(q, k, g, v, r, s0, gain)


def _conv_kernel(xn_ref, w_ref, prev_ref, cw_ref, cb_ref, a_ref, new_ref, ue_ref, *, tt):
    t = pl.program_id(1)
    pad = SUBLANES
    lo = pad - (CONV_W - 1)

    bb = xn_ref.shape[0]

    @pl.when(t == 0)
    def _():
        ue_ref[:, lo:pad, :] = prev_ref[...]

    x = xn_ref[...].reshape(bb * tt, D_MODEL)
    proj = lambda i: _dot(x, w_ref[:, i * D_MODEL:(i + 1) * D_MODEL]).reshape(bb, tt, D_MODEL)
    u = proj(1) * proj(2)
    ue_ref[:, pad:pad + tt, :] = u
    conv = cb_ref[...] + cw_ref[CONV_W - 1:CONV_W, :] * u
    for j in range(CONV_W - 1):
        conv = conv + cw_ref[j:j + 1, :] * ue_ref[:, lo + j:lo + j + tt, :]
    a_ref[...] = (proj(0) * conv).astype(BF16)
    tail = ue_ref[:, lo + tt:pad + tt, :]
    ue_ref[:, lo:pad, :] = tail

    @pl.when(t == pl.num_programs(1) - 1)
    def _():
        new_ref[...] = tail


def _conv_front(xn, prev, w):
    b, t, _ = xn.shape
    tt = _row_tile(t, 512)
    bb = _rows_per_step(b, t, tt, 512)
    tok = pl.BlockSpec((bb, tt, D_MODEL), lambda i, j: (i, j, 0))
    st = pl.BlockSpec((bb, CONV_W - 1, D_MODEL), lambda i, j: (i, 0, 0))
    w_spec = _const_spec((D_MODEL, 3 * D_MODEL))
    return pl.pallas_call(
        functools.partial(_conv_kernel, tt=tt),
        grid=(b // bb, t // tt),
        in_specs=[tok, w_spec, st, _const_spec((CONV_W, D_MODEL)), _const_spec((1, D_MODEL))],
        out_specs=[tok, st],
        out_shape=[jax.ShapeDtypeStruct((b, t, D_MODEL), BF16),
                   jax.ShapeDtypeStruct((b, CONV_W - 1, D_MODEL), F32)],
        scratch_shapes=[pltpu.VMEM((bb, SUBLANES + tt, D_MODEL), F32)],
        compiler_params=_params("arbitrary", "arbitrary"),
        name="conv_front",
    )(xn, w["w_in"], prev, w["cw"], w["cb"])


def _trunk(x, p, pool_prev, k_past, v_past, ik_past, gla_prev, conv_prev, w):
    b, t, _ = x.shape
    n = b * t
    past = k_past.shape[1]
    flat = lambda z: z.reshape(n, z.shape[-1])
    p = p.reshape(p.shape[0], n, PLE_DIM)
    post = lambda h, a, layer: _post_block(h, a, p, w["post"], layer, *w["mix_out"][layer])

    a, pool_new = _pool_front(x, pool_prev, w["gmix0"], past)
    h, xn = post(flat(x), flat(a), 0)

    a, k_new, v_new, ik_new = _dsa_mixer(xn, k_past, v_past, ik_past, w["dsa"], b, t)
    h, xn = post(h, a, 1)

    q, k, v, r, g = _gla_in(xn, w["gla"])
    seq = lambda z: z.reshape(b, t, z.shape[-1])
    a, gla_new = _gla_core(seq(q), seq(k), seq(g), seq(v), seq(r), gla_prev, w["gla"]["gain"])
    h, xn = post(h, flat(a), 2)

    a, conv_new = _conv_front(seq(xn), conv_prev, w["conv"])
    h, _ = post(h, flat(a), 3)
    return h.reshape(b, t, D_MODEL), pool_new, k_new, v_new, ik_new, gla_new, conv_new


def _prepare_weights(norm_mix, norm_mlp, norm_ple, w_mlp1, w_mlp2, w_ple_proj, w_ple_gate, w_pool, b_pool,
                     pool_scale, w_dsa_in, w_dsa_out, q_norm, k_norm, rel_bias, w_gla_in, w_gla_a2, b_gla_a,
                     gla_norm, w_gla_out, w_conv_in, conv_w, conv_b, w_conv_out):
    bf = lambda z: z.astype(BF16)
    vec = lambda z: z.reshape(1, -1).astype(F32)
    stack_vec = lambda z: z.reshape(z.shape[0], 1, z.shape[1]).astype(F32)
    w_pool_bd = jax.scipy.linalg.block_diag(*[w_pool[g] for g in range(w_pool.shape[0])])
    zeros = jnp.zeros((1, D_MODEL), F32)
    ones = jnp.ones((1, D_MODEL), F32)
    mix_out = [(bf(w_pool_bd), vec(b_pool), vec(pool_scale)), (bf(w_dsa_out), zeros, ones),
               (bf(w_gla_out), zeros, ones), (bf(w_conv_out), zeros, ones)]
    post = dict(gmix=stack_vec(norm_mix), gmlp=stack_vec(norm_mlp), gple=stack_vec(norm_ple), w1=bf(w_mlp1),
                w2=bf(w_mlp2), wg=bf(w_ple_gate), wp=bf(w_ple_proj))

    dsa = dict(w_in=bf(w_dsa_in), qg=vec(q_norm), kg=vec(k_norm), rel_bias=rel_bias.astype(F32))
    gla = dict(w_in=bf(w_gla_in), wa2=bf(w_gla_a2), ba=vec(b_gla_a), gain=vec(gla_norm))
    conv = dict(w_in=bf(w_conv_in), cw=conv_w.astype(F32), cb=vec(conv_b))
    return dict(post=post, mix_out=mix_out, gmix0=vec(norm_mix[0]), dsa=dsa, gla=gla, conv=conv)


def kernel(x_prompt, x_sample, p_prompt, p_sample, state_pool, cache_k, cache_v, cache_idx_k, state_gla, state_conv, norm_mix, norm_mlp, norm_ple, w_mlp1, w_mlp2, w_ple_proj, w_ple_gate, w_pool, b_pool, pool_scale, w_dsa_in, w_dsa_out, q_norm, k_norm, rel_bias, w_gla_in, w_gla_a2, b_gla_a, gla_norm, w_gla_out, w_conv_in, conv_w, conv_b, w_conv_out):
    w = _prepare_weights(norm_mix, norm_mlp, norm_ple, w_mlp1, w_mlp2, w_ple_proj, w_ple_gate, w_pool, b_pool,
                         pool_scale, w_dsa_in, w_dsa_out, q_norm, k_norm, rel_bias, w_gla_in, w_gla_a2, b_gla_a,
                         gla_norm, w_gla_out, w_conv_in, conv_w, conv_b, w_conv_out)
    bp = x_prompt.shape[0]
    dt = x_prompt.dtype
    y_p, pool_p, k_p, v_p, ik_p, gla_p, conv_p = _trunk(
        x_prompt, p_prompt,
        jnp.zeros((bp, POOL_CTX, D_MODEL), dt),
        jnp.zeros((bp, 0, N_KV_HEADS, HEAD_DIM), dt),
        jnp.zeros((bp, 0, N_KV_HEADS, HEAD_DIM), dt),
        jnp.zeros((bp, 0, IDX_DIM), dt),
        jnp.zeros((bp, GLA_HEADS, GLA_DK, GLA_DV), dt),
        jnp.zeros((bp, CONV_W - 1, D_MODEL), dt),
        w)
    y_s, pool_s, k_s, v_s, ik_s, gla_s, conv_s = _trunk(
        x_sample, p_sample, state_pool, cache_k, cache_v, cache_idx_k, state_gla, state_conv, w)
    return (y_p, y_s, pool_p, pool_s, k_p, v_p, ik_p, k_s, v_s, ik_s, gla_p, gla_s, conv_p, conv_s)
```

```python
import functools

import jax
import jax.numpy as jnp
from jax import lax
from jax.experimental import pallas as pl
from jax.experimental.pallas import tpu as pltpu

F32 = jnp.float32
BF16 = jnp.bfloat16
I32 = jnp.int32

D_MODEL = 1024
D_FF = 4 * D_MODEL
PLE_DIM = 256
EPS = 1e-6
CHUNK = 64
CHUNK_SHIFT = 6
POOL_WINDOWS = (2, 4, 8, 16)
POOL_GC = D_MODEL // len(POOL_WINDOWS)
POOL_CTX = max(POOL_WINDOWS) - 1
POOL_ROW_BLOCK = 128
N_HEADS = 8
N_KV_HEADS = 2
HEAD_DIM = D_MODEL // N_HEADS
GROUP = N_HEADS // N_KV_HEADS
IDX_HEADS = 8
IDX_DIM = 64
TOPK_MAX = 256
REL_BUCKETS = 32
REL_LOG_EDGES = (12, 16, 23, 32, 46, 64, 91)
REL_FAR_BUCKET = REL_BUCKETS // 2 - 1
GLA_HEADS = 4
GLA_DK = D_MODEL // 2 // GLA_HEADS
GLA_DV = D_MODEL // GLA_HEADS
GLA_RANK = 16
GLA_GATE_NORM = 16.0
GLA_SUB = 16
CONV_W = 3

V7X_VMEM_LIMIT_BYTES = 56 * 1024 * 1024
SUBLANES = 8
INT_MIN = -(2 ** 31)
QUERY_BLOCK = 256
SHARED_QUERY_BLOCK = 128
KEY_TILE = 256

HIGHEST = lax.Precision.HIGHEST
LOG2_E = 1.4426950408889634


def _params(*sem):
    return pltpu.CompilerParams(dimension_semantics=sem, vmem_limit_bytes=V7X_VMEM_LIMIT_BYTES)


def _const_spec(shape):
    nd = len(shape)
    return pl.BlockSpec(shape, lambda *_: (0,) * nd, pipeline_mode=pl.Buffered(1))


def _rms(x, g):
    return x * lax.rsqrt(jnp.mean(x * x, axis=-1, keepdims=True) + EPS) * g


def _dot(a, b):
    return jnp.dot(a, b, preferred_element_type=F32)


def _dot_nt(a, b):
    return lax.dot_general(a, b, (((1,), (1,)), ((), ())), preferred_element_type=F32)


def _dot_tn(a, b):
    return lax.dot_general(a, b, (((0,), (0,)), ((), ())), preferred_element_type=F32)


def _fold_rows(x, op):
    n = x.shape[0] // 8
    assert n & (n - 1) == 0
    parts = x.reshape(n, 8, x.shape[1])
    while n > 1:
        n //= 2
        parts = op(parts[:n], parts[n:])
    return parts[0]


def _row_tile(n, want):
    t = min(n, want)
    assert n % t == 0
    return t


def _rows_per_step(b, t, tt, max_tokens):
    if t != tt:
        return 1
    return max(d for d in range(1, b + 1) if b % d == 0 and d * tt <= max(tt, max_tokens))


def _pool_step(x, g, prev_ref, row, xe_ref, is_first, pos_first):
    tt = x.shape[0]
    ctx = POOL_CTX + 1

    @pl.when(is_first)
    def _():
        xe_ref[0:1, :] = jnp.zeros((1, D_MODEL), F32)
        xe_ref[1:ctx, :] = prev_ref[row]

    xe_ref[ctx:ctx + tt, :] = _rms(x, g)
    rb = min(tt, POOL_ROW_BLOCK)
    kk = rb + ctx
    t_i = lax.broadcasted_iota(I32, (rb, kk), 0) + ctx
    c_i = lax.broadcasted_iota(I32, (rb, kk), 1)
    bands = [((c_i <= t_i) & (c_i > t_i - w)).astype(BF16) for w in POOL_WINDOWS]
    blocks = []
    for r in range(tt // rb):
        slab = xe_ref[r * rb:r * rb + kk, :]
        hi = slab.astype(BF16)
        lo = (slab - hi.astype(F32)).astype(BF16)
        pos1 = (pos_first + 1 + r * rb + lax.broadcasted_iota(I32, (rb, 1), 0)).astype(F32)
        parts = []
        for gi, w in enumerate(POOL_WINDOWS):
            cols = slice(gi * POOL_GC, (gi + 1) * POOL_GC)
            win = _dot(bands[gi], hi[:, cols]) + _dot(bands[gi], lo[:, cols])
            parts.append((win / jnp.minimum(float(w), pos1) - slab[ctx:, cols]).astype(BF16))
        blocks.append(jnp.concatenate(parts, axis=1))
    xe_ref[0:ctx, :] = xe_ref[tt:tt + ctx, :]
    return jnp.concatenate(blocks, axis=0)


def _post_kernel(h_ref, a_ref, p_ref, wout_ref, bout_ref, sout_ref, gmlp_ref, w1_ref, w2_ref, gple_ref, wg_ref,
                 wp_ref, *rest, ff_chunk, with_next):
    if with_next:
        gnext_ref, h_out_ref, xn_out_ref = rest
    else:
        (h_out_ref,) = rest
    if len(wout_ref.shape) == 3:
        gc = wout_ref.shape[1]
        y = jnp.concatenate([_dot(a_ref[:, g * gc:(g + 1) * gc], wout_ref[g]) for g in range(wout_ref.shape[0])],
                            axis=1)
    else:
        y = _dot(a_ref[...], wout_ref[...])
    y = (y + bout_ref[...]) * sout_ref[...]
    h1 = h_ref[...] + y
    hn = _rms(h1, gmlp_ref[...]).astype(BF16)
    acc = h1
    for c in range(D_FF // ff_chunk):
        cols = slice(c * ff_chunk, (c + 1) * ff_chunk)
        hid = jnp.square(jnp.maximum(_dot(hn, w1_ref[:, cols]), 0.0)).astype(BF16)
        acc = acc + _dot(hid, w2_ref[cols, :])
    gate = jax.nn.sigmoid(_dot(_rms(acc, gple_ref[...]).astype(BF16), wg_ref[...]))
    h3 = acc + _dot(p_ref[...].astype(BF16), wp_ref[...]) * gate
    h_out_ref[...] = h3
    if with_next:
        xn_out_ref[...] = _rms(h3, gnext_ref[...]).astype(BF16)


def _layer_spec(shape, layer):
    nd = len(shape)
    return pl.BlockSpec((None,) + tuple(shape), lambda *_: (layer,) + (0,) * nd, pipeline_mode=pl.Buffered(1))


def _post_block(h, a, p, pw, layer, wout, bout, sout):
    n = h.shape[0]
    tm = _row_tile(n, 512)
    depth = pw["w1"].shape[0]
    with_next = layer + 1 < depth
    row = lambda w: pl.BlockSpec((tm, w), lambda i: (i, 0))
    vec = _const_spec((1, D_MODEL))
    lvec = lambda l: _layer_spec((1, D_MODEL), l)
    out_shape = [jax.ShapeDtypeStruct((n, D_MODEL), F32)]
    out_specs = [row(D_MODEL)]
    in_specs = [row(D_MODEL), row(D_MODEL), pl.BlockSpec((None, tm, PLE_DIM), lambda i: (layer, i, 0)),
                _const_spec(wout.shape), vec, vec, lvec(layer),
                _layer_spec((D_MODEL, D_FF), layer), _layer_spec((D_FF, D_MODEL), layer), lvec(layer),
                _layer_spec((D_MODEL, D_MODEL), layer), _layer_spec((PLE_DIM, D_MODEL), layer)]
    args = [h, a, p, wout, bout, sout, pw["gmlp"], pw["w1"], pw["w2"], pw["gple"], pw["wg"], pw["wp"]]
    if with_next:
        out_shape.append(jax.ShapeDtypeStruct((n, D_MODEL), BF16))
        out_specs.append(row(D_MODEL))
        in_specs.append(lvec(layer + 1))
        args.append(pw["gmix"])
    outs = pl.pallas_call(
        functools.partial(_post_kernel, ff_chunk=1024, with_next=with_next),
        grid=(n // tm,),
        in_specs=in_specs,
        out_specs=out_specs,
        out_shape=out_shape,
        compiler_params=_params("parallel"),
        name="post_block",
    )(*args)
    return (outs[0], outs[1]) if with_next else (outs[0], None)


def _pool_kernel(x_ref, prev_ref, g_ref, a_ref, pool_ref, xe_ref, *, tt, pos0):
    t = pl.program_id(1)
    for row in range(x_ref.shape[0]):
        a_ref[row] = _pool_step(x_ref[row], g_ref[...], prev_ref, row, xe_ref, t == 0, pos0 + t * tt)

        @pl.when(t == pl.num_programs(1) - 1)
        def _():
            pool_ref[row] = xe_ref[1:POOL_CTX + 1, :]


def _pool_front(x, prev, g, pos0):
    b, t, _ = x.shape
    tt = _row_tile(t, 512)
    bb = _rows_per_step(b, t, tt, 256)
    return pl.pallas_call(
        functools.partial(_pool_kernel, tt=tt, pos0=pos0),
        grid=(b // bb, t // tt),
        in_specs=[pl.BlockSpec((bb, tt, D_MODEL), lambda i, j: (i, j, 0)),
                  pl.BlockSpec((bb, POOL_CTX, D_MODEL), lambda i, j: (i, 0, 0)),
                  _const_spec((1, D_MODEL))],
        out_specs=[pl.BlockSpec((bb, tt, D_MODEL), lambda i, j: (i, j, 0)),
                   pl.BlockSpec((bb, POOL_CTX, D_MODEL), lambda i, j: (i, 0, 0))],
        out_shape=[jax.ShapeDtypeStruct((b, t, D_MODEL), BF16),
                   jax.ShapeDtypeStruct((b, POOL_CTX, D_MODEL), F32)],
        scratch_shapes=[pltpu.VMEM((POOL_CTX + 1 + tt, D_MODEL), F32)],
        compiler_params=_params("arbitrary", "arbitrary"),
        name="pool_front",
    )(x, prev, g)


DSA_KV = N_KV_HEADS * HEAD_DIM
DSA_IQ = IDX_HEADS * IDX_DIM
DSA_COLS = (0, D_MODEL, D_MODEL + DSA_KV, D_MODEL + 2 * DSA_KV, D_MODEL + 2 * DSA_KV + DSA_IQ,
            D_MODEL + 2 * DSA_KV + DSA_IQ + IDX_DIM, D_MODEL + 2 * DSA_KV + DSA_IQ + IDX_DIM + IDX_HEADS)


def _dsa_in_kernel(xn_ref, w_ref, qg_ref, kg_ref, q_ref, iq_ref, iw_ref, k2_ref, v2_ref, ik_ref, *key_refs,
                   tm, for_attn):
    x = xn_ref[...]
    qb = QUERY_BLOCK
    assert tm % qb == 0 or not for_attn
    if for_attn:
        kb_ref, vt_ref, ikb_ref = key_refs
    col = lambda i: slice(DSA_COLS[i], DSA_COLS[i + 1])
    for h in range(N_HEADS):
        if h % 2 == 0:
            q2 = _dot(x, w_ref[:, DSA_COLS[0] + h * HEAD_DIM:DSA_COLS[0] + (h + 2) * HEAD_DIM])
            iq2 = _dot(x, w_ref[:, DSA_COLS[3] + h * IDX_DIM:DSA_COLS[3] + (h + 2) * IDX_DIM]).astype(BF16)
        qh = _rms(q2[:, (h % 2) * HEAD_DIM:(h % 2 + 1) * HEAD_DIM], qg_ref[...]).astype(BF16)
        iqh = iq2[:, (h % 2) * IDX_DIM:(h % 2 + 1) * IDX_DIM]
        if for_attn:
            for j in range(tm // qb):
                q_ref[j, h] = qh[j * qb:(j + 1) * qb]
                iq_ref[j, h] = iqh[j * qb:(j + 1) * qb]
        else:
            q_ref[:, h * HEAD_DIM:(h + 1) * HEAD_DIM] = qh
            iq_ref[:, h * IDX_DIM:(h + 1) * IDX_DIM] = iqh
    k = _dot(x, w_ref[:, col(1)])
    v = _dot(x, w_ref[:, col(2)])
    tail = _dot(x, w_ref[:, DSA_COLS[4]:DSA_COLS[6]])
    ik = tail[:, :IDX_DIM]
    ik_ref[...] = ik
    iw_ref[...] = tail[:, IDX_DIM:]
    for h in range(N_KV_HEADS):
        cols = slice(h * HEAD_DIM, (h + 1) * HEAD_DIM)
        kh = _rms(k[:, cols], kg_ref[...])
        k2_ref[pl.ds(h, tm, stride=N_KV_HEADS), :] = kh
        v2_ref[pl.ds(h, tm, stride=N_KV_HEADS), :] = v[:, cols]
        if for_attn:
            kb_ref[:, cols] = kh.astype(BF16)
            for j in range(tm // KEY_TILE):
                vt_ref[j, h] = v[j * KEY_TILE:(j + 1) * KEY_TILE, cols].T.astype(BF16)
    if for_attn:
        ikb_ref[...] = ik.astype(BF16)


def _dsa_in(xn, w, for_attn):
    n = xn.shape[0]
    tm = _row_tile(n, 512)
    qb = QUERY_BLOCK
    row = lambda wd: pl.BlockSpec((tm, wd), lambda i: (i, 0))
    lead = lambda shape: pl.BlockSpec(shape, lambda i: (i,) + (0,) * (len(shape) - 1))
    if for_attn:
        assert tm % KEY_TILE == 0 and tm % qb == 0
        q_specs = [lead((tm // qb, N_HEADS, qb, HEAD_DIM)), lead((tm // qb, IDX_HEADS, qb, IDX_DIM))]
        q_shapes = [jax.ShapeDtypeStruct((n // qb, N_HEADS, qb, HEAD_DIM), BF16),
                    jax.ShapeDtypeStruct((n // qb, IDX_HEADS, qb, IDX_DIM), BF16)]
        key_specs = [row(DSA_KV), lead((tm // KEY_TILE, N_KV_HEADS, HEAD_DIM, KEY_TILE)), row(IDX_DIM)]
        key_shapes = [jax.ShapeDtypeStruct((n, DSA_KV), BF16),
                      jax.ShapeDtypeStruct((n // KEY_TILE, N_KV_HEADS, HEAD_DIM, KEY_TILE), BF16),
                      jax.ShapeDtypeStruct((n, IDX_DIM), BF16)]
    else:
        q_specs = [row(D_MODEL), row(DSA_IQ)]
        q_shapes = [jax.ShapeDtypeStruct((n, D_MODEL), BF16), jax.ShapeDtypeStruct((n, DSA_IQ), BF16)]
        key_specs, key_shapes = [], []
    kv2 = pl.BlockSpec((N_KV_HEADS * tm, HEAD_DIM), lambda i: (i, 0))
    return pl.pallas_call(
        functools.partial(_dsa_in_kernel, tm=tm, for_attn=for_attn),
        grid=(n // tm,),
        in_specs=[row(D_MODEL), _const_spec((D_MODEL, DSA_COLS[-1])), _const_spec((1, HEAD_DIM)),
                  _const_spec((1, HEAD_DIM))],
        out_specs=q_specs + [row(IDX_HEADS), kv2, kv2, row(IDX_DIM)] + key_specs,
        out_shape=q_shapes + [jax.ShapeDtypeStruct((n, IDX_HEADS), F32),
                              jax.ShapeDtypeStruct((N_KV_HEADS * n, HEAD_DIM), F32),
                              jax.ShapeDtypeStruct((N_KV_HEADS * n, HEAD_DIM), F32),
                              jax.ShapeDtypeStruct((n, IDX_DIM), F32)] + key_shapes,
        compiler_params=_params("parallel"),
        name="dsa_in",
    )(xn, w["w_in"], w["qg"], w["kg"])


def _dsa_keys_kernel(ck_ref, cv_ref, k2_ref, v2_ref, kb_ref, vt_ref, *, past, t, nb):
    lp = kb_ref.shape[1]
    nh = N_KV_HEADS
    for r in range(nb):
        for h in range(nh):
            cols = slice((h * nb + r) * HEAD_DIM, (h * nb + r + 1) * HEAD_DIM)
            keys = slice(r * KEY_TILE, (r + 1) * KEY_TILE)
            kb_ref[0, 0:past, cols] = ck_ref[r, pl.ds(h, past, stride=nh), :].astype(BF16)
            kb_ref[0, past:past + t, cols] = k2_ref[r, pl.ds(h, t, stride=nh), :].astype(BF16)
            kb_ref[0, past + t:lp, cols] = jnp.zeros((lp - past - t, HEAD_DIM), BF16)
            for j in range(past // KEY_TILE):
                v_old = cv_ref[r, pl.ds(h + nh * KEY_TILE * j, KEY_TILE, stride=nh), :]
                vt_ref[0, j, h, :, keys] = v_old.T.astype(BF16)
            v_new = jnp.concatenate([v2_ref[r, pl.ds(h, t, stride=nh), :],
                                     jnp.zeros((KEY_TILE - t, HEAD_DIM), F32)], axis=0)
            vt_ref[0, past // KEY_TILE, h, :, keys] = v_new.T.astype(BF16)


def _dsa_keys(cache_k, cache_v, k2, v2, nb):
    b, past = cache_k.shape[:2]
    nh = N_KV_HEADS
    t = k2.shape[1] // nh
    lp = past + KEY_TILE
    assert past % KEY_TILE == 0 and t <= KEY_TILE and t % SUBLANES == 0 and b % nb == 0
    rows = lambda *shape: pl.BlockSpec((nb,) + shape, lambda i: (i,) + (0,) * len(shape))
    group = lambda *shape: pl.BlockSpec((1,) + shape, lambda i: (i,) + (0,) * len(shape))
    return pl.pallas_call(
        functools.partial(_dsa_keys_kernel, past=past, t=t, nb=nb),
        grid=(b // nb,),
        in_specs=[rows(nh * past, HEAD_DIM), rows(nh * past, HEAD_DIM), rows(nh * t, HEAD_DIM),
                  rows(nh * t, HEAD_DIM)],
        out_specs=[group(lp, nb * DSA_KV), group(lp // KEY_TILE, nh, HEAD_DIM, nb * KEY_TILE)],
        out_shape=[jax.ShapeDtypeStruct((b // nb, lp, nb * DSA_KV), BF16),
                   jax.ShapeDtypeStruct((b // nb, lp // KEY_TILE, nh, HEAD_DIM, nb * KEY_TILE), BF16)],
        compiler_params=_params("parallel"),
        name="dsa_keys",
    )(cache_k.reshape(b, nh * past, HEAD_DIM), cache_v.reshape(b, nh * past, HEAD_DIM), k2, v2)


def _dsa_attn_kernel(rb_ref, q_ref, iq_ref, iwt_ref, kb_ref, vt_ref, ik_ref, o_ref,
                     key_ref, msk_ref, lg_ref, bias_ref, j_ref, *, past, n_keys, topk, idx_bits, nb):
    kt_sz = KEY_TILE
    qb = o_ref.shape[1]
    near_slots = bias_ref.shape[1]
    qpb = qb // nb
    b = pl.program_id(0)
    i = pl.program_id(1)
    q0 = past + i * qpb
    kend = jnp.minimum(n_keys, (((q0 + qpb - 1) >> CHUNK_SHIFT) + 1) * CHUNK)
    nkt = (kend + kt_sz - 1) // kt_sz
    row_i = lax.broadcasted_iota(I32, (kt_sz, qb), 0)
    lane_i = lax.broadcasted_iota(I32, (kt_sz, qb), 1)
    col_i = lane_i % qpb

    @pl.when((b == 0) & (i == 0))
    def _():
        for w in range(1, near_slots):
            rel = row_i - col_i + (w - (near_slots - 1)) * qb
            n = jnp.abs(rel)
            log_bucket = jnp.full_like(n, REL_BUCKETS // 4)
            for edge in REL_LOG_EDGES:
                log_bucket = log_bucket + (n >= edge).astype(I32)
            bucket = jnp.where(n < REL_BUCKETS // 4, n, log_bucket) + jnp.where(rel > 0, REL_BUCKETS // 2, 0)
            for h in range(N_HEADS):
                val = jnp.zeros((kt_sz, qb), F32)
                for bk in range(REL_BUCKETS):
                    val = jnp.where(bucket == bk, rb_ref[bk, h], val)
                bias_ref[h, w] = val * LOG2_E
        for h in range(N_HEADS):
            bias_ref[h, 0] = jnp.full((kt_sz, qb), rb_ref[REL_FAR_BUCKET, h], F32) * LOG2_E

    iw = iwt_ref[0] * (IDX_DIM ** -0.5)
    iq_all = iq_ref[0, 0].reshape(IDX_HEADS * qb, nb * IDX_DIM)
    q_chunk = (q0 + col_i) >> CHUNK_SHIFT

    def tile_rows(jt):
        return pl.ds(pl.multiple_of(jt * kt_sz, kt_sz), kt_sz)

    def score_tile(jt, carry):
        ikt = ik_ref[0, tile_rows(jt), :]
        s = jnp.zeros((kt_sz, qb), F32)
        for hp in range(IDX_HEADS // 2):
            s2 = _dot_nt(ikt, iq_all[2 * hp * qb:(2 * hp + 2) * qb, :])
            for h in (2 * hp, 2 * hp + 1):
                s = s + iw[h:h + 1, :] * jnp.maximum(s2[:, (h - 2 * hp) * qb:(h - 2 * hp + 1) * qb], 0.0)
        bits = pltpu.bitcast(s, I32)
        key = bits ^ ((bits >> 31) & 0x7FFFFFFF)
        kpos = jt * kt_sz + row_i
        adm = ((kpos >> CHUNK_SHIFT) <= q_chunk) & (kpos < n_keys)
        key_ref[tile_rows(jt), :] = jnp.where(adm, key, INT_MIN)
        return carry

    lax.fori_loop(0, nkt, score_tile, 0)

    def count(pred):
        def body(jt, acc):
            hit = jnp.where(pred(key_ref[tile_rows(jt), :], jt), 1.0, 0.0)
            return acc + _fold_rows(hit, jnp.add)

        acc = lax.fori_loop(0, nkt, body, jnp.zeros((8, qb), F32))
        return jnp.sum(acc, axis=0, keepdims=True)

    kf = float(topk)
    n_pos = count(lambda kk, jt: kk >= 0)
    thr0 = jnp.where(n_pos >= kf, 0, INT_MIN).astype(I32)
    n_ge0 = n_pos

    def bisect(p, state):
        thr, n_ge = state
        cand = thr | jnp.left_shift(jnp.int32(1), 30 - p)
        n_cand = count(lambda kk, jt: kk >= cand)
        take = n_cand >= kf
        return jnp.where(take, cand, thr), jnp.where(take, n_cand, n_ge)

    thr, n_ge = lax.fori_loop(0, 31, bisect, (thr0, n_ge0))
    has_thr = thr > INT_MIN
    j_ref[...] = jnp.where(has_thr, n_keys, -1).astype(I32)
    tied_cut = jnp.max(jnp.where(has_thr & (n_ge > kf), 1, 0)) > 0

    @pl.when(tied_cut)
    def _():
        need = kf - count(lambda kk, jt: kk > thr)
        lo = jnp.zeros((1, qb), I32)
        for bit in range(idx_bits - 1, -1, -1):
            cand = lo + (1 << bit)
            below = count(lambda kk, jt: (kk == thr) & ((jt * kt_sz + row_i) < cand))
            lo = jnp.where(below < need, cand, lo)
        j_ref[...] = jnp.where(has_thr, lo, -1)

    j_last = j_ref[...]

    def mask_tile(jt, carry):
        kk = key_ref[tile_rows(jt), :]
        sel = (kk > thr) | ((kk == thr) & ((jt * kt_sz + row_i) <= j_last))
        msk_ref[tile_rows(jt), :] = jnp.where(sel, 0.0, -jnp.inf)
        return carry

    lax.fori_loop(0, nkt, mask_tile, 0)

    gq = GROUP * qb
    kdim = nb * HEAD_DIM
    slot0 = q0 // qb - (near_slots - 1)

    def bias_slot(jt):
        w = jt * (kt_sz // qb) - slot0
        return jnp.where(w >= 1, w, 0)

    kv_heads = range(N_KV_HEADS)
    q_groups = [q_ref[0, 0, kvh * GROUP:(kvh + 1) * GROUP].reshape(gq, kdim) for kvh in kv_heads]

    def logits_tile(jt, m8):
        msk = msk_ref[tile_rows(jt), :]
        w = bias_slot(jt)
        new_m8 = []
        for kvh in kv_heads:
            kt = kb_ref[0, tile_rows(jt), kvh * kdim:(kvh + 1) * kdim]
            lg = _dot_nt(kt, q_groups[kvh]) * (HEAD_DIM ** -0.5 * LOG2_E)
            parts = []
            for g in range(GROUP):
                h = kvh * GROUP + g
                lgh = lg[:, g * qb:(g + 1) * qb] + bias_ref[h, w] + msk
                lg_ref[tile_rows(jt), h * qb:(h + 1) * qb] = lgh
                parts.append(_fold_rows(lgh, jnp.maximum))
            new_m8.append(jnp.maximum(m8[kvh], jnp.concatenate(parts, axis=1)))
        return tuple(new_m8)

    m8 = lax.fori_loop(0, nkt, logits_tile, tuple(jnp.full((8, gq), -jnp.inf, F32) for _ in kv_heads))
    m = [jnp.max(m8[kvh], axis=0, keepdims=True) for kvh in kv_heads]

    lane_row_g = lax.broadcasted_iota(I32, (HEAD_DIM, gq), 1) % qb // qpb

    def pv_tiles(jt, n_tiles, carry):
        rows = pl.ds(pl.multiple_of(jt * kt_sz, kt_sz), n_tiles * kt_sz)
        new = []
        for kvh in kv_heads:
            acc, l8 = carry[kvh]
            p = jnp.exp2(lg_ref[rows, kvh * gq:(kvh + 1) * gq] - m[kvh])
            pb = p.astype(BF16)

            def values(r):
                tiles = [vt_ref[0, jt + u, kvh, :, r * kt_sz:(r + 1) * kt_sz] for u in range(n_tiles)]
                return tiles[0] if n_tiles == 1 else jnp.concatenate(tiles, axis=1)

            pv = _dot(values(0), pb)
            for r in range(1, nb):
                pv = jnp.where(lane_row_g == r, _dot(values(r), pb), pv)
            new.append((acc + pv, l8 + _fold_rows(p, jnp.add)))
        return tuple(new)

    zero = (jnp.zeros((HEAD_DIM, gq), F32), jnp.zeros((8, gq), F32))
    acc_l8 = lax.fori_loop(0, nkt // 2, lambda j2, carry: pv_tiles(2 * j2, 2, carry), tuple(zero for _ in kv_heads))
    acc_l8 = lax.cond(nkt % 2 == 1, lambda carry: pv_tiles(nkt - 1, 1, carry), lambda carry: carry, acc_l8)
    for kvh in kv_heads:
        acc, l8 = acc_l8[kvh]
        o_t = acc / jnp.sum(l8, axis=0, keepdims=True)
        for g in range(GROUP):
            h = kvh * GROUP + g
            o_ref[0, :, h * HEAD_DIM:(h + 1) * HEAD_DIM] = o_t[:, g * qb:(g + 1) * qb].T.astype(BF16)


def _dsa_attn(rel_bias, q, iq, iwt, kb, vt, ikb, *, past, n_keys):
    g, nblk = q.shape[:2]
    nb = q.shape[-1] // HEAD_DIM
    lp = kb.shape[1]
    qb = q.shape[3]
    near_slots = KEY_TILE // qb + 2
    assert lp % KEY_TILE == 0 and past % KEY_TILE == 0 and KEY_TILE % qb == 0 and qb >= REL_LOG_EDGES[-1]
    assert qb % nb == 0 and (nb == 1 or nblk == 1)
    topk = min(TOPK_MAX, n_keys // 4)
    return pl.pallas_call(
        functools.partial(_dsa_attn_kernel, past=past, n_keys=n_keys, topk=topk,
                          idx_bits=max(1, (lp - 1).bit_length()), nb=nb),
        grid=(g, nblk),
        in_specs=[pl.BlockSpec(memory_space=pltpu.SMEM),
                  pl.BlockSpec((1, 1, N_HEADS, qb, nb * HEAD_DIM), lambda i, j: (i, j, 0, 0, 0)),
                  pl.BlockSpec((1, 1, IDX_HEADS, qb, nb * IDX_DIM), lambda i, j: (i, j, 0, 0, 0)),
                  pl.BlockSpec((1, IDX_HEADS, qb), lambda i, j: (i, 0, j)),
                  pl.BlockSpec((1, lp, nb * DSA_KV), lambda i, j: (i, 0, 0)),
                  pl.BlockSpec((1, lp // KEY_TILE, N_KV_HEADS, HEAD_DIM, nb * KEY_TILE),
                               lambda i, j: (i, 0, 0, 0, 0)),
                  pl.BlockSpec((1, lp, nb * IDX_DIM), lambda i, j: (i, 0, 0))],
        out_specs=pl.BlockSpec((1, qb, D_MODEL), lambda i, j: (i, j, 0)),
        out_shape=jax.ShapeDtypeStruct((g, nblk * qb, D_MODEL), BF16),
        scratch_shapes=[pltpu.VMEM((lp, qb), I32), pltpu.VMEM((lp, qb), F32), pltpu.VMEM((lp, N_HEADS * qb), F32),
                        pltpu.VMEM((N_HEADS, near_slots, KEY_TILE, qb), F32), pltpu.VMEM((1, qb), I32)],
        compiler_params=_params("arbitrary", "arbitrary"),
        name="dsa_attn",
    )(rel_bias, q, iq, iwt, kb, vt, ikb)


def _dsa_mixer(xn, k_past, v_past, ik_past, w, b, t):
    past = k_past.shape[1]
    n_keys = past + t
    nh = N_KV_HEADS
    if past == 0:
        qb = QUERY_BLOCK
        assert t % KEY_TILE == 0 and t % qb == 0
        q, iq, iw, k2, v2, ik, kb, vt, ikb = _dsa_in(xn, w, for_attn=True)
        q = q.reshape(b, t // qb, N_HEADS, qb, HEAD_DIM)
        iq = iq.reshape(b, t // qb, IDX_HEADS, qb, IDX_DIM)
        iwt = iw.reshape(b, t, IDX_HEADS).transpose(0, 2, 1)
        kb = kb.reshape(b, t, DSA_KV)
        vt = vt.reshape(b, t // KEY_TILE, nh, HEAD_DIM, KEY_TILE)
        ikb = ikb.reshape(b, t, IDX_DIM)
    else:
        qb = SHARED_QUERY_BLOCK
        assert qb % t == 0 and b % (qb // t) == 0
        nb = qb // t
        g = b // nb
        q, iq, iw, k2, v2, ik = _dsa_in(xn, w, for_attn=False)
        eye = jnp.eye(nb, dtype=BF16)

        def own_slice(z, n_h):
            d = z.shape[-1] // n_h
            z = z.reshape(g, nb, t, n_h, d).transpose(0, 3, 1, 2, 4)
            z = z[:, :, :, :, None, :] * eye[None, None, :, None, :, None]
            return z.reshape(g, 1, n_h, qb, nb * d)

        q, iq = own_slice(q, N_HEADS), own_slice(iq, IDX_HEADS)
        iwt = iw.reshape(g, qb, IDX_HEADS).transpose(0, 2, 1)
        kb, vt = _dsa_keys(k_past, v_past, k2.reshape(b, nh * t, HEAD_DIM), v2.reshape(b, nh * t, HEAD_DIM), nb)
        lp = kb.shape[1]
        ikb = jnp.concatenate([ik_past.astype(BF16), ik.reshape(b, t, IDX_DIM).astype(BF16),
                               jnp.zeros((b, lp - n_keys, IDX_DIM), BF16)], axis=1)
        ikb = ikb.reshape(g, nb, lp, IDX_DIM).transpose(0, 2, 1, 3).reshape(g, lp, nb * IDX_DIM)
    o = _dsa_attn(w["rel_bias"], q, iq, iwt, kb, vt, ikb, past=past, n_keys=n_keys)
    o = o.reshape(b * t, D_MODEL)
    return (o, k2.reshape(b, t, nh, HEAD_DIM), v2.reshape(b, t, nh, HEAD_DIM), ik.reshape(b, t, IDX_DIM))


GLA_QK = GLA_HEADS * GLA_DK
GLA_COLS = (0, GLA_QK, 2 * GLA_QK, 2 * GLA_QK + D_MODEL, 2 * GLA_QK + 2 * D_MODEL, 2 * GLA_QK + 2 * D_MODEL + GLA_RANK)


def _gla_in_kernel(xn_ref, w_ref, wa2_ref, ba_ref, q_ref, k_ref, v_ref, r_ref, g_ref):
    x = xn_ref[...]
    proj = lambda i: _dot(x, w_ref[:, GLA_COLS[i]:GLA_COLS[i + 1]])
    q_ref[...] = proj(0) * (GLA_DK ** -0.5)
    k_ref[...] = proj(1)
    v_ref[...] = proj(2).astype(BF16)
    r_ref[...] = proj(3)
    z = _dot(proj(4).astype(BF16), wa2_ref[...]) + ba_ref[...]
    g_ref[...] = (jnp.minimum(z, 0.0) - jnp.log1p(jnp.exp(-jnp.abs(z)))) * (1.0 / GLA_GATE_NORM)


def _gla_in(xn, w):
    n = xn.shape[0]
    tm = _row_tile(n, 512)
    qk = GLA_HEADS * GLA_DK
    row = lambda wd: pl.BlockSpec((tm, wd), lambda i: (i, 0))
    return pl.pallas_call(
        _gla_in_kernel,
        grid=(n // tm,),
        in_specs=[row(D_MODEL), _const_spec((D_MODEL, GLA_COLS[-1])), _const_spec((GLA_RANK, qk)),
                  _const_spec((1, qk))],
        out_specs=[row(qk), row(qk), row(D_MODEL), row(D_MODEL), row(qk)],
        out_shape=[jax.ShapeDtypeStruct((n, qk), F32), jax.ShapeDtypeStruct((n, qk), F32),
                   jax.ShapeDtypeStruct((n, D_MODEL), BF16), jax.ShapeDtypeStruct((n, D_MODEL), F32),
                   jax.ShapeDtypeStruct((n, qk), F32)],
        compiler_params=_params("parallel"),
        name="gla_in",
    )(xn, w["w_in"], w["wa2"], w["ba"])


def _gla_chunk(q, k, g, v, st, tri_c, tri_sb):
    c = q.shape[0]
    sb = GLA_SUB
    cum = jnp.dot(tri_c, g, precision=HIGHEST, preferred_element_type=F32)
    total = cum[c - 1:c, :]
    o_inter = _dot_nt((q * jnp.exp(cum)).astype(BF16), st.astype(BF16))
    st_new = st * jnp.exp(total) + _dot_tn(v, (k * jnp.exp(total - cum)).astype(BF16))
    o_rows = []
    cum2 = cum * LOG2_E
    for i in range(c // sb):
        r = slice(i * sb, (i + 1) * sb)
        ci = cum2[r]
        decay = jnp.exp2(ci[:, None, :] - ci[None, :, :])
        sc = jnp.sum(decay * q[r][:, None, :] * k[r][None, :, :], axis=-1)
        sc = jnp.where(tri_sb, sc, 0.0)
        o_rows.append(o_inter[r] + _dot(sc.astype(BF16), v[r]))
    width = sb
    while width < c:
        for p in range(c // (2 * width)):
            lo = 2 * p * width
            left = slice(lo, lo + width)
            right = slice(lo + width, lo + 2 * width)
            edge = cum[lo + width - 1:lo + width, :]
            q_hat = (q[right] * jnp.exp(cum[right] - edge)).astype(BF16)
            k_hat = (k[left] * jnp.exp(edge - cum[left])).astype(BF16)
            o_pair = _dot(_dot_nt(q_hat, k_hat).astype(BF16), v[left])
            for j in range(width // sb):
                o_rows[(lo + width) // sb + j] += o_pair[j * sb:(j + 1) * sb]
        width *= 2
    return jnp.concatenate(o_rows, axis=0), st_new


def _gla_core_kernel(q_ref, k_ref, g_ref, v_ref, r_ref, s0_ref, gain_ref, a_ref, sout_ref, st_ref, *, tt, c):
    sb = GLA_SUB
    t = pl.program_id(1)
    bb = q_ref.shape[0]
    batch_rows = range(bb)

    @pl.when(t == 0)
    def _():
        for row in batch_rows:
            for h in range(GLA_HEADS):
                st_ref[row * GLA_HEADS + h] = s0_ref[row, h].T

    tri_c = (lax.broadcasted_iota(I32, (c, c), 0) >= lax.broadcasted_iota(I32, (c, c), 1)).astype(F32)
    tri_sb = lax.broadcasted_iota(I32, (sb, sb), 0) >= lax.broadcasted_iota(I32, (sb, sb), 1)
    gain = gain_ref[...]

    per_trip = 4 if (tt // c) % 4 == 0 else 1

    def chunks(ci, carry):
        for u in range(per_trip):
            rows = pl.ds(pl.multiple_of((ci * per_trip + u) * c, c), c)
            for row in batch_rows:
                for h in range(GLA_HEADS):
                    dk = slice(h * GLA_DK, (h + 1) * GLA_DK)
                    dv = slice(h * GLA_DV, (h + 1) * GLA_DV)
                    si = row * GLA_HEADS + h
                    o, st_ref[si] = _gla_chunk(q_ref[row, rows, dk], k_ref[row, rows, dk], g_ref[row, rows, dk],
                                               v_ref[row, rows, dv], st_ref[si], tri_c, tri_sb)
                    r = r_ref[row, rows, dv]
                    a_ref[row, rows, dv] = (_rms(o, gain) * (r * jax.nn.sigmoid(r))).astype(BF16)
        return carry

    lax.fori_loop(0, tt // (c * per_trip), chunks, 0)

    @pl.when(t == pl.num_programs(1) - 1)
    def _():
        for row in batch_rows:
            for h in range(GLA_HEADS):
                sout_ref[row, h] = st_ref[row * GLA_HEADS + h].T


def _gla_core(q, k, g, v, r, s0, gain):
    b, t, _ = q.shape
    tt = _row_tile(t, 256)
    c = min(tt, CHUNK)
    assert tt % c == 0 and c % GLA_SUB == 0 and (c // GLA_SUB) & (c // GLA_SUB - 1) == 0
    bb = _rows_per_step(b, t, tt, 128)
    qk = GLA_HEADS * GLA_DK
    tok = lambda wd: pl.BlockSpec((bb, tt, wd), lambda i, j: (i, j, 0))
    st_spec = pl.BlockSpec((bb, GLA_HEADS, GLA_DK, GLA_DV), lambda i, j: (i, 0, 0, 0))
    return pl.pallas_call(
        functools.partial(_gla_core_kernel, tt=tt, c=c),
        grid=(b // bb, t // tt),
        in_specs=[tok(qk), tok(qk), tok(qk), tok(D_MODEL), tok(D_MODEL), st_spec, _const_spec((1, GLA_DV))],
        out_specs=[tok(D_MODEL), st_spec],
        out_shape=[jax.ShapeDtypeStruct((b, t, D_MODEL), BF16),
                   jax.ShapeDtypeStruct((b, GLA_HEADS, GLA_DK, GLA_DV), F32)],
        scratch_shapes=[pltpu.VMEM((bb * GLA_HEADS, GLA_DV, GLA_DK), F32)],
        compiler_params=_params("arbitrary", "arbitrary"),
        name="gla_core",
    )(q, k, g, v, r, s0, gain)


def _conv_kernel(xn_ref, w_ref, prev_ref, cw_ref, cb_ref, a_ref, new_ref, ue_ref, *, tt):
    t = pl.program_id(1)
    pad = SUBLANES
    lo = pad - (CONV_W - 1)

    bb = xn_ref.shape[0]

    @pl.when(t == 0)
    def _():
        ue_ref[:, lo:pad, :] = prev_ref[...]

    x = xn_ref[...].reshape(bb * tt, D_MODEL)
    proj = lambda i: _dot(x, w_ref[:, i * D_MODEL:(i + 1) * D_MODEL]).reshape(bb, tt, D_MODEL)
    u = proj(1) * proj(2)
    ue_ref[:, pad:pad + tt, :] = u
    conv = cb_ref[...] + cw_ref[CONV_W - 1:CONV_W, :] * u
    for j in range(CONV_W - 1):
        conv = conv + cw_ref[j:j + 1, :] * ue_ref[:, lo + j:lo + j + tt, :]
    a_ref[...] = (proj(0) * conv).astype(BF16)
    tail = ue_ref[:, lo + tt:pad + tt, :]
    ue_ref[:, lo:pad, :] = tail

    @pl.when(t == pl.num_programs(1) - 1)
    def _():
        new_ref[...] = tail


def _conv_front(xn, prev, w):
    b, t, _ = xn.shape
    tt = _row_tile(t, 512)
    bb = _rows_per_step(b, t, tt, 512)
    tok = pl.BlockSpec((bb, tt, D_MODEL), lambda i, j: (i, j, 0))
    st = pl.BlockSpec((bb, CONV_W - 1, D_MODEL), lambda i, j: (i, 0, 0))
    w_spec = _const_spec((D_MODEL, 3 * D_MODEL))
    return pl.pallas_call(
        functools.partial(_conv_kernel, tt=tt),
        grid=(b // bb, t // tt),
        in_specs=[tok, w_spec, st, _const_spec((CONV_W, D_MODEL)), _const_spec((1, D_MODEL))],
        out_specs=[tok, st],
        out_shape=[jax.ShapeDtypeStruct((b, t, D_MODEL), BF16),
                   jax.ShapeDtypeStruct((b, CONV_W - 1, D_MODEL), F32)],
        scratch_shapes=[pltpu.VMEM((bb, SUBLANES + tt, D_MODEL), F32)],
        compiler_params=_params("arbitrary", "arbitrary"),
        name="conv_front",
    )(xn, w["w_in"], prev, w["cw"], w["cb"])


def _trunk(x, p, pool_prev, k_past, v_past, ik_past, gla_prev, conv_prev, w):
    b, t, _ = x.shape
    n = b * t
    past = k_past.shape[1]
    flat = lambda z: z.reshape(n, z.shape[-1])
    p = p.reshape(p.shape[0], n, PLE_DIM)
    post = lambda h, a, layer: _post_block(h, a, p, w["post"], layer, *w["mix_out"][layer])

    a, pool_new = _pool_front(x, pool_prev, w["gmix0"], past)
    h, xn = post(flat(x), flat(a), 0)

    a, k_new, v_new, ik_new = _dsa_mixer(xn, k_past, v_past, ik_past, w["dsa"], b, t)
    h, xn = post(h, a, 1)

    q, k, v, r, g = _gla_in(xn, w["gla"])
    seq = lambda z: z.reshape(b, t, z.shape[-1])
    a, gla_new = _gla_core(seq(q), seq(k), seq(g), seq(v), seq(r), gla_prev, w["gla"]["gain"])
    h, xn = post(h, flat(a), 2)

    a, conv_new = _conv_front(seq(xn), conv_prev, w["conv"])
    h, _ = post(h, flat(a), 3)
    return h.reshape(b, t, D_MODEL), pool_new, k_new, v_new, ik_new, gla_new, conv_new


def _prepare_weights(norm_mix, norm_mlp, norm_ple, w_mlp1, w_mlp2, w_ple_proj, w_ple_gate, w_pool, b_pool,
                     pool_scale, w_dsa_in, w_dsa_out, q_norm, k_norm, rel_bias, w_gla_in, w_gla_a2, b_gla_a,
                     gla_norm, w_gla_out, w_conv_in, conv_w, conv_b, w_conv_out):
    bf = lambda z: z.astype(BF16)
    vec = lambda z: z.reshape(1, -1).astype(F32)
    stack_vec = lambda z: z.reshape(z.shape[0], 1, z.shape[1]).astype(F32)
    zeros = jnp.zeros((1, D_MODEL), F32)
    ones = jnp.ones((1, D_MODEL), F32)
    mix_out = [(bf(w_pool), vec(b_pool), vec(pool_scale)), (bf(w_dsa_out), zeros, ones),
               (bf(w_gla_out), zeros, ones), (bf(w_conv_out), zeros, ones)]
    post = dict(gmix=stack_vec(norm_mix), gmlp=stack_vec(norm_mlp), gple=stack_vec(norm_ple), w1=bf(w_mlp1),
                w2=bf(w_mlp2), wg=bf(w_ple_gate), wp=bf(w_ple_proj))

    dsa = dict(w_in=bf(w_dsa_in), qg=vec(q_norm), kg=vec(k_norm), rel_bias=rel_bias.astype(F32))
    gla = dict(w_in=bf(w_gla_in), wa2=bf(w_gla_a2), ba=vec(b_gla_a), gain=vec(gla_norm))
    conv = dict(w_in=bf(w_conv_in), cw=conv_w.astype(F32), cb=vec(conv_b))
    return dict(post=post, mix_out=mix_out, gmix0=vec(norm_mix[0]), dsa=dsa, gla=gla, conv=conv)


def kernel(x_prompt, x_sample, p_prompt, p_sample, state_pool, cache_k, cache_v, cache_idx_k, state_gla, state_conv, norm_mix, norm_mlp, norm_ple, w_mlp1, w_mlp2, w_ple_proj, w_ple_gate, w_pool, b_pool, pool_scale, w_dsa_in, w_dsa_out, q_norm, k_norm, rel_bias, w_gla_in, w_gla_a2, b_gla_a, gla_norm, w_gla_out, w_conv_in, conv_w, conv_b, w_conv_out):
    w = _prepare_weights(norm_mix, norm_mlp, norm_ple, w_mlp1, w_mlp2, w_ple_proj, w_ple_gate, w_pool, b_pool,
                         pool_scale, w_dsa_in, w_dsa_out, q_norm, k_norm, rel_bias, w_gla_in, w_gla_a2, b_gla_a,
                         gla_norm, w_gla_out, w_conv_in, conv_w, conv_b, w_conv_out)
    bp = x_prompt.shape[0]
    dt = x_prompt.dtype
    y_p, pool_p, k_p, v_p, ik_p, gla_p, conv_p = _trunk(
        x_prompt, p_prompt,
        jnp.zeros((bp, POOL_CTX, D_MODEL), dt),
        jnp.zeros((bp, 0, N_KV_HEADS, HEAD_DIM), dt),
        jnp.zeros((bp, 0, N_KV_HEADS, HEAD_DIM), dt),
        jnp.zeros((bp, 0, IDX_DIM), dt),
        jnp.zeros((bp, GLA_HEADS, GLA_DK, GLA_DV), dt),
        jnp.zeros((bp, CONV_W - 1, D_MODEL), dt),
        w)
    y_s, pool_s, k_s, v_s, ik_s, gla_s, conv_s = _trunk(
        x_sample, p_sample, state_pool, cache_k, cache_v, cache_idx_k, state_gla, state_conv, w)
    return (y_p, y_s, pool_p, pool_s, k_p, v_p, ik_p, k_s, v_s, ik_s, gla_p, gla_s, conv_p, conv_s)
```

```python
import functools

import jax
import jax.numpy as jnp
from jax import lax
from jax.experimental import pallas as pl
from jax.experimental.pallas import tpu as pltpu

F32 = jnp.float32
BF16 = jnp.bfloat16
I32 = jnp.int32

D_MODEL = 1024
D_FF = 4 * D_MODEL
PLE_DIM = 256
EPS = 1e-6
CHUNK = 64
CHUNK_SHIFT = 6
POOL_WINDOWS = (2, 4, 8, 16)
POOL_GC = D_MODEL // len(POOL_WINDOWS)
POOL_CTX = max(POOL_WINDOWS) - 1
POOL_ROW_BLOCK = 128
N_HEADS = 8
N_KV_HEADS = 2
HEAD_DIM = D_MODEL // N_HEADS
GROUP = N_HEADS // N_KV_HEADS
IDX_HEADS = 8
IDX_DIM = 64
TOPK_MAX = 256
REL_BUCKETS = 32
REL_LOG_EDGES = (12, 16, 23, 32, 46, 64, 91)
REL_FAR_BUCKET = REL_BUCKETS // 2 - 1
GLA_HEADS = 4
GLA_DK = D_MODEL // 2 // GLA_HEADS
GLA_DV = D_MODEL // GLA_HEADS
GLA_RANK = 16
GLA_GATE_NORM = 16.0
GLA_SUB = 16
CONV_W = 3

V7X_VMEM_LIMIT_BYTES = 56 * 1024 * 1024
SUBLANES = 8
BF16_SUBLANES = 16
HIGH_HALF = -(2 ** 16)
F32_MIN_NORMAL_BITS = 0x00800000
F32_MIN_NORMAL = 2.0 ** -126
INT_MIN = -(2 ** 31)
QUERY_BLOCK = 256
SHARED_QUERY_BLOCK = 128
KEY_TILE = 256

HIGHEST = lax.Precision.HIGHEST
LOG2_E = 1.4426950408889634


def _params(*sem):
    return pltpu.CompilerParams(dimension_semantics=sem, vmem_limit_bytes=V7X_VMEM_LIMIT_BYTES)


def _const_spec(shape):
    nd = len(shape)
    return pl.BlockSpec(shape, lambda *_: (0,) * nd, pipeline_mode=pl.Buffered(1))


def _rms(x, g):
    return x * lax.rsqrt(jnp.mean(x * x, axis=-1, keepdims=True) + EPS) * g


def _dot(a, b):
    return jnp.dot(a, b, preferred_element_type=F32)


def _dot_nt(a, b):
    return lax.dot_general(a, b, (((1,), (1,)), ((), ())), preferred_element_type=F32)


def _dot_tn(a, b):
    return lax.dot_general(a, b, (((0,), (0,)), ((), ())), preferred_element_type=F32)


def _fold_rows(x, op, tile_rows=SUBLANES):
    n = x.shape[0] // tile_rows
    assert n & (n - 1) == 0
    parts = x.reshape(n, tile_rows, x.shape[1])
    while n > 1:
        n //= 2
        parts = op(parts[:n], parts[n:])
    return parts[0]


def _row_tile(n, want):
    t = min(n, want)
    assert n % t == 0
    return t


def _rows_per_step(b, t, tt, max_tokens):
    if t != tt:
        return 1
    return max(d for d in range(1, b + 1) if b % d == 0 and d * tt <= max(tt, max_tokens))


def _pool_step(x, g, prev_ref, row, xe_ref, is_first, pos_first):
    tt = x.shape[0]
    ctx = POOL_CTX + 1

    @pl.when(is_first)
    def _():
        xe_ref[0:1, :] = jnp.zeros((1, D_MODEL), F32)
        xe_ref[1:ctx, :] = prev_ref[row]

    xe_ref[ctx:ctx + tt, :] = _rms(x, g)
    rb = min(tt, POOL_ROW_BLOCK)
    kk = rb + ctx
    t_i = lax.broadcasted_iota(I32, (rb, kk), 0) + ctx
    c_i = lax.broadcasted_iota(I32, (rb, kk), 1)
    bands = [((c_i <= t_i) & (c_i > t_i - w)).astype(BF16) for w in POOL_WINDOWS]
    blocks = []
    for r in range(tt // rb):
        slab = xe_ref[r * rb:r * rb + kk, :]
        hi = slab.astype(BF16)
        lo = (slab - hi.astype(F32)).astype(BF16)
        pos1 = (pos_first + 1 + r * rb + lax.broadcasted_iota(I32, (rb, 1), 0)).astype(F32)
        parts = []
        for gi, w in enumerate(POOL_WINDOWS):
            cols = slice(gi * POOL_GC, (gi + 1) * POOL_GC)
            win = _dot(bands[gi], hi[:, cols]) + _dot(bands[gi], lo[:, cols])
            parts.append((win / jnp.minimum(float(w), pos1) - slab[ctx:, cols]).astype(BF16))
        blocks.append(jnp.concatenate(parts, axis=1))
    xe_ref[0:ctx, :] = xe_ref[tt:tt + ctx, :]
    return jnp.concatenate(blocks, axis=0)


def _post_kernel(h_ref, a_ref, p_ref, wout_ref, bout_ref, sout_ref, gmlp_ref, w1_ref, w2_ref, gple_ref, wg_ref,
                 wp_ref, *rest, ff_chunk, with_next):
    if with_next:
        gnext_ref, h_out_ref, xn_out_ref = rest
    else:
        (h_out_ref,) = rest
    if len(wout_ref.shape) == 3:
        gc = wout_ref.shape[1]
        y = jnp.concatenate([_dot(a_ref[:, g * gc:(g + 1) * gc], wout_ref[g]) for g in range(wout_ref.shape[0])],
                            axis=1)
    else:
        y = _dot(a_ref[...], wout_ref[...])
    y = (y + bout_ref[...]) * sout_ref[...]
    h1 = h_ref[...] + y
    hn = _rms(h1, gmlp_ref[...]).astype(BF16)
    acc = h1
    for c in range(D_FF // ff_chunk):
        cols = slice(c * ff_chunk, (c + 1) * ff_chunk)
        hid = jnp.square(jnp.maximum(_dot(hn, w1_ref[:, cols]), 0.0)).astype(BF16)
        acc = acc + _dot(hid, w2_ref[cols, :])
    gate = jax.nn.sigmoid(_dot(_rms(acc, gple_ref[...]).astype(BF16), wg_ref[...]))
    h3 = acc + _dot(p_ref[...].astype(BF16), wp_ref[...]) * gate
    h_out_ref[...] = h3
    if with_next:
        xn_out_ref[...] = _rms(h3, gnext_ref[...]).astype(BF16)


def _layer_spec(shape, layer):
    nd = len(shape)
    return pl.BlockSpec((None,) + tuple(shape), lambda *_: (layer,) + (0,) * nd, pipeline_mode=pl.Buffered(1))


def _post_block(h, a, p, pw, layer, wout, bout, sout):
    n = h.shape[0]
    tm = _row_tile(n, 512)
    depth = pw["w1"].shape[0]
    with_next = layer + 1 < depth
    row = lambda w: pl.BlockSpec((tm, w), lambda i: (i, 0))
    vec = _const_spec((1, D_MODEL))
    lvec = lambda l: _layer_spec((1, D_MODEL), l)
    out_shape = [jax.ShapeDtypeStruct((n, D_MODEL), F32)]
    out_specs = [row(D_MODEL)]
    in_specs = [row(D_MODEL), row(D_MODEL), pl.BlockSpec((None, tm, PLE_DIM), lambda i: (layer, i, 0)),
                _const_spec(wout.shape), vec, vec, lvec(layer),
                _layer_spec((D_MODEL, D_FF), layer), _layer_spec((D_FF, D_MODEL), layer), lvec(layer),
                _layer_spec((D_MODEL, D_MODEL), layer), _layer_spec((PLE_DIM, D_MODEL), layer)]
    args = [h, a, p, wout, bout, sout, pw["gmlp"], pw["w1"], pw["w2"], pw["gple"], pw["wg"], pw["wp"]]
    if with_next:
        out_shape.append(jax.ShapeDtypeStruct((n, D_MODEL), BF16))
        out_specs.append(row(D_MODEL))
        in_specs.append(lvec(layer + 1))
        args.append(pw["gmix"])
    outs = pl.pallas_call(
        functools.partial(_post_kernel, ff_chunk=1024, with_next=with_next),
        grid=(n // tm,),
        in_specs=in_specs,
        out_specs=out_specs,
        out_shape=out_shape,
        compiler_params=_params("parallel"),
        name="post_block",
    )(*args)
    return (outs[0], outs[1]) if with_next else (outs[0], None)


def _pool_kernel(x_ref, prev_ref, g_ref, a_ref, pool_ref, xe_ref, *, tt, pos0):
    t = pl.program_id(1)
    for row in range(x_ref.shape[0]):
        a_ref[row] = _pool_step(x_ref[row], g_ref[...], prev_ref, row, xe_ref, t == 0, pos0 + t * tt)

        @pl.when(t == pl.num_programs(1) - 1)
        def _():
            pool_ref[row] = xe_ref[1:POOL_CTX + 1, :]


def _pool_front(x, prev, g, pos0):
    b, t, _ = x.shape
    tt = _row_tile(t, 512)
    bb = _rows_per_step(b, t, tt, 256)
    return pl.pallas_call(
        functools.partial(_pool_kernel, tt=tt, pos0=pos0),
        grid=(b // bb, t // tt),
        in_specs=[pl.BlockSpec((bb, tt, D_MODEL), lambda i, j: (i, j, 0)),
                  pl.BlockSpec((bb, POOL_CTX, D_MODEL), lambda i, j: (i, 0, 0)),
                  _const_spec((1, D_MODEL))],
        out_specs=[pl.BlockSpec((bb, tt, D_MODEL), lambda i, j: (i, j, 0)),
                   pl.BlockSpec((bb, POOL_CTX, D_MODEL), lambda i, j: (i, 0, 0))],
        out_shape=[jax.ShapeDtypeStruct((b, t, D_MODEL), BF16),
                   jax.ShapeDtypeStruct((b, POOL_CTX, D_MODEL), F32)],
        scratch_shapes=[pltpu.VMEM((POOL_CTX + 1 + tt, D_MODEL), F32)],
        compiler_params=_params("arbitrary", "arbitrary"),
        name="pool_front",
    )(x, prev, g)


DSA_KV = N_KV_HEADS * HEAD_DIM
DSA_IQ = IDX_HEADS * IDX_DIM
DSA_COLS = (0, D_MODEL, D_MODEL + DSA_KV, D_MODEL + 2 * DSA_KV, D_MODEL + 2 * DSA_KV + DSA_IQ,
            D_MODEL + 2 * DSA_KV + DSA_IQ + IDX_DIM, D_MODEL + 2 * DSA_KV + DSA_IQ + IDX_DIM + IDX_HEADS)


def _dsa_in_kernel(xn_ref, w_ref, qg_ref, kg_ref, q_ref, iq_ref, iw_ref, k2_ref, v2_ref, ik_ref, *key_refs,
                   tm, for_attn):
    x = xn_ref[...]
    qb = QUERY_BLOCK
    assert tm % qb == 0 or not for_attn
    if for_attn:
        kb_ref, vt_ref, ikb_ref = key_refs
    col = lambda i: slice(DSA_COLS[i], DSA_COLS[i + 1])
    for h in range(N_HEADS):
        if h % 2 == 0:
            q2 = _dot(x, w_ref[:, DSA_COLS[0] + h * HEAD_DIM:DSA_COLS[0] + (h + 2) * HEAD_DIM])
            iq2 = _dot(x, w_ref[:, DSA_COLS[3] + h * IDX_DIM:DSA_COLS[3] + (h + 2) * IDX_DIM]).astype(BF16)
        qh = _rms(q2[:, (h % 2) * HEAD_DIM:(h % 2 + 1) * HEAD_DIM], qg_ref[...]).astype(BF16)
        iqh = iq2[:, (h % 2) * IDX_DIM:(h % 2 + 1) * IDX_DIM]
        if for_attn:
            for j in range(tm // qb):
                q_ref[j, h] = qh[j * qb:(j + 1) * qb]
                iq_ref[j, h] = iqh[j * qb:(j + 1) * qb]
        else:
            q_ref[:, h * HEAD_DIM:(h + 1) * HEAD_DIM] = qh
            iq_ref[:, h * IDX_DIM:(h + 1) * IDX_DIM] = iqh
    k = _dot(x, w_ref[:, col(1)])
    v = _dot(x, w_ref[:, col(2)])
    tail = _dot(x, w_ref[:, DSA_COLS[4]:DSA_COLS[6]])
    ik = tail[:, :IDX_DIM]
    ik_ref[...] = ik
    iw_ref[...] = tail[:, IDX_DIM:]
    for h in range(N_KV_HEADS):
        cols = slice(h * HEAD_DIM, (h + 1) * HEAD_DIM)
        kh = _rms(k[:, cols], kg_ref[...])
        k2_ref[pl.ds(h, tm, stride=N_KV_HEADS), :] = kh
        v2_ref[pl.ds(h, tm, stride=N_KV_HEADS), :] = v[:, cols]
        if for_attn:
            kb_ref[:, cols] = kh.astype(BF16)
            for j in range(tm // KEY_TILE):
                vt_ref[j, h] = v[j * KEY_TILE:(j + 1) * KEY_TILE, cols].T.astype(BF16)
    if for_attn:
        ikb_ref[...] = ik.astype(BF16)


def _dsa_in(xn, w, for_attn):
    n = xn.shape[0]
    tm = _row_tile(n, 512)
    qb = QUERY_BLOCK
    row = lambda wd: pl.BlockSpec((tm, wd), lambda i: (i, 0))
    lead = lambda shape: pl.BlockSpec(shape, lambda i: (i,) + (0,) * (len(shape) - 1))
    if for_attn:
        assert tm % KEY_TILE == 0 and tm % qb == 0
        q_specs = [lead((tm // qb, N_HEADS, qb, HEAD_DIM)), lead((tm // qb, IDX_HEADS, qb, IDX_DIM))]
        q_shapes = [jax.ShapeDtypeStruct((n // qb, N_HEADS, qb, HEAD_DIM), BF16),
                    jax.ShapeDtypeStruct((n // qb, IDX_HEADS, qb, IDX_DIM), BF16)]
        key_specs = [row(DSA_KV), lead((tm // KEY_TILE, N_KV_HEADS, HEAD_DIM, KEY_TILE)), row(IDX_DIM)]
        key_shapes = [jax.ShapeDtypeStruct((n, DSA_KV), BF16),
                      jax.ShapeDtypeStruct((n // KEY_TILE, N_KV_HEADS, HEAD_DIM, KEY_TILE), BF16),
                      jax.ShapeDtypeStruct((n, IDX_DIM), BF16)]
    else:
        q_specs = [row(D_MODEL), row(DSA_IQ)]
        q_shapes = [jax.ShapeDtypeStruct((n, D_MODEL), BF16), jax.ShapeDtypeStruct((n, DSA_IQ), BF16)]
        key_specs, key_shapes = [], []
    kv2 = pl.BlockSpec((N_KV_HEADS * tm, HEAD_DIM), lambda i: (i, 0))
    return pl.pallas_call(
        functools.partial(_dsa_in_kernel, tm=tm, for_attn=for_attn),
        grid=(n // tm,),
        in_specs=[row(D_MODEL), _const_spec((D_MODEL, DSA_COLS[-1])), _const_spec((1, HEAD_DIM)),
                  _const_spec((1, HEAD_DIM))],
        out_specs=q_specs + [row(IDX_HEADS), kv2, kv2, row(IDX_DIM)] + key_specs,
        out_shape=q_shapes + [jax.ShapeDtypeStruct((n, IDX_HEADS), F32),
                              jax.ShapeDtypeStruct((N_KV_HEADS * n, HEAD_DIM), F32),
                              jax.ShapeDtypeStruct((N_KV_HEADS * n, HEAD_DIM), F32),
                              jax.ShapeDtypeStruct((n, IDX_DIM), F32)] + key_shapes,
        compiler_params=_params("parallel"),
        name="dsa_in",
    )(xn, w["w_in"], w["qg"], w["kg"])


def _dsa_keys_kernel(ck_ref, cv_ref, k2_ref, v2_ref, kb_ref, vt_ref, *, past, t, nb):
    lp = kb_ref.shape[1]
    nh = N_KV_HEADS
    for r in range(nb):
        for h in range(nh):
            cols = slice((h * nb + r) * HEAD_DIM, (h * nb + r + 1) * HEAD_DIM)
            keys = slice(r * KEY_TILE, (r + 1) * KEY_TILE)
            kb_ref[0, 0:past, cols] = ck_ref[r, pl.ds(h, past, stride=nh), :].astype(BF16)
            kb_ref[0, past:past + t, cols] = k2_ref[r, pl.ds(h, t, stride=nh), :].astype(BF16)
            kb_ref[0, past + t:lp, cols] = jnp.zeros((lp - past - t, HEAD_DIM), BF16)
            for j in range(past // KEY_TILE):
                v_old = cv_ref[r, pl.ds(h + nh * KEY_TILE * j, KEY_TILE, stride=nh), :]
                vt_ref[0, j, h, :, keys] = v_old.T.astype(BF16)
            v_new = jnp.concatenate([v2_ref[r, pl.ds(h, t, stride=nh), :],
                                     jnp.zeros((KEY_TILE - t, HEAD_DIM), F32)], axis=0)
            vt_ref[0, past // KEY_TILE, h, :, keys] = v_new.T.astype(BF16)


def _dsa_keys(cache_k, cache_v, k2, v2, nb):
    b, past = cache_k.shape[:2]
    nh = N_KV_HEADS
    t = k2.shape[1] // nh
    lp = past + KEY_TILE
    assert past % KEY_TILE == 0 and t <= KEY_TILE and t % SUBLANES == 0 and b % nb == 0
    rows = lambda *shape: pl.BlockSpec((nb,) + shape, lambda i: (i,) + (0,) * len(shape))
    group = lambda *shape: pl.BlockSpec((1,) + shape, lambda i: (i,) + (0,) * len(shape))
    return pl.pallas_call(
        functools.partial(_dsa_keys_kernel, past=past, t=t, nb=nb),
        grid=(b // nb,),
        in_specs=[rows(nh * past, HEAD_DIM), rows(nh * past, HEAD_DIM), rows(nh * t, HEAD_DIM),
                  rows(nh * t, HEAD_DIM)],
        out_specs=[group(lp, nb * DSA_KV), group(lp // KEY_TILE, nh, HEAD_DIM, nb * KEY_TILE)],
        out_shape=[jax.ShapeDtypeStruct((b // nb, lp, nb * DSA_KV), BF16),
                   jax.ShapeDtypeStruct((b // nb, lp // KEY_TILE, nh, HEAD_DIM, nb * KEY_TILE), BF16)],
        compiler_params=_params("parallel"),
        name="dsa_keys",
    )(cache_k.reshape(b, nh * past, HEAD_DIM), cache_v.reshape(b, nh * past, HEAD_DIM), k2, v2)


def _dsa_attn_kernel(rb_ref, q_ref, iq_ref, iwt_ref, kb_ref, vt_ref, ik_ref, o_ref,
                     key_ref, msk_ref, lg_ref, bias_ref, j_ref, hi_ref, *, past, n_keys, topk, idx_bits, nb):
    kt_sz = KEY_TILE
    qb = o_ref.shape[1]
    near_slots = bias_ref.shape[1]
    qpb = qb // nb
    b = pl.program_id(0)
    i = pl.program_id(1)
    q0 = past + i * qpb
    kend = jnp.minimum(n_keys, (((q0 + qpb - 1) >> CHUNK_SHIFT) + 1) * CHUNK)
    nkt = (kend + kt_sz - 1) // kt_sz
    row_i = lax.broadcasted_iota(I32, (kt_sz, qb), 0)
    lane_i = lax.broadcasted_iota(I32, (kt_sz, qb), 1)
    col_i = lane_i % qpb

    @pl.when((b == 0) & (i == 0))
    def _():
        for w in range(1, near_slots):
            rel = row_i - col_i + (w - (near_slots - 1)) * qb
            n = jnp.abs(rel)
            log_bucket = jnp.full_like(n, REL_BUCKETS // 4)
            for edge in REL_LOG_EDGES:
                log_bucket = log_bucket + (n >= edge).astype(I32)
            bucket = jnp.where(n < REL_BUCKETS // 4, n, log_bucket) + jnp.where(rel > 0, REL_BUCKETS // 2, 0)
            for h in range(N_HEADS):
                val = jnp.zeros((kt_sz, qb), F32)
                for bk in range(REL_BUCKETS):
                    val = jnp.where(bucket == bk, rb_ref[bk, h], val)
                bias_ref[h, w] = val * LOG2_E
        for h in range(N_HEADS):
            bias_ref[h, 0] = jnp.full((kt_sz, qb), rb_ref[REL_FAR_BUCKET, h], F32) * LOG2_E

    iw = iwt_ref[0] * (IDX_DIM ** -0.5)
    iq_all = iq_ref[0, 0].reshape(IDX_HEADS * qb, nb * IDX_DIM)
    q_chunk = (q0 + col_i) >> CHUNK_SHIFT

    def tile_rows(jt):
        return pl.ds(pl.multiple_of(jt * kt_sz, kt_sz), kt_sz)

    def score_tile(jt, carry):
        ikt = ik_ref[0, tile_rows(jt), :]
        s = jnp.zeros((kt_sz, qb), F32)
        for hp in range(IDX_HEADS // 2):
            s2 = _dot_nt(ikt, iq_all[2 * hp * qb:(2 * hp + 2) * qb, :])
            for h in (2 * hp, 2 * hp + 1):
                s = s + iw[h:h + 1, :] * jnp.maximum(s2[:, (h - 2 * hp) * qb:(h - 2 * hp + 1) * qb], 0.0)
        s = jnp.where(jnp.abs(s) < F32_MIN_NORMAL, 0.0, s)
        bits = pltpu.bitcast(s, I32)
        key = bits ^ ((bits >> 31) & 0x7FFFFFFF)
        kpos = jt * kt_sz + row_i
        adm = ((kpos >> CHUNK_SHIFT) <= q_chunk) & (kpos < n_keys)
        key_ref[tile_rows(jt), :] = jnp.where(adm, key, INT_MIN)
        s_hi = pltpu.bitcast(bits & HIGH_HALF, F32)
        hi_ref[tile_rows(jt), :] = jnp.where(adm, s_hi, -jnp.inf).astype(BF16)
        return carry

    lax.fori_loop(0, nkt, score_tile, 0)

    def over_tiles(body, carry):
        carry = lax.fori_loop(0, nkt // 2, lambda j2, c: body(2 * j2, 2, c), carry)
        return lax.cond(nkt % 2 == 1, lambda c: body(nkt - 1, 1, c), lambda c: c, carry)

    def tiles_rows(jt, n_tiles):
        return pl.ds(pl.multiple_of(jt * kt_sz, kt_sz), n_tiles * kt_sz)

    def count(pred):
        def body(jt, acc):
            hit = jnp.where(pred(key_ref[tile_rows(jt), :], jt * kt_sz + row_i), 1.0, 0.0)
            return acc + _fold_rows(hit, jnp.add)

        acc = lax.fori_loop(0, nkt, body, jnp.zeros((8, qb), F32))
        return jnp.sum(acc, axis=0, keepdims=True)

    kf = float(topk)

    def count_high(cand):
        cand_bits = (cand ^ ((cand >> 31) & 0x7FFFFFFF)) & HIGH_HALF
        cand_bits = jnp.where((cand_bits > 0) & (cand_bits < F32_MIN_NORMAL_BITS), F32_MIN_NORMAL_BITS, cand_bits)
        cand_hi = jnp.broadcast_to(pltpu.bitcast(cand_bits, F32).astype(BF16), (kt_sz, qb))
        one = jnp.ones((kt_sz, qb), BF16)
        zero = jnp.zeros((kt_sz, qb), BF16)

        def body(jt, acc):
            hit = jnp.where(hi_ref[tile_rows(jt), :] >= cand_hi, one, zero)
            return acc + _fold_rows(hit, jnp.add, BF16_SUBLANES).astype(F32)

        acc = lax.fori_loop(0, nkt, body, jnp.zeros((BF16_SUBLANES, qb), F32))
        return jnp.sum(acc, axis=0, keepdims=True)

    n_pos = count_high(jnp.zeros((1, qb), I32))
    thr0 = jnp.where(n_pos >= kf, 0, INT_MIN).astype(I32)
    n_ge0 = n_pos

    def bisect(counter, p, state):
        thr, n_ge = state
        cand = thr | jnp.left_shift(jnp.int32(1), 30 - p)
        n_cand = counter(cand)
        take = n_cand >= kf
        return jnp.where(take, cand, thr), jnp.where(take, n_cand, n_ge)

    high_passes = 15
    state = lax.fori_loop(0, high_passes, functools.partial(bisect, count_high), (thr0, n_ge0))
    thr, n_ge = lax.fori_loop(high_passes, 31,
                              functools.partial(bisect, lambda cand: count(lambda kk, kpos: kk >= cand)), state)
    has_thr = thr > INT_MIN
    j_ref[...] = jnp.where(has_thr, n_keys, -1).astype(I32)
    tied_cut = jnp.max(jnp.where(has_thr & (n_ge > kf), 1, 0)) > 0

    @pl.when(tied_cut)
    def _():
        need = kf - count(lambda kk, jt: kk > thr)
        lo = jnp.zeros((1, qb), I32)
        for bit in range(idx_bits - 1, -1, -1):
            cand = lo + (1 << bit)
            below = count(lambda kk, kpos: (kk == thr) & (kpos < cand))
            lo = jnp.where(below < need, cand, lo)
        j_ref[...] = jnp.where(has_thr, lo, -1)

    j_last = j_ref[...]

    def mask_tile(jt, carry):
        kk = key_ref[tile_rows(jt), :]
        sel = (kk > thr) | ((kk == thr) & ((jt * kt_sz + row_i) <= j_last))
        msk_ref[tile_rows(jt), :] = jnp.where(sel, 0.0, -jnp.inf)
        return carry

    lax.fori_loop(0, nkt, mask_tile, 0)

    gq = GROUP * qb
    kdim = nb * HEAD_DIM
    slot0 = q0 // qb - (near_slots - 1)

    def bias_slot(jt):
        w = jt * (kt_sz // qb) - slot0
        return jnp.where(w >= 1, w, 0)

    kv_heads = range(N_KV_HEADS)
    q_groups = [q_ref[0, 0, kvh * GROUP:(kvh + 1) * GROUP].reshape(gq, kdim) for kvh in kv_heads]

    def logits_tile(jt, m8):
        msk = msk_ref[tile_rows(jt), :]
        w = bias_slot(jt)
        new_m8 = []
        for kvh in kv_heads:
            kt = kb_ref[0, tile_rows(jt), kvh * kdim:(kvh + 1) * kdim]
            lg = _dot_nt(kt, q_groups[kvh]) * (HEAD_DIM ** -0.5 * LOG2_E)
            parts = []
            for g in range(GROUP):
                h = kvh * GROUP + g
                lgh = lg[:, g * qb:(g + 1) * qb] + bias_ref[h, w] + msk
                lg_ref[tile_rows(jt), h * qb:(h + 1) * qb] = lgh
                parts.append(_fold_rows(lgh, jnp.maximum))
            new_m8.append(jnp.maximum(m8[kvh], jnp.concatenate(parts, axis=1)))
        return tuple(new_m8)

    m8 = lax.fori_loop(0, nkt, logits_tile, tuple(jnp.full((8, gq), -jnp.inf, F32) for _ in kv_heads))
    m = [jnp.max(m8[kvh], axis=0, keepdims=True) for kvh in kv_heads]

    lane_row_g = lax.broadcasted_iota(I32, (HEAD_DIM, gq), 1) % qb // qpb

    def pv_tiles(jt, n_tiles, carry):
        rows = pl.ds(pl.multiple_of(jt * kt_sz, kt_sz), n_tiles * kt_sz)
        new = []
        for kvh in kv_heads:
            acc, l8 = carry[kvh]
            p = jnp.exp2(lg_ref[rows, kvh * gq:(kvh + 1) * gq] - m[kvh])
            pb = p.astype(BF16)

            def values(r):
                tiles = [vt_ref[0, jt + u, kvh, :, r * kt_sz:(r + 1) * kt_sz] for u in range(n_tiles)]
                return tiles[0] if n_tiles == 1 else jnp.concatenate(tiles, axis=1)

            pv = _dot(values(0), pb)
            for r in range(1, nb):
                pv = jnp.where(lane_row_g == r, _dot(values(r), pb), pv)
            new.append((acc + pv, l8 + _fold_rows(p, jnp.add)))
        return tuple(new)

    zero = (jnp.zeros((HEAD_DIM, gq), F32), jnp.zeros((8, gq), F32))
    acc_l8 = over_tiles(pv_tiles, tuple(zero for _ in kv_heads))
    for kvh in kv_heads:
        acc, l8 = acc_l8[kvh]
        o_t = acc / jnp.sum(l8, axis=0, keepdims=True)
        for g in range(GROUP):
            h = kvh * GROUP + g
            o_ref[0, :, h * HEAD_DIM:(h + 1) * HEAD_DIM] = o_t[:, g * qb:(g + 1) * qb].T.astype(BF16)


def _dsa_attn(rel_bias, q, iq, iwt, kb, vt, ikb, *, past, n_keys):
    g, nblk = q.shape[:2]
    nb = q.shape[-1] // HEAD_DIM
    lp = kb.shape[1]
    qb = q.shape[3]
    near_slots = KEY_TILE // qb + 2
    assert lp % KEY_TILE == 0 and past % KEY_TILE == 0 and KEY_TILE % qb == 0 and qb >= REL_LOG_EDGES[-1]
    assert qb % nb == 0 and (nb == 1 or nblk == 1)
    topk = min(TOPK_MAX, n_keys // 4)
    return pl.pallas_call(
        functools.partial(_dsa_attn_kernel, past=past, n_keys=n_keys, topk=topk,
                          idx_bits=max(1, (lp - 1).bit_length()), nb=nb),
        grid=(g, nblk),
        in_specs=[pl.BlockSpec(memory_space=pltpu.SMEM),
                  pl.BlockSpec((1, 1, N_HEADS, qb, nb * HEAD_DIM), lambda i, j: (i, j, 0, 0, 0)),
                  pl.BlockSpec((1, 1, IDX_HEADS, qb, nb * IDX_DIM), lambda i, j: (i, j, 0, 0, 0)),
                  pl.BlockSpec((1, IDX_HEADS, qb), lambda i, j: (i, 0, j)),
                  pl.BlockSpec((1, lp, nb * DSA_KV), lambda i, j: (i, 0, 0)),
                  pl.BlockSpec((1, lp // KEY_TILE, N_KV_HEADS, HEAD_DIM, nb * KEY_TILE),
                               lambda i, j: (i, 0, 0, 0, 0)),
                  pl.BlockSpec((1, lp, nb * IDX_DIM), lambda i, j: (i, 0, 0))],
        out_specs=pl.BlockSpec((1, qb, D_MODEL), lambda i, j: (i, j, 0)),
        out_shape=jax.ShapeDtypeStruct((g, nblk * qb, D_MODEL), BF16),
        scratch_shapes=[pltpu.VMEM((lp, qb), I32), pltpu.VMEM((lp, qb), F32), pltpu.VMEM((lp, N_HEADS * qb), F32),
                        pltpu.VMEM((N_HEADS, near_slots, KEY_TILE, qb), F32), pltpu.VMEM((1, qb), I32),
                        pltpu.VMEM((lp, qb), BF16)],
        compiler_params=_params("arbitrary", "arbitrary"),
        name="dsa_attn",
    )(rel_bias, q, iq, iwt, kb, vt, ikb)


def _dsa_mixer(xn, k_past, v_past, ik_past, w, b, t):
    past = k_past.shape[1]
    n_keys = past + t
    nh = N_KV_HEADS
    if past == 0:
        qb = QUERY_BLOCK
        assert t % KEY_TILE == 0 and t % qb == 0
        q, iq, iw, k2, v2, ik, kb, vt, ikb = _dsa_in(xn, w, for_attn=True)
        q = q.reshape(b, t // qb, N_HEADS, qb, HEAD_DIM)
        iq = iq.reshape(b, t // qb, IDX_HEADS, qb, IDX_DIM)
        iwt = iw.reshape(b, t, IDX_HEADS).transpose(0, 2, 1)
        kb = kb.reshape(b, t, DSA_KV)
        vt = vt.reshape(b, t // KEY_TILE, nh, HEAD_DIM, KEY_TILE)
        ikb = ikb.reshape(b, t, IDX_DIM)
    else:
        qb = SHARED_QUERY_BLOCK
        assert qb % t == 0 and b % (qb // t) == 0
        nb = qb // t
        g = b // nb
        q, iq, iw, k2, v2, ik = _dsa_in(xn, w, for_attn=False)
        eye = jnp.eye(nb, dtype=BF16)

        def own_slice(z, n_h):
            d = z.shape[-1] // n_h
            z = z.reshape(g, nb, t, n_h, d).transpose(0, 3, 1, 2, 4)
            z = z[:, :, :, :, None, :] * eye[None, None, :, None, :, None]
            return z.reshape(g, 1, n_h, qb, nb * d)

        q, iq = own_slice(q, N_HEADS), own_slice(iq, IDX_HEADS)
        iwt = iw.reshape(g, qb, IDX_HEADS).transpose(0, 2, 1)
        kb, vt = _dsa_keys(k_past, v_past, k2.reshape(b, nh * t, HEAD_DIM), v2.reshape(b, nh * t, HEAD_DIM), nb)
        lp = kb.shape[1]
        ikb = jnp.concatenate([ik_past.astype(BF16), ik.reshape(b, t, IDX_DIM).astype(BF16),
                               jnp.zeros((b, lp - n_keys, IDX_DIM), BF16)], axis=1)
        ikb = ikb.reshape(g, nb, lp, IDX_DIM).transpose(0, 2, 1, 3).reshape(g, lp, nb * IDX_DIM)
    o = _dsa_attn(w["rel_bias"], q, iq, iwt, kb, vt, ikb, past=past, n_keys=n_keys)
    o = o.reshape(b * t, D_MODEL)
    return (o, k2.reshape(b, t, nh, HEAD_DIM), v2.reshape(b, t, nh, HEAD_DIM), ik.reshape(b, t, IDX_DIM))


GLA_QK = GLA_HEADS * GLA_DK
GLA_COLS = (0, GLA_QK, 2 * GLA_QK, 2 * GLA_QK + D_MODEL, 2 * GLA_QK + 2 * D_MODEL, 2 * GLA_QK + 2 * D_MODEL + GLA_RANK)


def _gla_in_kernel(xn_ref, w_ref, wa2_ref, ba_ref, q_ref, k_ref, v_ref, r_ref, g_ref):
    x = xn_ref[...]
    proj = lambda i: _dot(x, w_ref[:, GLA_COLS[i]:GLA_COLS[i + 1]])
    q_ref[...] = proj(0) * (GLA_DK ** -0.5)
    k_ref[...] = proj(1)
    v_ref[...] = proj(2).astype(BF16)
    r_ref[...] = proj(3)
    z = _dot(proj(4).astype(BF16), wa2_ref[...]) + ba_ref[...]
    g_ref[...] = (jnp.minimum(z, 0.0) - jnp.log1p(jnp.exp(-jnp.abs(z)))) * (1.0 / GLA_GATE_NORM)


def _gla_in(xn, w):
    n = xn.shape[0]
    tm = _row_tile(n, 512)
    qk = GLA_HEADS * GLA_DK
    row = lambda wd: pl.BlockSpec((tm, wd), lambda i: (i, 0))
    return pl.pallas_call(
        _gla_in_kernel,
        grid=(n // tm,),
        in_specs=[row(D_MODEL), _const_spec((D_MODEL, GLA_COLS[-1])), _const_spec((GLA_RANK, qk)),
                  _const_spec((1, qk))],
        out_specs=[row(qk), row(qk), row(D_MODEL), row(D_MODEL), row(qk)],
        out_shape=[jax.ShapeDtypeStruct((n, qk), F32), jax.ShapeDtypeStruct((n, qk), F32),
                   jax.ShapeDtypeStruct((n, D_MODEL), BF16), jax.ShapeDtypeStruct((n, D_MODEL), F32),
                   jax.ShapeDtypeStruct((n, qk), F32)],
        compiler_params=_params("parallel"),
        name="gla_in",
    )(xn, w["w_in"], w["wa2"], w["ba"])


def _gla_chunk(q, k, g, v, st, tri_c, tri_sb):
    c = q.shape[0]
    sb = GLA_SUB
    cum = jnp.dot(tri_c, g, precision=HIGHEST, preferred_element_type=F32)
    total = cum[c - 1:c, :]
    o_inter = _dot_nt((q * jnp.exp(cum)).astype(BF16), st.astype(BF16))
    st_new = st * jnp.exp(total) + _dot_tn(v, (k * jnp.exp(total - cum)).astype(BF16))
    o_rows = []
    cum2 = cum * LOG2_E
    for i in range(c // sb):
        r = slice(i * sb, (i + 1) * sb)
        ci = cum2[r]
        decay = jnp.exp2(ci[:, None, :] - ci[None, :, :])
        sc = jnp.sum(decay * q[r][:, None, :] * k[r][None, :, :], axis=-1)
        sc = jnp.where(tri_sb, sc, 0.0)
        o_rows.append(o_inter[r] + _dot(sc.astype(BF16), v[r]))
    width = sb
    while width < c:
        for p in range(c // (2 * width)):
            lo = 2 * p * width
            left = slice(lo, lo + width)
            right = slice(lo + width, lo + 2 * width)
            edge = cum[lo + width - 1:lo + width, :]
            q_hat = (q[right] * jnp.exp(cum[right] - edge)).astype(BF16)
            k_hat = (k[left] * jnp.exp(edge - cum[left])).astype(BF16)
            o_pair = _dot(_dot_nt(q_hat, k_hat).astype(BF16), v[left])
            for j in range(width // sb):
                o_rows[(lo + width) // sb + j] += o_pair[j * sb:(j + 1) * sb]
        width *= 2
    return jnp.concatenate(o_rows, axis=0), st_new


def _gla_core_kernel(q_ref, k_ref, g_ref, v_ref, r_ref, s0_ref, gain_ref, a_ref, sout_ref, st_ref, *, tt, c):
    sb = GLA_SUB
    t = pl.program_id(1)
    bb = q_ref.shape[0]
    batch_rows = range(bb)

    @pl.when(t == 0)
    def _():
        for row in batch_rows:
            for h in range(GLA_HEADS):
                st_ref[row * GLA_HEADS + h] = s0_ref[row, h].T

    tri_c = (lax.broadcasted_iota(I32, (c, c), 0) >= lax.broadcasted_iota(I32, (c, c), 1)).astype(F32)
    tri_sb = lax.broadcasted_iota(I32, (sb, sb), 0) >= lax.broadcasted_iota(I32, (sb, sb), 1)
    gain = gain_ref[...]

    per_trip = 4 if (tt // c) % 4 == 0 else 1

    def chunks(ci, carry):
        for u in range(per_trip):
            rows = pl.ds(pl.multiple_of((ci * per_trip + u) * c, c), c)
            for row in batch_rows:
                for h in range(GLA_HEADS):
                    dk = slice(h * GLA_DK, (h + 1) * GLA_DK)
                    dv = slice(h * GLA_DV, (h + 1) * GLA_DV)
                    si = row * GLA_HEADS + h
                    o, st_ref[si] = _gla_chunk(q_ref[row, rows, dk], k_ref[row, rows, dk], g_ref[row, rows, dk],
                                               v_ref[row, rows, dv], st_ref[si], tri_c, tri_sb)
                    r = r_ref[row, rows, dv]
                    a_ref[row, rows, dv] = (_rms(o, gain) * (r * jax.nn.sigmoid(r))).astype(BF16)
        return carry

    lax.fori_loop(0, tt // (c * per_trip), chunks, 0)

    @pl.when(t == pl.num_programs(1) - 1)
    def _():
        for row in batch_rows:
            for h in range(GLA_HEADS):
                sout_ref[row, h] = st_ref[row * GLA_HEADS + h].T


def _gla_core(q, k, g, v, r, s0, gain):
    b, t, _ = q.shape
    tt = _row_tile(t, 256)
    c = min(tt, CHUNK)
    assert tt % c == 0 and c % GLA_SUB == 0 and (c // GLA_SUB) & (c // GLA_SUB - 1) == 0
    bb = _rows_per_step(b, t, tt, 128)
    qk = GLA_HEADS * GLA_DK
    tok = lambda wd: pl.BlockSpec((bb, tt, wd), lambda i, j: (i, j, 0))
    st_spec = pl.BlockSpec((bb, GLA_HEADS, GLA_DK, GLA_DV), lambda i, j: (i, 0, 0, 0))
    return pl.pallas_call(
        functools.partial(_gla_core_kernel, tt=tt, c=c),
        grid=(b // bb, t // tt),
        in_specs=[tok(qk), tok(qk), tok(qk), tok(D_MODEL), tok(D_MODEL), st_spec, _const_spec((1, GLA_DV))],
        out_specs=[tok(D_MODEL), st_spec],
        out_shape=[jax.ShapeDtypeStruct((b, t, D_MODEL), BF16),
                   jax.ShapeDtypeStruct((b, GLA_HEADS, GLA_DK, GLA_DV), F32)],
        scratch_shapes=[pltpu.VMEM((bb * GLA_HEADS, GLA_DV, GLA_DK), F32)],
        compiler_params=_params("arbitrary", "arbitrary"),
        name="gla_core",
    )(q, k, g, v, r, s0, gain)


def _conv_kernel(xn_ref, w_ref, prev_ref, cw_ref, cb_ref, a_ref, new_ref, ue_ref, *, tt):
    t = pl.program_id(1)
    pad = SUBLANES
    lo = pad - (CONV_W - 1)

    bb = xn_ref.shape[0]

    @pl.when(t == 0)
    def _():
        ue_ref[:, lo:pad, :] = prev_ref[...]

    x = xn_ref[...].reshape(bb * tt, D_MODEL)
    proj = lambda i: _dot(x, w_ref[:, i * D_MODEL:(i + 1) * D_MODEL]).reshape(bb, tt, D_MODEL)
    u = proj(1) * proj(2)
    ue_ref[:, pad:pad + tt, :] = u
    conv = cb_ref[...] + cw_ref[CONV_W - 1:CONV_W, :] * u
    for j in range(CONV_W - 1):
        conv = conv + cw_ref[j:j + 1, :] * ue_ref[:, lo + j:lo + j + tt, :]
    a_ref[...] = (proj(0) * conv).astype(BF16)
    tail = ue_ref[:, lo + tt:pad + tt, :]
    ue_ref[:, lo:pad, :] = tail

    @pl.when(t == pl.num_programs(1) - 1)
    def _():
        new_ref[...] = tail


def _conv_front(xn, prev, w):
    b, t, _ = xn.shape
    tt = _row_tile(t, 512)
    bb = _rows_per_step(b, t, tt, 512)
    tok = pl.BlockSpec((bb, tt, D_MODEL), lambda i, j: (i, j, 0))
    st = pl.BlockSpec((bb, CONV_W - 1, D_MODEL), lambda i, j: (i, 0, 0))
    w_spec = _const_spec((D_MODEL, 3 * D_MODEL))
    return pl.pallas_call(
        functools.partial(_conv_kernel, tt=tt),
        grid=(b // bb, t // tt),
        in_specs=[tok, w_spec, st, _const_spec((CONV_W, D_MODEL)), _const_spec((1, D_MODEL))],
        out_specs=[tok, st],
        out_shape=[jax.ShapeDtypeStruct((b, t, D_MODEL), BF16),
                   jax.ShapeDtypeStruct((b, CONV_W - 1, D_MODEL), F32)],
        scratch_shapes=[pltpu.VMEM((bb, SUBLANES + tt, D_MODEL), F32)],
        compiler_params=_params("arbitrary", "arbitrary"),
        name="conv_front",
    )(xn, w["w_in"], prev, w["cw"], w["cb"])


def _trunk(x, p, pool_prev, k_past, v_past, ik_past, gla_prev, conv_prev, w):
    b, t, _ = x.shape
    n = b * t
    past = k_past.shape[1]
    flat = lambda z: z.reshape(n, z.shape[-1])
    p = p.reshape(p.shape[0], n, PLE_DIM)
    post = lambda h, a, layer: _post_block(h, a, p, w["post"], layer, *w["mix_out"][layer])

    a, pool_new = _pool_front(x, pool_prev, w["gmix0"], past)
    h, xn = post(flat(x), flat(a), 0)

    a, k_new, v_new, ik_new = _dsa_mixer(xn, k_past, v_past, ik_past, w["dsa"], b, t)
    h, xn = post(h, a, 1)

    q, k, v, r, g = _gla_in(xn, w["gla"])
    seq = lambda z: z.reshape(b, t, z.shape[-1])
    a, gla_new = _gla_core(seq(q), seq(k), seq(g), seq(v), seq(r), gla_prev, w["gla"]["gain"])
    h, xn = post(h, flat(a), 2)

    a, conv_new = _conv_front(seq(xn), conv_prev, w["conv"])
    h, _ = post(h, flat(a), 3)
    return h.reshape(b, t, D_MODEL), pool_new, k_new, v_new, ik_new, gla_new, conv_new


def _prepare_weights(norm_mix, norm_mlp, norm_ple, w_mlp1, w_mlp2, w_ple_proj, w_ple_gate, w_pool, b_pool,
                     pool_scale, w_dsa_in, w_dsa_out, q_norm, k_norm, rel_bias, w_gla_in, w_gla_a2, b_gla_a,
                     gla_norm, w_gla_out, w_conv_in, conv_w, conv_b, w_conv_out):
    bf = lambda z: z.astype(BF16)
    vec = lambda z: z.reshape(1, -1).astype(F32)
    stack_vec = lambda z: z.reshape(z.shape[0], 1, z.shape[1]).astype(F32)
    zeros = jnp.zeros((1, D_MODEL), F32)
    ones = jnp.ones((1, D_MODEL), F32)
    mix_out = [(bf(w_pool), vec(b_pool), vec(pool_scale)), (bf(w_dsa_out), zeros, ones),
               (bf(w_gla_out), zeros, ones), (bf(w_conv_out), zeros, ones)]
    post = dict(gmix=stack_vec(norm_mix), gmlp=stack_vec(norm_mlp), gple=stack_vec(norm_ple), w1=bf(w_mlp1),
                w2=bf(w_mlp2), wg=bf(w_ple_gate), wp=bf(w_ple_proj))

    dsa = dict(w_in=bf(w_dsa_in), qg=vec(q_norm), kg=vec(k_norm), rel_bias=rel_bias.astype(F32))
    gla = dict(w_in=bf(w_gla_in), wa2=bf(w_gla_a2), ba=vec(b_gla_a), gain=vec(gla_norm))
    conv = dict(w_in=bf(w_conv_in), cw=conv_w.astype(F32), cb=vec(conv_b))
    return dict(post=post, mix_out=mix_out, gmix0=vec(norm_mix[0]), dsa=dsa, gla=gla, conv=conv)


def kernel(x_prompt, x_sample, p_prompt, p_sample, state_pool, cache_k, cache_v, cache_idx_k, state_gla, state_conv, norm_mix, norm_mlp, norm_ple, w_mlp1, w_mlp2, w_ple_proj, w_ple_gate, w_pool, b_pool, pool_scale, w_dsa_in, w_dsa_out, q_norm, k_norm, rel_bias, w_gla_in, w_gla_a2, b_gla_a, gla_norm, w_gla_out, w_conv_in, conv_w, conv_b, w_conv_out):
    w = _prepare_weights(norm_mix, norm_mlp, norm_ple, w_mlp1, w_mlp2, w_ple_proj, w_ple_gate, w_pool, b_pool,
                         pool_scale, w_dsa_in, w_dsa_out, q_norm, k_norm, rel_bias, w_gla_in, w_gla_a2, b_gla_a,
                         gla_norm, w_gla_out, w_conv_in, conv_w, conv_b, w_conv_out)
    bp = x_prompt.shape[0]
    dt = x_prompt.dtype
    y_p, pool_p, k_p, v_p, ik_p, gla_p, conv_p = _trunk(
        x_prompt, p_prompt,
        jnp.zeros((bp, POOL_CTX, D_MODEL), dt),
        jnp.zeros((bp, 0, N_KV_HEADS, HEAD_DIM), dt),
        jnp.zeros((bp, 0, N_KV_HEADS, HEAD_DIM), dt),
        jnp.zeros((bp, 0, IDX_DIM), dt),
        jnp.zeros((bp, GLA_HEADS, GLA_DK, GLA_DV), dt),
        jnp.zeros((bp, CONV_W - 1, D_MODEL), dt),
        w)
    y_s, pool_s, k_s, v_s, ik_s, gla_s, conv_s = _trunk(
        x_sample, p_sample, state_pool, cache_k, cache_v, cache_idx_k, state_gla, state_conv, w)
    return (y_p, y_s, pool_p, pool_s, k_p, v_p, ik_p, k_s, v_s, ik_s, gla_p, gla_s, conv_p, conv_s)
```

```python
import functools

import jax
import jax.numpy as jnp
from jax import lax
from jax.experimental import pallas as pl
from jax.experimental.pallas import tpu as pltpu

F32 = jnp.float32
BF16 = jnp.bfloat16
I32 = jnp.int32

D_MODEL = 1024
D_FF = 4 * D_MODEL
PLE_DIM = 256
EPS = 1e-6
CHUNK = 64
CHUNK_SHIFT = CHUNK.bit_length() - 1
POOL_WINDOWS = (2, 4, 8, 16)
POOL_GC = D_MODEL // len(POOL_WINDOWS)
POOL_CTX = max(POOL_WINDOWS) - 1
POOL_ROW_BLOCK = 128
N_HEADS = 8
N_KV_HEADS = 2
HEAD_DIM = D_MODEL // N_HEADS
GROUP = N_HEADS // N_KV_HEADS
IDX_HEADS = 8
IDX_DIM = 64
TOPK_MAX = 256
REL_BUCKETS = 32
REL_LOG_EDGES = (12, 16, 23, 32, 46, 64, 91)
REL_FAR_BUCKET = REL_BUCKETS // 2 - 1
GLA_HEADS = 4
GLA_DK = D_MODEL // 2 // GLA_HEADS
GLA_DV = D_MODEL // GLA_HEADS
GLA_RANK = 16
GLA_GATE_NORM = 16.0
GLA_SUB = 16
CONV_W = 3

V7X_VMEM_LIMIT_BYTES = 56 * 1024 * 1024
SUBLANES = 8
BF16_SUBLANES = 16
HIGH_HALF = -(2 ** 16)
F32_MIN_NORMAL_BITS = 0x00800000
F32_MIN_NORMAL = 2.0 ** -126
INT_MIN = -(2 ** 31)
QUERY_BLOCK = 256
SHARED_QUERY_BLOCK = 128
KEY_TILE = 256

HIGHEST = lax.Precision.HIGHEST
LOG2_E = 1.4426950408889634


def _params(*sem):
    return pltpu.CompilerParams(dimension_semantics=sem, vmem_limit_bytes=V7X_VMEM_LIMIT_BYTES)


def _const_spec(shape):
    nd = len(shape)
    return pl.BlockSpec(shape, lambda *_: (0,) * nd, pipeline_mode=pl.Buffered(1))


def _rms(x, g):
    return x * lax.rsqrt(jnp.mean(x * x, axis=-1, keepdims=True) + EPS) * g


def _dot(a, b):
    return jnp.dot(a, b, preferred_element_type=F32)


def _dot_nt(a, b):
    return lax.dot_general(a, b, (((1,), (1,)), ((), ())), preferred_element_type=F32)


def _dot_tn(a, b):
    return lax.dot_general(a, b, (((0,), (0,)), ((), ())), preferred_element_type=F32)


def _fold_rows(x, op, tile_rows=SUBLANES):
    n = x.shape[0] // tile_rows
    assert n & (n - 1) == 0
    parts = x.reshape(n, tile_rows, x.shape[1])
    while n > 1:
        n //= 2
        parts = op(parts[:n], parts[n:])
    return parts[0]


def _row_tile(n, want):
    t = min(n, want)
    assert n % t == 0
    return t


def _rows_per_step(b, t, tt, max_tokens):
    if t != tt:
        return 1
    return max(d for d in range(1, b + 1) if b % d == 0 and d * tt <= max(tt, max_tokens))


def _pool_step(x, g, prev_ref, row, xe_ref, is_first, pos_first):
    tt = x.shape[0]
    ctx = POOL_CTX + 1

    @pl.when(is_first)
    def _():
        xe_ref[0:1, :] = jnp.zeros((1, D_MODEL), F32)
        xe_ref[1:ctx, :] = prev_ref[row]

    xe_ref[ctx:ctx + tt, :] = _rms(x, g)
    rb = min(tt, POOL_ROW_BLOCK)
    kk = rb + ctx
    t_i = lax.broadcasted_iota(I32, (rb, kk), 0) + ctx
    c_i = lax.broadcasted_iota(I32, (rb, kk), 1)
    bands = [((c_i <= t_i) & (c_i > t_i - w)).astype(BF16) for w in POOL_WINDOWS]
    blocks = []
    for r in range(tt // rb):
        slab = xe_ref[r * rb:r * rb + kk, :]
        hi = slab.astype(BF16)
        lo = (slab - hi.astype(F32)).astype(BF16)
        pos1 = (pos_first + 1 + r * rb + lax.broadcasted_iota(I32, (rb, 1), 0)).astype(F32)
        parts = []
        for gi, w in enumerate(POOL_WINDOWS):
            cols = slice(gi * POOL_GC, (gi + 1) * POOL_GC)
            win = _dot(bands[gi], hi[:, cols]) + _dot(bands[gi], lo[:, cols])
            parts.append((win / jnp.minimum(float(w), pos1) - slab[ctx:, cols]).astype(BF16))
        blocks.append(jnp.concatenate(parts, axis=1))
    xe_ref[0:ctx, :] = xe_ref[tt:tt + ctx, :]
    return jnp.concatenate(blocks, axis=0)


def _post_kernel(h_ref, a_ref, p_ref, wout_ref, bout_ref, sout_ref, gmlp_ref, w1_ref, w2_ref, gple_ref, wg_ref,
                 wp_ref, *rest, ff_chunk, with_next):
    if with_next:
        gnext_ref, h_out_ref, xn_out_ref = rest
    else:
        (h_out_ref,) = rest
    if len(wout_ref.shape) == 3:
        gc = wout_ref.shape[1]
        y = jnp.concatenate([_dot(a_ref[:, g * gc:(g + 1) * gc], wout_ref[g]) for g in range(wout_ref.shape[0])],
                            axis=1)
    else:
        y = _dot(a_ref[...], wout_ref[...])
    y = (y + bout_ref[...]) * sout_ref[...]
    h1 = h_ref[...] + y
    hn = _rms(h1, gmlp_ref[...]).astype(BF16)
    acc = h1
    for c in range(D_FF // ff_chunk):
        cols = slice(c * ff_chunk, (c + 1) * ff_chunk)
        hid = jnp.square(jnp.maximum(_dot(hn, w1_ref[:, cols]), 0.0)).astype(BF16)
        acc = acc + _dot(hid, w2_ref[cols, :])
    gate = jax.nn.sigmoid(_dot(_rms(acc, gple_ref[...]).astype(BF16), wg_ref[...]))
    h3 = acc + _dot(p_ref[...].astype(BF16), wp_ref[...]) * gate
    h_out_ref[...] = h3
    if with_next:
        xn_out_ref[...] = _rms(h3, gnext_ref[...]).astype(BF16)


def _layer_spec(shape, layer):
    nd = len(shape)
    return pl.BlockSpec((None,) + tuple(shape), lambda *_: (layer,) + (0,) * nd, pipeline_mode=pl.Buffered(1))


def _post_block(h, a, p, pw, layer, wout, bout, sout):
    n = h.shape[0]
    tm = _row_tile(n, 512)
    depth = pw["w1"].shape[0]
    with_next = layer + 1 < depth
    row = lambda w: pl.BlockSpec((tm, w), lambda i: (i, 0))
    vec = _const_spec((1, D_MODEL))
    lvec = lambda l: _layer_spec((1, D_MODEL), l)
    out_shape = [jax.ShapeDtypeStruct((n, D_MODEL), F32)]
    out_specs = [row(D_MODEL)]
    in_specs = [row(D_MODEL), row(D_MODEL), pl.BlockSpec((None, tm, PLE_DIM), lambda i: (layer, i, 0)),
                _const_spec(wout.shape), vec, vec, lvec(layer),
                _layer_spec((D_MODEL, D_FF), layer), _layer_spec((D_FF, D_MODEL), layer), lvec(layer),
                _layer_spec((D_MODEL, D_MODEL), layer), _layer_spec((PLE_DIM, D_MODEL), layer)]
    args = [h, a, p, wout, bout, sout, pw["gmlp"], pw["w1"], pw["w2"], pw["gple"], pw["wg"], pw["wp"]]
    if with_next:
        out_shape.append(jax.ShapeDtypeStruct((n, D_MODEL), BF16))
        out_specs.append(row(D_MODEL))
        in_specs.append(lvec(layer + 1))
        args.append(pw["gmix"])
    outs = pl.pallas_call(
        functools.partial(_post_kernel, ff_chunk=1024, with_next=with_next),
        grid=(n // tm,),
        in_specs=in_specs,
        out_specs=out_specs,
        out_shape=out_shape,
        compiler_params=_params("parallel"),
        name="post_block",
    )(*args)
    return (outs[0], outs[1]) if with_next else (outs[0], None)


def _pool_kernel(x_ref, prev_ref, g_ref, a_ref, pool_ref, xe_ref, *, tt, pos0):
    t = pl.program_id(1)
    for row in range(x_ref.shape[0]):
        a_ref[row] = _pool_step(x_ref[row], g_ref[...], prev_ref, row, xe_ref, t == 0, pos0 + t * tt)

        @pl.when(t == pl.num_programs(1) - 1)
        def _():
            pool_ref[row] = xe_ref[1:POOL_CTX + 1, :]


def _pool_front(x, prev, g, pos0):
    b, t, _ = x.shape
    tt = _row_tile(t, 512)
    bb = _rows_per_step(b, t, tt, 256)
    return pl.pallas_call(
        functools.partial(_pool_kernel, tt=tt, pos0=pos0),
        grid=(b // bb, t // tt),
        in_specs=[pl.BlockSpec((bb, tt, D_MODEL), lambda i, j: (i, j, 0)),
                  pl.BlockSpec((bb, POOL_CTX, D_MODEL), lambda i, j: (i, 0, 0)),
                  _const_spec((1, D_MODEL))],
        out_specs=[pl.BlockSpec((bb, tt, D_MODEL), lambda i, j: (i, j, 0)),
                   pl.BlockSpec((bb, POOL_CTX, D_MODEL), lambda i, j: (i, 0, 0))],
        out_shape=[jax.ShapeDtypeStruct((b, t, D_MODEL), BF16),
                   jax.ShapeDtypeStruct((b, POOL_CTX, D_MODEL), F32)],
        scratch_shapes=[pltpu.VMEM((POOL_CTX + 1 + tt, D_MODEL), F32)],
        compiler_params=_params("arbitrary", "arbitrary"),
        name="pool_front",
    )(x, prev, g)


DSA_KV = N_KV_HEADS * HEAD_DIM
DSA_IQ = IDX_HEADS * IDX_DIM
DSA_COLS = (0, D_MODEL, D_MODEL + DSA_KV, D_MODEL + 2 * DSA_KV, D_MODEL + 2 * DSA_KV + DSA_IQ,
            D_MODEL + 2 * DSA_KV + DSA_IQ + IDX_DIM, D_MODEL + 2 * DSA_KV + DSA_IQ + IDX_DIM + IDX_HEADS)


def _dsa_in_kernel(xn_ref, w_ref, qg_ref, kg_ref, q_ref, iq_ref, iw_ref, k2_ref, v2_ref, ik_ref, *key_refs,
                   tm, for_attn):
    x = xn_ref[...]
    qb = QUERY_BLOCK
    assert tm % qb == 0 or not for_attn
    if for_attn:
        kb_ref, vt_ref, ikb_ref = key_refs
    col = lambda i: slice(DSA_COLS[i], DSA_COLS[i + 1])
    for h in range(N_HEADS):
        if h % 2 == 0:
            q2 = _dot(x, w_ref[:, DSA_COLS[0] + h * HEAD_DIM:DSA_COLS[0] + (h + 2) * HEAD_DIM])
            iq2 = _dot(x, w_ref[:, DSA_COLS[3] + h * IDX_DIM:DSA_COLS[3] + (h + 2) * IDX_DIM]).astype(BF16)
        qh = _rms(q2[:, (h % 2) * HEAD_DIM:(h % 2 + 1) * HEAD_DIM], qg_ref[...]).astype(BF16)
        iqh = iq2[:, (h % 2) * IDX_DIM:(h % 2 + 1) * IDX_DIM]
        if for_attn:
            for j in range(tm // qb):
                q_ref[j, h] = qh[j * qb:(j + 1) * qb]
                iq_ref[j, h] = iqh[j * qb:(j + 1) * qb]
        else:
            q_ref[:, h * HEAD_DIM:(h + 1) * HEAD_DIM] = qh
            iq_ref[:, h * IDX_DIM:(h + 1) * IDX_DIM] = iqh
    k = _dot(x, w_ref[:, col(1)])
    v = _dot(x, w_ref[:, col(2)])
    tail = _dot(x, w_ref[:, DSA_COLS[4]:DSA_COLS[6]])
    ik = tail[:, :IDX_DIM]
    ik_ref[...] = ik
    iw_ref[...] = tail[:, IDX_DIM:]
    for h in range(N_KV_HEADS):
        cols = slice(h * HEAD_DIM, (h + 1) * HEAD_DIM)
        kh = _rms(k[:, cols], kg_ref[...])
        k2_ref[pl.ds(h, tm, stride=N_KV_HEADS), :] = kh
        v2_ref[pl.ds(h, tm, stride=N_KV_HEADS), :] = v[:, cols]
        if for_attn:
            kb_ref[:, cols] = kh.astype(BF16)
            for j in range(tm // KEY_TILE):
                vt_ref[j, h] = v[j * KEY_TILE:(j + 1) * KEY_TILE, cols].T.astype(BF16)
    if for_attn:
        ikb_ref[...] = ik.astype(BF16)


def _dsa_in(xn, w, for_attn):
    n = xn.shape[0]
    tm = _row_tile(n, 1024)
    qb = QUERY_BLOCK
    row = lambda wd: pl.BlockSpec((tm, wd), lambda i: (i, 0))
    lead = lambda shape: pl.BlockSpec(shape, lambda i: (i,) + (0,) * (len(shape) - 1))
    if for_attn:
        assert tm % KEY_TILE == 0 and tm % qb == 0
        q_specs = [lead((tm // qb, N_HEADS, qb, HEAD_DIM)), lead((tm // qb, IDX_HEADS, qb, IDX_DIM))]
        q_shapes = [jax.ShapeDtypeStruct((n // qb, N_HEADS, qb, HEAD_DIM), BF16),
                    jax.ShapeDtypeStruct((n // qb, IDX_HEADS, qb, IDX_DIM), BF16)]
        key_specs = [row(DSA_KV), lead((tm // KEY_TILE, N_KV_HEADS, HEAD_DIM, KEY_TILE)), row(IDX_DIM)]
        key_shapes = [jax.ShapeDtypeStruct((n, DSA_KV), BF16),
                      jax.ShapeDtypeStruct((n // KEY_TILE, N_KV_HEADS, HEAD_DIM, KEY_TILE), BF16),
                      jax.ShapeDtypeStruct((n, IDX_DIM), BF16)]
    else:
        q_specs = [row(D_MODEL), row(DSA_IQ)]
        q_shapes = [jax.ShapeDtypeStruct((n, D_MODEL), BF16), jax.ShapeDtypeStruct((n, DSA_IQ), BF16)]
        key_specs, key_shapes = [], []
    kv2 = pl.BlockSpec((N_KV_HEADS * tm, HEAD_DIM), lambda i: (i, 0))
    return pl.pallas_call(
        functools.partial(_dsa_in_kernel, tm=tm, for_attn=for_attn),
        grid=(n // tm,),
        in_specs=[row(D_MODEL), _const_spec((D_MODEL, DSA_COLS[-1])), _const_spec((1, HEAD_DIM)),
                  _const_spec((1, HEAD_DIM))],
        out_specs=q_specs + [row(IDX_HEADS), kv2, kv2, row(IDX_DIM)] + key_specs,
        out_shape=q_shapes + [jax.ShapeDtypeStruct((n, IDX_HEADS), F32),
                              jax.ShapeDtypeStruct((N_KV_HEADS * n, HEAD_DIM), F32),
                              jax.ShapeDtypeStruct((N_KV_HEADS * n, HEAD_DIM), F32),
                              jax.ShapeDtypeStruct((n, IDX_DIM), F32)] + key_shapes,
        compiler_params=_params("parallel"),
        name="dsa_in",
    )(xn, w["w_in"], w["qg"], w["kg"])


def _dsa_keys_kernel(ck_ref, cv_ref, k2_ref, v2_ref, kb_ref, vt_ref, *, past, t, nb):
    lp = kb_ref.shape[1]
    nh = N_KV_HEADS
    for r in range(nb):
        for h in range(nh):
            cols = slice((h * nb + r) * HEAD_DIM, (h * nb + r + 1) * HEAD_DIM)
            keys = slice(r * KEY_TILE, (r + 1) * KEY_TILE)
            kb_ref[0, 0:past, cols] = ck_ref[r, pl.ds(h, past, stride=nh), :].astype(BF16)
            kb_ref[0, past:past + t, cols] = k2_ref[r, pl.ds(h, t, stride=nh), :].astype(BF16)
            kb_ref[0, past + t:lp, cols] = jnp.zeros((lp - past - t, HEAD_DIM), BF16)
            for j in range(past // KEY_TILE):
                v_old = cv_ref[r, pl.ds(h + nh * KEY_TILE * j, KEY_TILE, stride=nh), :]
                vt_ref[0, j, h, :, keys] = v_old.T.astype(BF16)
            v_new = jnp.concatenate([v2_ref[r, pl.ds(h, t, stride=nh), :],
                                     jnp.zeros((KEY_TILE - t, HEAD_DIM), F32)], axis=0)
            vt_ref[0, past // KEY_TILE, h, :, keys] = v_new.T.astype(BF16)


def _dsa_keys(cache_k, cache_v, k2, v2, nb):
    b, past = cache_k.shape[:2]
    nh = N_KV_HEADS
    t = k2.shape[1] // nh
    lp = past + KEY_TILE
    assert past % KEY_TILE == 0 and t <= KEY_TILE and t % SUBLANES == 0 and b % nb == 0
    rows = lambda *shape: pl.BlockSpec((nb,) + shape, lambda i: (i,) + (0,) * len(shape))
    group = lambda *shape: pl.BlockSpec((1,) + shape, lambda i: (i,) + (0,) * len(shape))
    return pl.pallas_call(
        functools.partial(_dsa_keys_kernel, past=past, t=t, nb=nb),
        grid=(b // nb,),
        in_specs=[rows(nh * past, HEAD_DIM), rows(nh * past, HEAD_DIM), rows(nh * t, HEAD_DIM),
                  rows(nh * t, HEAD_DIM)],
        out_specs=[group(lp, nb * DSA_KV), group(lp // KEY_TILE, nh, HEAD_DIM, nb * KEY_TILE)],
        out_shape=[jax.ShapeDtypeStruct((b // nb, lp, nb * DSA_KV), BF16),
                   jax.ShapeDtypeStruct((b // nb, lp // KEY_TILE, nh, HEAD_DIM, nb * KEY_TILE), BF16)],
        compiler_params=_params("parallel"),
        name="dsa_keys",
    )(cache_k.reshape(b, nh * past, HEAD_DIM), cache_v.reshape(b, nh * past, HEAD_DIM), k2, v2)


def _dsa_attn_kernel(rb_ref, q_ref, iq_ref, iwt_ref, kb_ref, vt_ref, ik_ref, o_ref,
                     key_ref, msk_ref, lg_ref, bias_ref, j_ref, hi_ref, *, past, n_keys, topk, idx_bits, nb):
    kt_sz = KEY_TILE
    qb = o_ref.shape[1]
    near_slots = bias_ref.shape[1]
    qpb = qb // nb
    b = pl.program_id(0)
    i = pl.program_id(1)
    q0 = past + i * qpb
    kend = jnp.minimum(n_keys, (((q0 + qpb - 1) >> CHUNK_SHIFT) + 1) * CHUNK)
    nkt = (kend + kt_sz - 1) // kt_sz
    row_i = lax.broadcasted_iota(I32, (kt_sz, qb), 0)
    lane_i = lax.broadcasted_iota(I32, (kt_sz, qb), 1)
    col_i = lane_i % qpb

    @pl.when((b == 0) & (i == 0))
    def _():
        for w in range(1, near_slots):
            rel = row_i - col_i + (w - (near_slots - 1)) * qb
            n = jnp.abs(rel)
            log_bucket = jnp.full_like(n, REL_BUCKETS // 4)
            for edge in REL_LOG_EDGES:
                log_bucket = log_bucket + (n >= edge).astype(I32)
            bucket = jnp.where(n < REL_BUCKETS // 4, n, log_bucket) + jnp.where(rel > 0, REL_BUCKETS // 2, 0)
            for h in range(N_HEADS):
                val = jnp.zeros((kt_sz, qb), F32)
                for bk in range(REL_BUCKETS):
                    val = jnp.where(bucket == bk, rb_ref[bk, h], val)
                bias_ref[h, w] = val * LOG2_E
        for h in range(N_HEADS):
            bias_ref[h, 0] = jnp.full((kt_sz, qb), rb_ref[REL_FAR_BUCKET, h], F32) * LOG2_E

    iw = iwt_ref[0] * (IDX_DIM ** -0.5)
    iq_all = iq_ref[0, 0].reshape(IDX_HEADS * qb, nb * IDX_DIM)
    q_chunk = (q0 + col_i) >> CHUNK_SHIFT

    def tile_rows(jt):
        return pl.ds(pl.multiple_of(jt * kt_sz, kt_sz), kt_sz)

    def score_tile(jt, carry):
        ikt = ik_ref[0, tile_rows(jt), :]
        s = jnp.zeros((kt_sz, qb), F32)
        for hp in range(IDX_HEADS // 2):
            s2 = _dot_nt(ikt, iq_all[2 * hp * qb:(2 * hp + 2) * qb, :])
            for h in (2 * hp, 2 * hp + 1):
                s = s + iw[h:h + 1, :] * jnp.maximum(s2[:, (h - 2 * hp) * qb:(h - 2 * hp + 1) * qb], 0.0)
        s = jnp.where(jnp.abs(s) < F32_MIN_NORMAL, 0.0, s)
        bits = pltpu.bitcast(s, I32)
        key = bits ^ ((bits >> 31) & 0x7FFFFFFF)
        kpos = jt * kt_sz + row_i
        adm = ((kpos >> CHUNK_SHIFT) <= q_chunk) & (kpos < n_keys)
        key_ref[tile_rows(jt), :] = jnp.where(adm, key, INT_MIN)
        s_hi = pltpu.bitcast(bits & HIGH_HALF, F32)
        hi_ref[tile_rows(jt), :] = jnp.where(adm, s_hi, -jnp.inf).astype(BF16)
        return carry

    lax.fori_loop(0, nkt, score_tile, 0)

    def over_tiles(body, carry):
        carry = lax.fori_loop(0, nkt // 2, lambda j2, c: body(2 * j2, 2, c), carry)
        return lax.cond(nkt % 2 == 1, lambda c: body(nkt - 1, 1, c), lambda c: c, carry)

    def count(pred):
        def body(jt, acc):
            hit = jnp.where(pred(key_ref[tile_rows(jt), :], jt * kt_sz + row_i), 1.0, 0.0)
            return acc + _fold_rows(hit, jnp.add)

        acc = lax.fori_loop(0, nkt, body, jnp.zeros((8, qb), F32))
        return jnp.sum(acc, axis=0, keepdims=True)

    kf = float(topk)

    def count_high(cand):
        cand_bits = (cand ^ ((cand >> 31) & 0x7FFFFFFF)) & HIGH_HALF
        cand_bits = jnp.where((cand_bits > 0) & (cand_bits < F32_MIN_NORMAL_BITS), F32_MIN_NORMAL_BITS, cand_bits)
        cand_hi = jnp.broadcast_to(pltpu.bitcast(cand_bits, F32).astype(BF16), (kt_sz, qb))
        one = jnp.ones((kt_sz, qb), BF16)
        zero = jnp.zeros((kt_sz, qb), BF16)

        def body(jt, acc):
            hit = jnp.where(hi_ref[tile_rows(jt), :] >= cand_hi, one, zero)
            return acc + _fold_rows(hit, jnp.add, BF16_SUBLANES).astype(F32)

        acc = lax.fori_loop(0, nkt, body, jnp.zeros((BF16_SUBLANES, qb), F32))
        return jnp.sum(acc, axis=0, keepdims=True)

    n_pos = count_high(jnp.zeros((1, qb), I32))
    thr0 = jnp.where(n_pos >= kf, 0, INT_MIN).astype(I32)
    n_ge0 = n_pos

    def bisect(counter, p, state):
        thr, n_ge = state
        cand = thr | jnp.left_shift(jnp.int32(1), 30 - p)
        n_cand = counter(cand)
        take = n_cand >= kf
        return jnp.where(take, cand, thr), jnp.where(take, n_cand, n_ge)

    high_passes = 15
    state = lax.fori_loop(0, high_passes, functools.partial(bisect, count_high), (thr0, n_ge0))
    thr, n_ge = lax.fori_loop(high_passes, 31,
                              functools.partial(bisect, lambda cand: count(lambda kk, kpos: kk >= cand)), state)
    has_thr = thr > INT_MIN
    j_ref[...] = jnp.where(has_thr, n_keys, -1).astype(I32)
    tied_cut = jnp.max(jnp.where(has_thr & (n_ge > kf), 1, 0)) > 0

    @pl.when(tied_cut)
    def _():
        need = kf - count(lambda kk, jt: kk > thr)
        lo = jnp.zeros((1, qb), I32)
        for bit in range(idx_bits - 1, -1, -1):
            cand = lo + (1 << bit)
            below = count(lambda kk, kpos: (kk == thr) & (kpos < cand))
            lo = jnp.where(below < need, cand, lo)
        j_ref[...] = jnp.where(has_thr, lo, -1)

    j_last = j_ref[...]

    def mask_tile(jt, carry):
        kk = key_ref[tile_rows(jt), :]
        sel = (kk > thr) | ((kk == thr) & ((jt * kt_sz + row_i) <= j_last))
        msk_ref[tile_rows(jt), :] = jnp.where(sel, 0.0, -jnp.inf)
        return carry

    lax.fori_loop(0, nkt, mask_tile, 0)

    gq = GROUP * qb
    kdim = nb * HEAD_DIM
    slot0 = q0 // qb - (near_slots - 1)

    def bias_slot(jt):
        w = jt * (kt_sz // qb) - slot0
        return jnp.where(w >= 1, w, 0)

    kv_heads = range(N_KV_HEADS)
    q_groups = [q_ref[0, 0, kvh * GROUP:(kvh + 1) * GROUP].reshape(gq, kdim) for kvh in kv_heads]

    def logits_tile(jt, m8):
        msk = msk_ref[tile_rows(jt), :]
        w = bias_slot(jt)
        new_m8 = []
        for kvh in kv_heads:
            kt = kb_ref[0, tile_rows(jt), kvh * kdim:(kvh + 1) * kdim]
            lg = _dot_nt(kt, q_groups[kvh]) * (HEAD_DIM ** -0.5 * LOG2_E)
            parts = []
            for g in range(GROUP):
                h = kvh * GROUP + g
                lgh = lg[:, g * qb:(g + 1) * qb] + bias_ref[h, w] + msk
                lg_ref[tile_rows(jt), h * qb:(h + 1) * qb] = lgh
                parts.append(_fold_rows(lgh, jnp.maximum))
            new_m8.append(jnp.maximum(m8[kvh], jnp.concatenate(parts, axis=1)))
        return tuple(new_m8)

    m8 = lax.fori_loop(0, nkt, logits_tile, tuple(jnp.full((8, gq), -jnp.inf, F32) for _ in kv_heads))
    m = [jnp.max(m8[kvh], axis=0, keepdims=True) for kvh in kv_heads]

    lane_row_g = lax.broadcasted_iota(I32, (HEAD_DIM, gq), 1) % qb // qpb

    def pv_tiles(jt, n_tiles, carry):
        rows = pl.ds(pl.multiple_of(jt * kt_sz, kt_sz), n_tiles * kt_sz)
        new = []
        for kvh in kv_heads:
            acc, l8 = carry[kvh]
            p = jnp.exp2(lg_ref[rows, kvh * gq:(kvh + 1) * gq] - m[kvh])
            pb = p.astype(BF16)

            def values(r):
                tiles = [vt_ref[0, jt + u, kvh, :, r * kt_sz:(r + 1) * kt_sz] for u in range(n_tiles)]
                return tiles[0] if n_tiles == 1 else jnp.concatenate(tiles, axis=1)

            pv = _dot(values(0), pb)
            for r in range(1, nb):
                pv = jnp.where(lane_row_g == r, _dot(values(r), pb), pv)
            new.append((acc + pv, l8 + _fold_rows(p, jnp.add)))
        return tuple(new)

    zero = (jnp.zeros((HEAD_DIM, gq), F32), jnp.zeros((8, gq), F32))
    acc_l8 = over_tiles(pv_tiles, tuple(zero for _ in kv_heads))
    for kvh in kv_heads:
        acc, l8 = acc_l8[kvh]
        o_t = acc / jnp.sum(l8, axis=0, keepdims=True)
        for g in range(GROUP):
            h = kvh * GROUP + g
            o_ref[0, :, h * HEAD_DIM:(h + 1) * HEAD_DIM] = o_t[:, g * qb:(g + 1) * qb].T.astype(BF16)


def _dsa_attn(rel_bias, q, iq, iwt, kb, vt, ikb, *, past, n_keys):
    g, nblk = q.shape[:2]
    nb = q.shape[-1] // HEAD_DIM
    lp = kb.shape[1]
    qb = q.shape[3]
    near_slots = KEY_TILE // qb + 2
    assert lp % KEY_TILE == 0 and past % KEY_TILE == 0 and KEY_TILE % qb == 0 and qb >= REL_LOG_EDGES[-1]
    assert qb % nb == 0 and (nb == 1 or nblk == 1)
    topk = min(TOPK_MAX, n_keys // 4)
    return pl.pallas_call(
        functools.partial(_dsa_attn_kernel, past=past, n_keys=n_keys, topk=topk,
                          idx_bits=max(1, (lp - 1).bit_length()), nb=nb),
        grid=(g, nblk),
        in_specs=[pl.BlockSpec(memory_space=pltpu.SMEM),
                  pl.BlockSpec((1, 1, N_HEADS, qb, nb * HEAD_DIM), lambda i, j: (i, j, 0, 0, 0)),
                  pl.BlockSpec((1, 1, IDX_HEADS, qb, nb * IDX_DIM), lambda i, j: (i, j, 0, 0, 0)),
                  pl.BlockSpec((1, IDX_HEADS, qb), lambda i, j: (i, 0, j)),
                  pl.BlockSpec((1, lp, nb * DSA_KV), lambda i, j: (i, 0, 0)),
                  pl.BlockSpec((1, lp // KEY_TILE, N_KV_HEADS, HEAD_DIM, nb * KEY_TILE),
                               lambda i, j: (i, 0, 0, 0, 0)),
                  pl.BlockSpec((1, lp, nb * IDX_DIM), lambda i, j: (i, 0, 0))],
        out_specs=pl.BlockSpec((1, qb, D_MODEL), lambda i, j: (i, j, 0)),
        out_shape=jax.ShapeDtypeStruct((g, nblk * qb, D_MODEL), BF16),
        scratch_shapes=[pltpu.VMEM((lp, qb), I32), pltpu.VMEM((lp, qb), F32), pltpu.VMEM((lp, N_HEADS * qb), F32),
                        pltpu.VMEM((N_HEADS, near_slots, KEY_TILE, qb), F32), pltpu.VMEM((1, qb), I32),
                        pltpu.VMEM((lp, qb), BF16)],
        compiler_params=_params("arbitrary", "arbitrary"),
        name="dsa_attn",
    )(rel_bias, q, iq, iwt, kb, vt, ikb)


def _dsa_mixer(xn, k_past, v_past, ik_past, w, b, t):
    past = k_past.shape[1]
    n_keys = past + t
    nh = N_KV_HEADS
    if past == 0:
        qb = QUERY_BLOCK
        assert t % KEY_TILE == 0 and t % qb == 0
        q, iq, iw, k2, v2, ik, kb, vt, ikb = _dsa_in(xn, w, for_attn=True)
        q = q.reshape(b, t // qb, N_HEADS, qb, HEAD_DIM)
        iq = iq.reshape(b, t // qb, IDX_HEADS, qb, IDX_DIM)
        iwt = iw.reshape(b, t, IDX_HEADS).transpose(0, 2, 1)
        kb = kb.reshape(b, t, DSA_KV)
        vt = vt.reshape(b, t // KEY_TILE, nh, HEAD_DIM, KEY_TILE)
        ikb = ikb.reshape(b, t, IDX_DIM)
    else:
        qb = SHARED_QUERY_BLOCK
        assert qb % t == 0 and b % (qb // t) == 0
        nb = qb // t
        g = b // nb
        q, iq, iw, k2, v2, ik = _dsa_in(xn, w, for_attn=False)
        eye = jnp.eye(nb, dtype=BF16)

        def own_slice(z, n_h):
            d = z.shape[-1] // n_h
            z = z.reshape(g, nb, t, n_h, d).transpose(0, 3, 1, 2, 4)
            z = z[:, :, :, :, None, :] * eye[None, None, :, None, :, None]
            return z.reshape(g, 1, n_h, qb, nb * d)

        q, iq = own_slice(q, N_HEADS), own_slice(iq, IDX_HEADS)
        iwt = iw.reshape(g, qb, IDX_HEADS).transpose(0, 2, 1)
        kb, vt = _dsa_keys(k_past, v_past, k2.reshape(b, nh * t, HEAD_DIM), v2.reshape(b, nh * t, HEAD_DIM), nb)
        lp = kb.shape[1]
        ikb = jnp.concatenate([ik_past.astype(BF16), ik.reshape(b, t, IDX_DIM).astype(BF16),
                               jnp.zeros((b, lp - n_keys, IDX_DIM), BF16)], axis=1)
        ikb = ikb.reshape(g, nb, lp, IDX_DIM).transpose(0, 2, 1, 3).reshape(g, lp, nb * IDX_DIM)
    o = _dsa_attn(w["rel_bias"], q, iq, iwt, kb, vt, ikb, past=past, n_keys=n_keys)
    o = o.reshape(b * t, D_MODEL)
    return (o, k2.reshape(b, t, nh, HEAD_DIM), v2.reshape(b, t, nh, HEAD_DIM), ik.reshape(b, t, IDX_DIM))


GLA_QK = GLA_HEADS * GLA_DK
GLA_COLS = (0, GLA_QK, 2 * GLA_QK, 2 * GLA_QK + D_MODEL, 2 * GLA_QK + 2 * D_MODEL, 2 * GLA_QK + 2 * D_MODEL + GLA_RANK)


def _gla_in_kernel(xn_ref, w_ref, wa2_ref, ba_ref, q_ref, k_ref, v_ref, r_ref, g_ref):
    x = xn_ref[...]
    proj = lambda i: _dot(x, w_ref[:, GLA_COLS[i]:GLA_COLS[i + 1]])
    q_ref[...] = proj(0) * (GLA_DK ** -0.5)
    k_ref[...] = proj(1)
    v_ref[...] = proj(2).astype(BF16)
    r_ref[...] = proj(3)
    z = _dot(proj(4).astype(BF16), wa2_ref[...]) + ba_ref[...]
    g_ref[...] = (jnp.minimum(z, 0.0) - jnp.log1p(jnp.exp(-jnp.abs(z)))) * (1.0 / GLA_GATE_NORM)


def _gla_in(xn, w):
    n = xn.shape[0]
    tm = _row_tile(n, 1024)
    qk = GLA_HEADS * GLA_DK
    row = lambda wd: pl.BlockSpec((tm, wd), lambda i: (i, 0))
    return pl.pallas_call(
        _gla_in_kernel,
        grid=(n // tm,),
        in_specs=[row(D_MODEL), _const_spec((D_MODEL, GLA_COLS[-1])), _const_spec((GLA_RANK, qk)),
                  _const_spec((1, qk))],
        out_specs=[row(qk), row(qk), row(D_MODEL), row(D_MODEL), row(qk)],
        out_shape=[jax.ShapeDtypeStruct((n, qk), F32), jax.ShapeDtypeStruct((n, qk), F32),
                   jax.ShapeDtypeStruct((n, D_MODEL), BF16), jax.ShapeDtypeStruct((n, D_MODEL), F32),
                   jax.ShapeDtypeStruct((n, qk), F32)],
        compiler_params=_params("parallel"),
        name="gla_in",
    )(xn, w["w_in"], w["wa2"], w["ba"])


def _gla_chunk(q, k, g, v, st, tri_c, tri_sb):
    c = q.shape[0]
    sb = GLA_SUB
    cum = jnp.dot(tri_c, g, precision=HIGHEST, preferred_element_type=F32)
    total = cum[c - 1:c, :]
    o_inter = _dot_nt((q * jnp.exp(cum)).astype(BF16), st.astype(BF16))
    st_new = st * jnp.exp(total) + _dot_tn(v, (k * jnp.exp(total - cum)).astype(BF16))
    o_rows = []
    cum2 = cum * LOG2_E
    for i in range(c // sb):
        r = slice(i * sb, (i + 1) * sb)
        ci = cum2[r]
        decay = jnp.exp2(ci[:, None, :] - ci[None, :, :])
        sc = jnp.sum(decay * q[r][:, None, :] * k[r][None, :, :], axis=-1)
        sc = jnp.where(tri_sb, sc, 0.0)
        o_rows.append(o_inter[r] + _dot(sc.astype(BF16), v[r]))
    width = sb
    while width < c:
        for p in range(c // (2 * width)):
            lo = 2 * p * width
            left = slice(lo, lo + width)
            right = slice(lo + width, lo + 2 * width)
            edge = cum[lo + width - 1:lo + width, :]
            q_hat = (q[right] * jnp.exp(cum[right] - edge)).astype(BF16)
            k_hat = (k[left] * jnp.exp(edge - cum[left])).astype(BF16)
            o_pair = _dot(_dot_nt(q_hat, k_hat).astype(BF16), v[left])
            for j in range(width // sb):
                o_rows[(lo + width) // sb + j] += o_pair[j * sb:(j + 1) * sb]
        width *= 2
    return jnp.concatenate(o_rows, axis=0), st_new


def _gla_core_kernel(q_ref, k_ref, g_ref, v_ref, r_ref, s0_ref, gain_ref, a_ref, sout_ref, st_ref, *, tt, c):
    sb = GLA_SUB
    t = pl.program_id(1)
    bb = q_ref.shape[0]
    batch_rows = range(bb)

    @pl.when(t == 0)
    def _():
        for row in batch_rows:
            for h in range(GLA_HEADS):
                st_ref[row * GLA_HEADS + h] = s0_ref[row, h].T

    tri_c = (lax.broadcasted_iota(I32, (c, c), 0) >= lax.broadcasted_iota(I32, (c, c), 1)).astype(F32)
    tri_sb = lax.broadcasted_iota(I32, (sb, sb), 0) >= lax.broadcasted_iota(I32, (sb, sb), 1)
    gain = gain_ref[...]

    per_trip = 4 if (tt // c) % 4 == 0 else 1

    def chunks(ci, carry):
        for u in range(per_trip):
            rows = pl.ds(pl.multiple_of((ci * per_trip + u) * c, c), c)
            for row in batch_rows:
                for h in range(GLA_HEADS):
                    dk = slice(h * GLA_DK, (h + 1) * GLA_DK)
                    dv = slice(h * GLA_DV, (h + 1) * GLA_DV)
                    si = row * GLA_HEADS + h
                    o, st_ref[si] = _gla_chunk(q_ref[row, rows, dk], k_ref[row, rows, dk], g_ref[row, rows, dk],
                                               v_ref[row, rows, dv], st_ref[si], tri_c, tri_sb)
                    r = r_ref[row, rows, dv]
                    a_ref[row, rows, dv] = (_rms(o, gain) * (r * jax.nn.sigmoid(r))).astype(BF16)
        return carry

    lax.fori_loop(0, tt // (c * per_trip), chunks, 0)

    @pl.when(t == pl.num_programs(1) - 1)
    def _():
        for row in batch_rows:
            for h in range(GLA_HEADS):
                sout_ref[row, h] = st_ref[row * GLA_HEADS + h].T


def _gla_core(q, k, g, v, r, s0, gain):
    b, t, _ = q.shape
    tt = _row_tile(t, 256)
    c = min(tt, CHUNK)
    assert tt % c == 0 and c % GLA_SUB == 0 and (c // GLA_SUB) & (c // GLA_SUB - 1) == 0
    bb = _rows_per_step(b, t, tt, 128)
    qk = GLA_HEADS * GLA_DK
    tok = lambda wd: pl.BlockSpec((bb, tt, wd), lambda i, j: (i, j, 0))
    st_spec = pl.BlockSpec((bb, GLA_HEADS, GLA_DK, GLA_DV), lambda i, j: (i, 0, 0, 0))
    return pl.pallas_call(
        functools.partial(_gla_core_kernel, tt=tt, c=c),
        grid=(b // bb, t // tt),
        in_specs=[tok(qk), tok(qk), tok(qk), tok(D_MODEL), tok(D_MODEL), st_spec, _const_spec((1, GLA_DV))],
        out_specs=[tok(D_MODEL), st_spec],
        out_shape=[jax.ShapeDtypeStruct((b, t, D_MODEL), BF16),
                   jax.ShapeDtypeStruct((b, GLA_HEADS, GLA_DK, GLA_DV), F32)],
        scratch_shapes=[pltpu.VMEM((bb * GLA_HEADS, GLA_DV, GLA_DK), F32)],
        compiler_params=_params("arbitrary", "arbitrary"),
        name="gla_core",
    )(q, k, g, v, r, s0, gain)


def _conv_kernel(xn_ref, w_ref, prev_ref, cw_ref, cb_ref, a_ref, new_ref, ue_ref, *, tt):
    t = pl.program_id(1)
    pad = SUBLANES
    lo = pad - (CONV_W - 1)

    bb = xn_ref.shape[0]

    @pl.when(t == 0)
    def _():
        ue_ref[:, lo:pad, :] = prev_ref[...]

    x = xn_ref[...].reshape(bb * tt, D_MODEL)
    proj = lambda i: _dot(x, w_ref[:, i * D_MODEL:(i + 1) * D_MODEL]).reshape(bb, tt, D_MODEL)
    u = proj(1) * proj(2)
    ue_ref[:, pad:pad + tt, :] = u
    conv = cb_ref[...] + cw_ref[CONV_W - 1:CONV_W, :] * u
    for j in range(CONV_W - 1):
        conv = conv + cw_ref[j:j + 1, :] * ue_ref[:, lo + j:lo + j + tt, :]
    a_ref[...] = (proj(0) * conv).astype(BF16)
    tail = ue_ref[:, lo + tt:pad + tt, :]
    ue_ref[:, lo:pad, :] = tail

    @pl.when(t == pl.num_programs(1) - 1)
    def _():
        new_ref[...] = tail


def _conv_front(xn, prev, w):
    b, t, _ = xn.shape
    tt = _row_tile(t, 1024)
    bb = _rows_per_step(b, t, tt, 1024)
    tok = pl.BlockSpec((bb, tt, D_MODEL), lambda i, j: (i, j, 0))
    st = pl.BlockSpec((bb, CONV_W - 1, D_MODEL), lambda i, j: (i, 0, 0))
    w_spec = _const_spec((D_MODEL, 3 * D_MODEL))
    return pl.pallas_call(
        functools.partial(_conv_kernel, tt=tt),
        grid=(b // bb, t // tt),
        in_specs=[tok, w_spec, st, _const_spec((CONV_W, D_MODEL)), _const_spec((1, D_MODEL))],
        out_specs=[tok, st],
        out_shape=[jax.ShapeDtypeStruct((b, t, D_MODEL), BF16),
                   jax.ShapeDtypeStruct((b, CONV_W - 1, D_MODEL), F32)],
        scratch_shapes=[pltpu.VMEM((bb, SUBLANES + tt, D_MODEL), F32)],
        compiler_params=_params("arbitrary", "arbitrary"),
        name="conv_front",
    )(xn, w["w_in"], prev, w["cw"], w["cb"])


def _trunk(x, p, pool_prev, k_past, v_past, ik_past, gla_prev, conv_prev, w):
    b, t, _ = x.shape
    n = b * t
    past = k_past.shape[1]
    flat = lambda z: z.reshape(n, z.shape[-1])
    p = p.reshape(p.shape[0], n, PLE_DIM)
    post = lambda h, a, layer: _post_block(h, a, p, w["post"], layer, *w["mix_out"][layer])

    a, pool_new = _pool_front(x, pool_prev, w["gmix0"], past)
    h, xn = post(flat(x), flat(a), 0)

    a, k_new, v_new, ik_new = _dsa_mixer(xn, k_past, v_past, ik_past, w["dsa"], b, t)
    h, xn = post(h, a, 1)

    q, k, v, r, g = _gla_in(xn, w["gla"])
    seq = lambda z: z.reshape(b, t, z.shape[-1])
    a, gla_new = _gla_core(seq(q), seq(k), seq(g), seq(v), seq(r), gla_prev, w["gla"]["gain"])
    h, xn = post(h, flat(a), 2)

    a, conv_new = _conv_front(seq(xn), conv_prev, w["conv"])
    h, _ = post(h, flat(a), 3)
    return h.reshape(b, t, D_MODEL), pool_new, k_new, v_new, ik_new, gla_new, conv_new


def _prepare_weights(norm_mix, norm_mlp, norm_ple, w_mlp1, w_mlp2, w_ple_proj, w_ple_gate, w_pool, b_pool,
                     pool_scale, w_dsa_in, w_dsa_out, q_norm, k_norm, rel_bias, w_gla_in, w_gla_a2, b_gla_a,
                     gla_norm, w_gla_out, w_conv_in, conv_w, conv_b, w_conv_out):
    bf = lambda z: z.astype(BF16)
    vec = lambda z: z.reshape(1, -1).astype(F32)
    stack_vec = lambda z: z.reshape(z.shape[0], 1, z.shape[1]).astype(F32)
    zeros = jnp.zeros((1, D_MODEL), F32)
    ones = jnp.ones((1, D_MODEL), F32)
    mix_out = [(bf(w_pool), vec(b_pool), vec(pool_scale)), (bf(w_dsa_out), zeros, ones),
               (bf(w_gla_out), zeros, ones), (bf(w_conv_out), zeros, ones)]
    post = dict(gmix=stack_vec(norm_mix), gmlp=stack_vec(norm_mlp), gple=stack_vec(norm_ple), w1=bf(w_mlp1),
                w2=bf(w_mlp2), wg=bf(w_ple_gate), wp=bf(w_ple_proj))

    dsa = dict(w_in=bf(w_dsa_in), qg=vec(q_norm), kg=vec(k_norm), rel_bias=rel_bias.astype(F32))
    gla = dict(w_in=bf(w_gla_in), wa2=bf(w_gla_a2), ba=vec(b_gla_a), gain=vec(gla_norm))
    conv = dict(w_in=bf(w_conv_in), cw=conv_w.astype(F32), cb=vec(conv_b))
    return dict(post=post, mix_out=mix_out, gmix0=vec(norm_mix[0]), dsa=dsa, gla=gla, conv=conv)


def kernel(x_prompt, x_sample, p_prompt, p_sample, state_pool, cache_k, cache_v, cache_idx_k, state_gla, state_conv, norm_mix, norm_mlp, norm_ple, w_mlp1, w_mlp2, w_ple_proj, w_ple_gate, w_pool, b_pool, pool_scale, w_dsa_in, w_dsa_out, q_norm, k_norm, rel_bias, w_gla_in, w_gla_a2, b_gla_a, gla_norm, w_gla_out, w_conv_in, conv_w, conv_b, w_conv_out):
    w = _prepare_weights(norm_mix, norm_mlp, norm_ple, w_mlp1, w_mlp2, w_ple_proj, w_ple_gate, w_pool, b_pool,
                         pool_scale, w_dsa_in, w_dsa_out, q_norm, k_norm, rel_bias, w_gla_in, w_gla_a2, b_gla_a,
                         gla_norm, w_gla_out, w_conv_in, conv_w, conv_b, w_conv_out)
    bp = x_prompt.shape[0]
    dt = x_prompt.dtype
    y_p, pool_p, k_p, v_p, ik_p, gla_p, conv_p = _trunk(
        x_prompt, p_prompt,
        jnp.zeros((bp, POOL_CTX, D_MODEL), dt),
        jnp.zeros((bp, 0, N_KV_HEADS, HEAD_DIM), dt),
        jnp.zeros((bp, 0, N_KV_HEADS, HEAD_DIM), dt),
        jnp.zeros((bp, 0, IDX_DIM), dt),
        jnp.zeros((bp, GLA_HEADS, GLA_DK, GLA_DV), dt),
        jnp.zeros((bp, CONV_W - 1, D_MODEL), dt),
        w)
    y_s, pool_s, k_s, v_s, ik_s, gla_s, conv_s = _trunk(
        x_sample, p_sample, state_pool, cache_k, cache_v, cache_idx_k, state_gla, state_conv, w)
    return (y_p, y_s, pool_p, pool_s, k_p, v_p, ik_p, k_s, v_s, ik_s, gla_p, gla_s, conv_p, conv_s)
```
